```python
import math
import jax, jax.numpy as jnp
from jax import lax
import numpy as np

D_MODEL = 1024
BATCH = 4
SEQ = 4096
DEPTH = 4
DEC_BATCH = 32
DEC_SEQ = 8
PAST_LEN = 8192
PAGE_SIZE = 128

PLE_DIM = 256
HEAD_DIM = 64
NSA_HEADS = 8
NSA_KV_HEADS = 2
NSA_HPG = NSA_HEADS // NSA_KV_HEADS
NSA_WIDTH = NSA_HEADS * HEAD_DIM
CMP_BLOCK = 32
CMP_STRIDE = 16
SLC_BLOCK = 64
N_SELECT = 16
WINDOW = 512
SPARSE_Q_BLOCK = 64
WIN_Q_BLOCK = 128
GLA_HEADS = 4
GLA_DK = 32
GLA_DV = 64
GLA_WIDTH = GLA_HEADS * GLA_DV
GLA_GATE_RANK = 16
GLA_GATE_TAU = 16.0
GLA_CHUNK = 32
S5_GROUPS = 16
S5_GROUP_CH = 16
S5_STATE = 64
S5_WIDTH = S5_GROUPS * S5_GROUP_CH
MIX_WIDTH = NSA_WIDTH + GLA_WIDTH + S5_WIDTH
IN_SPLITS = (NSA_WIDTH, 6 * NSA_KV_HEADS * HEAD_DIM, 3 * NSA_HEADS, NSA_WIDTH,
             GLA_HEADS * GLA_DK, GLA_HEADS * GLA_DK, GLA_WIDTH, GLA_GATE_RANK, GLA_WIDTH,
             S5_WIDTH, S5_WIDTH)
IN_WIDTH = (2 * NSA_WIDTH + 6 * NSA_KV_HEADS * HEAD_DIM + 3 * NSA_HEADS + 2 * GLA_HEADS * GLA_DK
            + 2 * GLA_WIDTH + GLA_GATE_RANK + 2 * S5_WIDTH)
RMS_EPS = 1e-6
NEG_BIG = 1e9

kernel_name = 'nsa_gla_s5_hybrid_step'


def rmsnorm(x, g):
    xf = x.astype(jnp.float32)
    y = xf * lax.rsqrt(jnp.mean(xf * xf, axis=-1, keepdims=True) + RMS_EPS)
    return (y * g.astype(jnp.float32)).astype(x.dtype)


def masked_softmax(s, mask):
    s = jnp.where(mask, s, -1e30)
    e = jnp.where(mask, jnp.exp(s - jnp.max(s, axis=-1, keepdims=True)), 0.0)
    return e / jnp.maximum(jnp.sum(e, axis=-1, keepdims=True), 1e-30)


def alibi_slopes():
    return jnp.exp2(-8.0 * jnp.arange(1, NSA_HEADS + 1, dtype=jnp.float32) / NSA_HEADS)


def split_columns(u):
    parts, start = [], 0
    for width in IN_SPLITS:
        parts.append(u[..., start:start + width])
        start += width
    return parts


def compress_kv(x, pe, w1, w2):
    b, t, g, d = x.shape
    ratio = CMP_BLOCK // CMP_STRIDE
    n_ch = t // CMP_STRIDE
    n_cmp = n_ch - ratio + 1
    ch = x[:, :n_ch * CMP_STRIDE].reshape(b, n_ch, CMP_STRIDE, g, d)
    blocks = jnp.concatenate([ch[:, s:s + n_cmp] for s in range(ratio)], axis=2)
    blocks = blocks + pe[None, None, :, None, :]
    flat = jnp.moveaxis(blocks, 2, 3).reshape(b, n_cmp, g, CMP_BLOCK * d)
    return jax.nn.gelu(flat @ w1) @ w2


def nsa_sparse_block(q, qpos, kc, vc, cend, ksb, vsb, slopes):
    f32 = jnp.float32
    b, qb = q.shape[:2]
    sl = slopes.reshape(NSA_KV_HEADS, NSA_HPG)
    scale = HEAD_DIM ** -0.5
    s = jnp.einsum('bqghd,bngd->bqghn', q, kc).astype(f32) * scale
    dist = (qpos[:, None] - cend[None, :]).astype(f32)
    s = s - sl[:, :, None] * dist[:, None, None, :]
    p = masked_softmax(s, (cend[None, :] <= qpos[:, None])[:, None, None, :])
    o_cmp = jnp.einsum('bqghn,bngd->bqghd', p.astype(vc.dtype), vc)
    imp = p.sum(axis=3)
    n_cmp = imp.shape[-1]
    n_slc = ksb.shape[2]
    run = imp
    for sft in range(1, CMP_BLOCK // CMP_STRIDE):
        run = run + jnp.pad(imp, ((0, 0), (0, 0), (0, 0), (sft, 0)))[..., :n_cmp]
    ratio = SLC_BLOCK // CMP_STRIDE
    run = jnp.pad(run, ((0, 0), (0, 0), (0, 0), (0, ratio * n_slc - n_cmp)))
    score = run.reshape(b, qb, NSA_KV_HEADS, n_slc, ratio).sum(-1)
    blk = jnp.arange(n_slc)
    cur = qpos // SLC_BLOCK
    eligible = blk[None, :] * SLC_BLOCK <= qpos[:, None]
    forced = (blk[None, :] == 0) | (blk[None, :] == cur[:, None]) | (blk[None, :] == cur[:, None] - 1)
    score = jnp.where(forced[:, None, :], NEG_BIG, score)
    score = jnp.where(eligible[:, None, :], score, -NEG_BIG)
    top_val, top_idx = lax.top_k(score, min(N_SELECT, n_slc))
    sel_ok = top_val > -0.5 * NEG_BIG
    b_ix = jnp.arange(b)[:, None, None, None]
    g_ix = jnp.arange(NSA_KV_HEADS)[None, None, :, None]
    ksel = ksb[b_ix, g_ix, top_idx]
    vsel = vsb[b_ix, g_ix, top_idx]
    kpos = top_idx[..., None] * SLC_BLOCK + jnp.arange(SLC_BLOCK)
    s2 = jnp.einsum('bqghd,bqgkjd->bqghkj', q, ksel).astype(f32) * scale
    dist2 = (qpos[None, :, None, None, None] - kpos).astype(f32)
    s2 = s2 - sl[None, None, :, :, None, None] * dist2[:, :, :, None]
    ok2 = (sel_ok[..., None] & (dist2 >= 0))[:, :, :, None]
    shp = s2.shape
    nk = shp[4] * shp[5]
    p2 = masked_softmax(s2.reshape(shp[:4] + (nk,)), ok2.reshape(ok2.shape[:4] + (nk,))).reshape(shp)
    o_slc = jnp.einsum('bqghkj,bqgkjd->bqghd', p2.astype(vsel.dtype), vsel)
    return o_cmp, o_slc


def window_attn(q, qpos, k, v, kpos, slopes):
    sl = slopes.reshape(NSA_KV_HEADS, NSA_HPG)
    s = jnp.einsum('bqghd,bkgd->bqghk', q, k).astype(jnp.float32) * HEAD_DIM ** -0.5
    dist = (qpos[:, None] - kpos[None, :]).astype(jnp.float32)
    s = s - sl[:, :, None] * dist[:, None, None, :]
    ok = ((dist >= 0) & (dist < WINDOW) & (kpos[None, :] >= 0))[:, None, None, :]
    p = masked_softmax(s, ok)
    return jnp.einsum('bqghk,bkgd->bqghd', p.astype(v.dtype), v)


def gla_chunked(q, k, v, log_a, s0):
    f32 = jnp.float32
    b, l = q.shape[:2]
    c = min(GLA_CHUNK, l)
    pad = (-l) % c
    n = (l + pad) // c
    def prep(t):
        t = jnp.pad(t.astype(f32), ((0, 0), (0, pad), (0, 0), (0, 0)))
        return t.reshape(b, n, c, GLA_HEADS, t.shape[-1])
    q = prep(q) * GLA_DK ** -0.5
    k, v, log_a = prep(k), prep(v), prep(log_a)
    cum = jnp.cumsum(log_a, axis=2)
    causal = jnp.tril(jnp.ones((c, c), dtype=bool))[:, :, None, None]
    diff = cum[:, :, :, None] - cum[:, :, None, :]
    decay = jnp.where(causal, jnp.exp(jnp.where(causal, diff, 0.0)), 0.0)
    att = jnp.einsum('bnthd,bnshd,bntshd->bnhts', q, k, decay)
    o_intra = jnp.einsum('bnhts,bnshe->bnthe', att, v)
    q_dec = q * jnp.exp(cum)
    k_dec = k * jnp.exp(cum[:, :, -1:] - cum)
    a_tot = jnp.exp(cum[:, :, -1])
    def step(state, inp):
        qd, kd, vv, at = inp
        o = jnp.einsum('bchd,bhde->bche', qd, state)
        state = at[..., None] * state + jnp.einsum('bchd,bche->bhde', kd, vv)
        return state, o
    xs = (jnp.moveaxis(q_dec, 1, 0), jnp.moveaxis(k_dec, 1, 0), jnp.moveaxis(v, 1, 0), jnp.moveaxis(a_tot, 1, 0))
    s_fin, o_inter = lax.scan(step, s0.astype(f32), xs)
    o = o_intra + jnp.moveaxis(o_inter, 0, 1)
    return o.reshape(b, n * c, GLA_HEADS, GLA_DV)[:, :l], s_fin


def s5_combine(e1, e2):
    a1r, a1i, b1r, b1i = e1
    a2r, a2i, b2r, b2i = e2
    return (a2r * a1r - a2i * a1i, a2r * a1i + a2i * a1r,
            a2r * b1r - a2i * b1i + b2r, a2r * b1i + a2i * b1r + b2i)


def s5_scan(u, h0, a_re, a_im, b_re, b_im, c_re, c_im, d, log_dt):
    f32 = jnp.float32
    b, l, _ = u.shape
    u = u.astype(f32)
    a_re, a_im = a_re.astype(f32), a_im.astype(f32)
    b_re, b_im, c_re, c_im = b_re.astype(f32), b_im.astype(f32), c_re.astype(f32), c_im.astype(f32)
    dt = jnp.exp(log_dt.astype(f32))[:, None]
    mag = jnp.exp(a_re * dt)
    abr, abi = mag * jnp.cos(a_im * dt), mag * jnp.sin(a_im * dt)
    den = a_re * a_re + a_im * a_im
    nr = abr - 1.0
    cr = (nr * a_re + abi * a_im) / den
    ci = (abi * a_re - nr * a_im) / den
    bbr = cr[..., None] * b_re - ci[..., None] * b_im
    bbi = cr[..., None] * b_im + ci[..., None] * b_re
    ug = u.reshape(b, l, S5_GROUPS, S5_GROUP_CH)
    xr = jnp.einsum('gpc,blgc->blgp', bbr, ug)
    xi = jnp.einsum('gpc,blgc->blgp', bbi, ug)
    h0r, h0i = h0[..., 0].astype(f32), h0[..., 1].astype(f32)
    xr = xr.at[:, 0].add(abr * h0r - abi * h0i)
    xi = xi.at[:, 0].add(abr * h0i + abi * h0r)
    ar = jnp.broadcast_to(abr, xr.shape)
    ai = jnp.broadcast_to(abi, xr.shape)
    _, _, hr, hi = lax.associative_scan(s5_combine, (ar, ai, xr, xi), axis=1)
    y = jnp.einsum('gcp,blgp->blgc', c_re, hr) - jnp.einsum('gcp,blgp->blgc', c_im, hi)
    y = y.reshape(b, l, S5_WIDTH) + d.astype(f32) * u
    return y, jnp.stack([hr[:, -1], hi[:, -1]], axis=-1)


def hybrid_layer(h, ple, lw, past):
    f32 = jnp.float32
    b, l, _ = h.shape
    G, J, E = NSA_KV_HEADS, NSA_HPG, HEAD_DIM
    slopes = alibi_slopes()
    xn = rmsnorm(h, lw['norm'])
    nq, nkv, ngate, nz, gq, gk, gv, glr, gz, su, sz = split_columns(xn @ lw['w_in'])

    q = nq.reshape(b, l, G, J, E)
    kv = nkv.reshape(b, l, 3, 2, G, E)
    new_cmp, new_slc, new_win = kv[:, :, 0], kv[:, :, 1], kv[:, :, 2]
    if past is None:
        pos0 = 0
        full_cmp, full_slc = new_cmp, new_slc
    else:
        pt = past['page_table']
        pos0 = pt.shape[1] * past['cmp'].shape[1]
        def gather_past(pool):
            rows = pool[pt]
            return rows.reshape((b, pos0) + rows.shape[3:])
        full_cmp = jnp.concatenate([gather_past(past['cmp']), new_cmp], axis=1)
        full_slc = jnp.concatenate([gather_past(past['slc']), new_slc], axis=1)
    t_all = pos0 + l
    kc = compress_kv(full_cmp[:, :, 0], lw['cmp_pe'][0], lw['cmp_w1'][0], lw['cmp_w2'][0])
    vc = compress_kv(full_cmp[:, :, 1], lw['cmp_pe'][1], lw['cmp_w1'][1], lw['cmp_w2'][1])
    cend = jnp.arange(kc.shape[1]) * CMP_STRIDE + (CMP_BLOCK - 1)
    n_slc = -(-t_all // SLC_BLOCK)
    slc = jnp.pad(full_slc, ((0, 0), (0, n_slc * SLC_BLOCK - t_all), (0, 0), (0, 0), (0, 0)))
    slc = jnp.transpose(slc.reshape(b, n_slc, SLC_BLOCK, 2, G, E), (3, 0, 4, 1, 2, 5))
    ksb, vsb = slc[0], slc[1]
    qb = min(SPARSE_Q_BLOCK, l)
    qpad = (-l) % qb
    nqb = (l + qpad) // qb
    q_blocks = jnp.moveaxis(jnp.pad(q, ((0, 0), (0, qpad), (0, 0), (0, 0), (0, 0))).reshape(b, nqb, qb, G, J, E), 1, 0)
    pos_blocks = (pos0 + jnp.arange(l + qpad)).reshape(nqb, qb)
    o_cmp, o_slc = lax.map(lambda a: nsa_sparse_block(a[0], a[1], kc, vc, cend, ksb, vsb, slopes), (q_blocks, pos_blocks))
    def unblock(o):
        return jnp.moveaxis(o, 0, 1).reshape(b, l + qpad, G, J, E)[:, :l]
    if past is None:
        span = WINDOW // WIN_Q_BLOCK
        nwb = l // WIN_Q_BLOCK
        wpad = jnp.pad(new_win, ((0, 0), (WINDOW, 0), (0, 0), (0, 0), (0, 0)))
        wb = wpad.reshape(b, nwb + span, WIN_Q_BLOCK, 2, G, E)
        band = jnp.concatenate([wb[:, s:s + nwb] for s in range(span + 1)], axis=2)
        band_pos = jnp.arange(nwb)[:, None] * WIN_Q_BLOCK - WINDOW + jnp.arange((span + 1) * WIN_Q_BLOCK)[None, :]
        qw = jnp.moveaxis(q.reshape(b, nwb, WIN_Q_BLOCK, G, J, E), 1, 0)
        qw_pos = jnp.arange(l).reshape(nwb, WIN_Q_BLOCK)
        o_win = lax.map(lambda a: window_attn(a[0], a[1], a[2][:, :, 0], a[2][:, :, 1], a[3], slopes),
                        (qw, qw_pos, jnp.moveaxis(band, 1, 0), band_pos))
        o_win = jnp.moveaxis(o_win, 0, 1).reshape(b, l, G, J, E)
        win_state = new_win[:, l - min(WINDOW, l):]
    else:
        buf = past['win']
        wall = jnp.concatenate([buf, new_win], axis=1)
        wpos = pos0 - buf.shape[1] + jnp.arange(wall.shape[1])
        o_win = window_attn(q, pos0 + jnp.arange(l), wall[:, :, 0], wall[:, :, 1], wpos, slopes)
        win_state = wall[:, l:]
    gates = jax.nn.sigmoid(ngate.astype(f32)).reshape(b, l, G, J, 3)
    o_nsa = (gates[..., 0:1] * unblock(o_cmp).astype(f32) + gates[..., 1:2] * unblock(o_slc).astype(f32)
             + gates[..., 2:3] * o_win.astype(f32))
    o_nsa = o_nsa.reshape(b, l, NSA_WIDTH).astype(h.dtype) * jax.nn.silu(nz)

    lg = glr @ lw['gla_wg'] + lw['gla_bg']
    log_a = (jax.nn.log_sigmoid(lg.astype(f32)) / GLA_GATE_TAU).reshape(b, l, GLA_HEADS, GLA_DK)
    s0 = jnp.zeros((b, GLA_HEADS, GLA_DK, GLA_DV), f32) if past is None else past['gla']
    o_g, gla_state = gla_chunked(gq.reshape(b, l, GLA_HEADS, GLA_DK), gk.reshape(b, l, GLA_HEADS, GLA_DK),
                                 gv.reshape(b, l, GLA_HEADS, GLA_DV), log_a, s0)
    o_gla = rmsnorm(o_g, lw['gla_norm']).reshape(b, l, GLA_WIDTH).astype(h.dtype) * jax.nn.silu(gz)

    h0 = jnp.zeros((b, S5_GROUPS, S5_STATE, 2), f32) if past is None else past['s5']
    y_s5, s5_state = s5_scan(su, h0, lw['s5_a_re'], lw['s5_a_im'], lw['s5_b_re'], lw['s5_b_im'],
                             lw['s5_c_re'], lw['s5_c_im'], lw['s5_d'], lw['s5_log_dt'])
    glu = jax.nn.gelu(y_s5).astype(h.dtype) @ lw['s5_w_glu']
    o_s5 = glu[..., :S5_WIDTH] * jax.nn.sigmoid(glu[..., S5_WIDTH:]) * jax.nn.silu(sz)

    h = h + jnp.concatenate([o_nsa, o_gla, o_s5], axis=-1) @ lw['w_out']
    gate = jax.nn.sigmoid(rmsnorm(h, lw['ple_norm']) @ lw['ple_w_gate'])
    h = h + gate * (ple @ lw['ple_w_proj'])
    return h, (new_cmp, new_slc, win_state, gla_state, s5_state)


def setup_inputs(seed: int = 0) -> dict:
    key = jax.random.key(seed)
    ks = jax.random.split(key, 48)
    f32 = jnp.float32
    def nrm(i, shape, scale):
        return scale * jax.random.normal(ks[i], shape, f32)
    n_pages = PAST_LEN // PAGE_SIZE
    n_used = DEC_BATCH * n_pages
    n_pool = n_used + max(1, n_used // 4)
    win_buf = min(WINDOW, PAST_LEN)
    kvrow = (2, NSA_KV_HEADS, HEAD_DIM)
    page_table = jax.random.permutation(ks[7], n_pool)[:n_used].reshape(DEC_BATCH, n_pages).astype(jnp.int32)
    a_im = jnp.broadcast_to(jnp.pi * jnp.arange(S5_STATE, dtype=f32), (DEPTH, S5_GROUPS, S5_STATE))
    return {
        'x_prompt': nrm(0, (BATCH, SEQ, D_MODEL), 1.0),
        'x_sample': nrm(1, (DEC_BATCH, DEC_SEQ, D_MODEL), 1.0),
        'cache_cmp': nrm(2, (DEPTH, n_pool, PAGE_SIZE) + kvrow, 1.0),
        'cache_slc': nrm(3, (DEPTH, n_pool, PAGE_SIZE) + kvrow, 1.0),
        'cache_win': nrm(4, (DEPTH, DEC_BATCH, win_buf) + kvrow, 1.0),
        'state_gla': nrm(5, (DEPTH, DEC_BATCH, GLA_HEADS, GLA_DK, GLA_DV), 1.0),
        'state_s5': nrm(6, (DEPTH, DEC_BATCH, S5_GROUPS, S5_STATE, 2), 1.0),
        'page_table': page_table,
        'p_prompt': nrm(8, (DEPTH, BATCH, SEQ, PLE_DIM), 1.0),
        'p_sample': nrm(9, (DEPTH, DEC_BATCH, DEC_SEQ, PLE_DIM), 1.0),
        'norm_mix': 1.0 + nrm(10, (DEPTH, D_MODEL), 0.01),
        'w_in': nrm(11, (DEPTH, D_MODEL, IN_WIDTH), D_MODEL ** -0.5),
        'w_out': nrm(12, (DEPTH, MIX_WIDTH, D_MODEL), MIX_WIDTH ** -0.5),
        'cmp_pe': nrm(13, (DEPTH, 2, CMP_BLOCK, HEAD_DIM), 0.1),
        'cmp_w1': nrm(14, (DEPTH, 2, CMP_BLOCK * HEAD_DIM, HEAD_DIM), (CMP_BLOCK * HEAD_DIM) ** -0.5),
        'cmp_w2': nrm(15, (DEPTH, 2, HEAD_DIM, HEAD_DIM), HEAD_DIM ** -0.5),
        'gla_wg': nrm(16, (DEPTH, GLA_GATE_RANK, GLA_HEADS * GLA_DK), GLA_GATE_RANK ** -0.5),
        'gla_bg': nrm(17, (DEPTH, GLA_HEADS * GLA_DK), 0.1),
        'gla_norm': 1.0 + nrm(18, (DEPTH, GLA_DV), 0.01),
        's5_a_re': -0.5 + nrm(19, (DEPTH, S5_GROUPS, S5_STATE), 0.01),
        's5_a_im': a_im,
        's5_b_re': nrm(20, (DEPTH, S5_GROUPS, S5_STATE, S5_GROUP_CH), (2 * S5_GROUP_CH) ** -0.5),
        's5_b_im': nrm(21, (DEPTH, S5_GROUPS, S5_STATE, S5_GROUP_CH), (2 * S5_GROUP_CH) ** -0.5),
        's5_c_re': nrm(22, (DEPTH, S5_GROUPS, S5_GROUP_CH, S5_STATE), (2 * S5_STATE) ** -0.5),
        's5_c_im': nrm(23, (DEPTH, S5_GROUPS, S5_GROUP_CH, S5_STATE), (2 * S5_STATE) ** -0.5),
        's5_d': nrm(24, (DEPTH, S5_WIDTH), 1.0),
        's5_log_dt': jax.random.uniform(ks[25], (DEPTH, S5_GROUPS), f32, math.log(0.001), math.log(0.1)),
        's5_w_glu': nrm(26, (DEPTH, S5_WIDTH, 2 * S5_WIDTH), S5_WIDTH ** -0.5),
        'ple_norm': 1.0 + nrm(27, (DEPTH, D_MODEL), 0.01),
        'ple_w_gate': nrm(28, (DEPTH, D_MODEL, D_MODEL), D_MODEL ** -0.5),
        'ple_w_proj': nrm(29, (DEPTH, PLE_DIM, D_MODEL), PLE_DIM ** -0.5),
        'final_norm': 1.0 + nrm(30, (D_MODEL,), 0.01),
    }


def reference(x_prompt, x_sample, cache_cmp, cache_slc, cache_win, state_gla, state_s5, page_table,
              p_prompt, p_sample, norm_mix, w_in, w_out, cmp_pe, cmp_w1, cmp_w2, gla_wg, gla_bg, gla_norm,
              s5_a_re, s5_a_im, s5_b_re, s5_b_im, s5_c_re, s5_c_im, s5_d, s5_log_dt, s5_w_glu,
              ple_norm, ple_w_gate, ple_w_proj, final_norm):
    hp, hs = x_prompt, x_sample
    st_p, st_s = [], []
    for i in range(DEPTH):
        lw = {'norm': norm_mix[i], 'w_in': w_in[i], 'w_out': w_out[i],
              'cmp_pe': cmp_pe[i], 'cmp_w1': cmp_w1[i], 'cmp_w2': cmp_w2[i],
              'gla_wg': gla_wg[i], 'gla_bg': gla_bg[i], 'gla_norm': gla_norm[i],
              's5_a_re': s5_a_re[i], 's5_a_im': s5_a_im[i], 's5_b_re': s5_b_re[i], 's5_b_im': s5_b_im[i],
              's5_c_re': s5_c_re[i], 's5_c_im': s5_c_im[i], 's5_d': s5_d[i], 's5_log_dt': s5_log_dt[i],
              's5_w_glu': s5_w_glu[i], 'ple_norm': ple_norm[i], 'ple_w_gate': ple_w_gate[i],
              'ple_w_proj': ple_w_proj[i]}
        hp, sp = hybrid_layer(hp, p_prompt[i], lw, None)
        past = {'page_table': page_table, 'cmp': cache_cmp[i], 'slc': cache_slc[i], 'win': cache_win[i],
                'gla': state_gla[i], 's5': state_s5[i]}
        hs, ss = hybrid_layer(hs, p_sample[i], lw, past)
        st_p.append(sp)
        st_s.append(ss)
    y_prompt = rmsnorm(hp, final_norm)
    y_sample = rmsnorm(hs, final_norm)
    cmp_p = jnp.stack([s[0] for s in st_p])
    cmp_s = jnp.stack([s[0] for s in st_s])
    slc_p = jnp.stack([s[1] for s in st_p])
    slc_s = jnp.stack([s[1] for s in st_s])
    win_p = jnp.stack([s[2] for s in st_p])
    win_s = jnp.stack([s[2] for s in st_s])
    gla_p = jnp.stack([s[3] for s in st_p])
    gla_s = jnp.stack([s[3] for s in st_s])
    s5_p = jnp.stack([s[4] for s in st_p])
    s5_s = jnp.stack([s[4] for s in st_s])
    return (y_prompt, y_sample, cmp_p, cmp_s, slc_p, slc_s, win_p, win_s, gla_p, gla_s, s5_p, s5_s)
```

```python
import functools
import math

import jax
import jax.numpy as jnp
from jax import lax
from jax.experimental import pallas as pl
from jax.experimental.pallas import tpu as pltpu

D_MODEL = 1024
PLE_DIM = 256
HEAD_DIM = 64
NSA_HEADS = 8
NSA_KV_HEADS = 2
NSA_HPG = NSA_HEADS // NSA_KV_HEADS
NSA_WIDTH = NSA_HEADS * HEAD_DIM
CMP_BLOCK = 32
CMP_STRIDE = 16
SLC_BLOCK = 64
N_SELECT = 16
WINDOW = 512
SPARSE_Q_BLOCK = 64
WIN_Q_BLOCK = 128
GLA_HEADS = 4
GLA_DK = 32
GLA_DV = 64
GLA_WIDTH = GLA_HEADS * GLA_DV
GLA_GATE_RANK = 16
GLA_GATE_TAU = 16.0
GLA_CHUNK = 32
S5_GROUPS = 16
S5_GROUP_CH = 16
S5_STATE = 64
S5_WIDTH = S5_GROUPS * S5_GROUP_CH
MIX_WIDTH = NSA_WIDTH + GLA_WIDTH + S5_WIDTH
KV_WIDTH = 6 * NSA_KV_HEADS * HEAD_DIM
RMS_EPS = 1e-6
NEG_BIG = 1e9

LANES = 128
VMEM_LIMIT = 48 * 1024 * 1024

_SRC = {}
_off = 0
for _name, _w in (("nq", NSA_WIDTH), ("nkv", KV_WIDTH), ("ngate", 3 * NSA_HEADS), ("nz", NSA_WIDTH),
                  ("gq", GLA_HEADS * GLA_DK), ("gk", GLA_HEADS * GLA_DK), ("gv", GLA_WIDTH),
                  ("glr", GLA_GATE_RANK), ("gz", GLA_WIDTH), ("su", S5_WIDTH), ("sz", S5_WIDTH)):
    _SRC[_name] = (_off, _w)
    _off += _w
IN_WIDTH = _off
_ORDER = ("nq", "nkv", "nz", "gq", "gk", "gv", "gz", "su", "sz", "ngate", "glr")
_DST = {}
_off = 0
for _name in _ORDER:
    _DST[_name] = (_off, _SRC[_name][1])
    _off += _SRC[_name][1]
IN_PAD = -(-_off // LANES) * LANES


def _relayout_w_in(w_in):
    parts = [w_in[..., _SRC[n][0]:_SRC[n][0] + _SRC[n][1]] for n in _ORDER]
    w = jnp.concatenate(parts, axis=-1)
    w = jnp.pad(w, ((0, 0), (0, 0), (0, IN_PAD - w.shape[-1])))
    return w.astype(jnp.bfloat16)


def _seg(u, name):
    o, w = _DST[name]
    return u[..., o:o + w]


def _in_proj_kernel(x_ref, g_ref, w_ref, o_ref):
    x = x_ref[...]
    ms = jnp.mean(x * x, axis=-1, keepdims=True)
    xn = x * lax.rsqrt(ms + RMS_EPS) * g_ref[...]
    o_ref[...] = jnp.dot(xn.astype(jnp.bfloat16), w_ref[...], preferred_element_type=jnp.float32)


def _in_proj(x2d, g, w_bf16):
    n, d = x2d.shape
    tm = min(512, n)
    return pl.pallas_call(
        _in_proj_kernel,
        out_shape=jax.ShapeDtypeStruct((n, IN_PAD), jnp.float32),
        grid=(n // tm,),
        in_specs=[pl.BlockSpec((tm, d), lambda i: (i, 0)),
                  pl.BlockSpec((1, d), lambda i: (0, 0)),
                  pl.BlockSpec((d, IN_PAD), lambda i: (0, 0))],
        out_specs=pl.BlockSpec((tm, IN_PAD), lambda i: (i, 0)),
        compiler_params=pltpu.CompilerParams(dimension_semantics=("parallel",), vmem_limit_bytes=VMEM_LIMIT),
        name="in_proj",
    )(x2d, g.reshape(1, d), w_bf16)


def _rmsnorm(x, g):
    xf = x.astype(jnp.float32)
    y = xf * lax.rsqrt(jnp.mean(xf * xf, axis=-1, keepdims=True) + RMS_EPS)
    return (y * g.astype(jnp.float32)).astype(x.dtype)


def _masked_softmax(s, mask):
    s = jnp.where(mask, s, -1e30)
    e = jnp.where(mask, jnp.exp(s - jnp.max(s, axis=-1, keepdims=True)), 0.0)
    return e / jnp.maximum(jnp.sum(e, axis=-1, keepdims=True), 1e-30)


def _alibi_slopes():
    return jnp.exp2(-8.0 * jnp.arange(1, NSA_HEADS + 1, dtype=jnp.float32) / NSA_HEADS)


def _compress_kv(x, pe, w1, w2):
    b, t, g, d = x.shape
    ratio = CMP_BLOCK // CMP_STRIDE
    n_ch = t // CMP_STRIDE
    n_cmp = n_ch - ratio + 1
    ch = x[:, :n_ch * CMP_STRIDE].reshape(b, n_ch, CMP_STRIDE, g, d)
    blocks = jnp.concatenate([ch[:, s:s + n_cmp] for s in range(ratio)], axis=2)
    blocks = blocks + pe[None, None, :, None, :]
    flat = jnp.moveaxis(blocks, 2, 3).reshape(b, n_cmp, g, CMP_BLOCK * d)
    return jax.nn.gelu(flat @ w1) @ w2


def _nsa_sparse_block(q, qpos, kc, vc, cend, ksb, vsb, slopes):
    f32 = jnp.float32
    b, qb = q.shape[:2]
    sl = slopes.reshape(NSA_KV_HEADS, NSA_HPG)
    scale = HEAD_DIM ** -0.5
    s = jnp.einsum('bqghd,bngd->bqghn', q, kc).astype(f32) * scale
    dist = (qpos[:, None] - cend[None, :]).astype(f32)
    s = s - sl[:, :, None] * dist[:, None, None, :]
    p = _masked_softmax(s, (cend[None, :] <= qpos[:, None])[:, None, None, :])
    o_cmp = jnp.einsum('bqghn,bngd->bqghd', p.astype(vc.dtype), vc)
    imp = p.sum(axis=3)
    n_cmp = imp.shape[-1]
    n_slc = ksb.shape[2]
    run = imp
    for sft in range(1, CMP_BLOCK // CMP_STRIDE):
        run = run + jnp.pad(imp, ((0, 0), (0, 0), (0, 0), (sft, 0)))[..., :n_cmp]
    ratio = SLC_BLOCK // CMP_STRIDE
    run = jnp.pad(run, ((0, 0), (0, 0), (0, 0), (0, ratio * n_slc - n_cmp)))
    score = run.reshape(b, qb, NSA_KV_HEADS, n_slc, ratio).sum(-1)
    blk = jnp.arange(n_slc)
    cur = qpos // SLC_BLOCK
    eligible = blk[None, :] * SLC_BLOCK <= qpos[:, None]
    forced = (blk[None, :] == 0) | (blk[None, :] == cur[:, None]) | (blk[None, :] == cur[:, None] - 1)
    score = jnp.where(forced[:, None, :], NEG_BIG, score)
    score = jnp.where(eligible[:, None, :], score, -NEG_BIG)
    top_val, top_idx = lax.top_k(score, min(N_SELECT, n_slc))
    sel_ok = top_val > -0.5 * NEG_BIG
    b_ix = jnp.arange(b)[:, None, None, None]
    g_ix = jnp.arange(NSA_KV_HEADS)[None, None, :, None]
    ksel = ksb[b_ix, g_ix, top_idx]
    vsel = vsb[b_ix, g_ix, top_idx]
    kpos = top_idx[..., None] * SLC_BLOCK + jnp.arange(SLC_BLOCK)
    s2 = jnp.einsum('bqghd,bqgkjd->bqghkj', q, ksel).astype(f32) * scale
    dist2 = (qpos[None, :, None, None, None] - kpos).astype(f32)
    s2 = s2 - sl[None, None, :, :, None, None] * dist2[:, :, :, None]
    ok2 = (sel_ok[..., None] & (dist2 >= 0))[:, :, :, None]
    shp = s2.shape
    nk = shp[4] * shp[5]
    p2 = _masked_softmax(s2.reshape(shp[:4] + (nk,)), ok2.reshape(ok2.shape[:4] + (nk,))).reshape(shp)
    o_slc = jnp.einsum('bqghkj,bqgkjd->bqghd', p2.astype(vsel.dtype), vsel)
    return o_cmp, o_slc


def _window_attn(q, qpos, k, v, kpos, slopes):
    sl = slopes.reshape(NSA_KV_HEADS, NSA_HPG)
    s = jnp.einsum('bqghd,bkgd->bqghk', q, k).astype(jnp.float32) * HEAD_DIM ** -0.5
    dist = (qpos[:, None] - kpos[None, :]).astype(jnp.float32)
    s = s - sl[:, :, None] * dist[:, None, None, :]
    ok = ((dist >= 0) & (dist < WINDOW) & (kpos[None, :] >= 0))[:, None, None, :]
    p = _masked_softmax(s, ok)
    return jnp.einsum('bqghk,bkgd->bqghd', p.astype(v.dtype), v)


def _gla_chunked(q, k, v, log_a, s0):
    f32 = jnp.float32
    b, l = q.shape[:2]
    c = min(GLA_CHUNK, l)
    pad = (-l) % c
    n = (l + pad) // c

    def prep(t):
        t = jnp.pad(t.astype(f32), ((0, 0), (0, pad), (0, 0), (0, 0)))
        return t.reshape(b, n, c, GLA_HEADS, t.shape[-1])
    q = prep(q) * GLA_DK ** -0.5
    k, v, log_a = prep(k), prep(v), prep(log_a)
    cum = jnp.cumsum(log_a, axis=2)
    causal = jnp.tril(jnp.ones((c, c), dtype=bool))[:, :, None, None]
    diff = cum[:, :, :, None] - cum[:, :, None, :]
    decay = jnp.where(causal, jnp.exp(jnp.where(causal, diff, 0.0)), 0.0)
    att = jnp.einsum('bnthd,bnshd,bntshd->bnhts', q, k, decay)
    o_intra = jnp.einsum('bnhts,bnshe->bnthe', att, v)
    q_dec = q * jnp.exp(cum)
    k_dec = k * jnp.exp(cum[:, :, -1:] - cum)
    a_tot = jnp.exp(cum[:, :, -1])

    def step(state, inp):
        qd, kd, vv, at = inp
        o = jnp.einsum('bchd,bhde->bche', qd, state)
        state = at[..., None] * state + jnp.einsum('bchd,bche->bhde', kd, vv)
        return state, o
    xs = (jnp.moveaxis(q_dec, 1, 0), jnp.moveaxis(k_dec, 1, 0), jnp.moveaxis(v, 1, 0), jnp.moveaxis(a_tot, 1, 0))
    s_fin, o_inter = lax.scan(step, s0.astype(f32), xs)
    o = o_intra + jnp.moveaxis(o_inter, 0, 1)
    return o.reshape(b, n * c, GLA_HEADS, GLA_DV)[:, :l], s_fin


def _s5_combine(e1, e2):
    a1r, a1i, b1r, b1i = e1
    a2r, a2i, b2r, b2i = e2
    return (a2r * a1r - a2i * a1i, a2r * a1i + a2i * a1r,
            a2r * b1r - a2i * b1i + b2r, a2r * b1i + a2i * b1r + b2i)


def _s5_scan(u, h0, a_re, a_im, b_re, b_im, c_re, c_im, d, log_dt):
    f32 = jnp.float32
    b, l, _ = u.shape
    dt = jnp.exp(log_dt.astype(f32))[:, None]
    mag = jnp.exp(a_re * dt)
    abr, abi = mag * jnp.cos(a_im * dt), mag * jnp.sin(a_im * dt)
    den = a_re * a_re + a_im * a_im
    nr = abr - 1.0
    cr = (nr * a_re + abi * a_im) / den
    ci = (abi * a_re - nr * a_im) / den
    bbr = cr[..., None] * b_re - ci[..., None] * b_im
    bbi = cr[..., None] * b_im + ci[..., None] * b_re
    ug = u.reshape(b, l, S5_GROUPS, S5_GROUP_CH)
    xr = jnp.einsum('gpc,blgc->blgp', bbr, ug)
    xi = jnp.einsum('gpc,blgc->blgp', bbi, ug)
    h0r, h0i = h0[..., 0].astype(f32), h0[..., 1].astype(f32)
    xr = xr.at[:, 0].add(abr * h0r - abi * h0i)
    xi = xi.at[:, 0].add(abr * h0i + abi * h0r)
    ar = jnp.broadcast_to(abr, xr.shape)
    ai = jnp.broadcast_to(abi, xr.shape)
    _, _, hr, hi = lax.associative_scan(_s5_combine, (ar, ai, xr, xi), axis=1)
    y = jnp.einsum('gcp,blgp->blgc', c_re, hr) - jnp.einsum('gcp,blgp->blgc', c_im, hi)
    y = y.reshape(b, l, S5_WIDTH) + d.astype(f32) * u
    return y, jnp.stack([hr[:, -1], hi[:, -1]], axis=-1)


def _hybrid_layer(h, ple, lw, past):
    f32 = jnp.float32
    b, l, _ = h.shape
    G, J, E = NSA_KV_HEADS, NSA_HPG, HEAD_DIM
    slopes = _alibi_slopes()
    u = _in_proj(h.reshape(b * l, D_MODEL), lw['norm'], lw['w_in_r']).reshape(b, l, IN_PAD)
    nq, nkv, ngate, nz = _seg(u, "nq"), _seg(u, "nkv"), _seg(u, "ngate"), _seg(u, "nz")
    gq, gk, gv, glr, gz = _seg(u, "gq"), _seg(u, "gk"), _seg(u, "gv"), _seg(u, "glr"), _seg(u, "gz")
    su, sz = _seg(u, "su"), _seg(u, "sz")

    q = nq.reshape(b, l, G, J, E)
    kv = nkv.reshape(b, l, 3, 2, G, E)
    new_cmp, new_slc, new_win = kv[:, :, 0], kv[:, :, 1], kv[:, :, 2]
    if past is None:
        pos0 = 0
        full_cmp, full_slc = new_cmp, new_slc
    else:
        pt = past['page_table']
        pos0 = pt.shape[1] * past['cmp'].shape[1]

        def gather_past(pool):
            rows = pool[pt]
            return rows.reshape((b, pos0) + rows.shape[3:])
        full_cmp = jnp.concatenate([gather_past(past['cmp']), new_cmp], axis=1)
        full_slc = jnp.concatenate([gather_past(past['slc']), new_slc], axis=1)
    t_all = pos0 + l
    kc = _compress_kv(full_cmp[:, :, 0], lw['cmp_pe'][0], lw['cmp_w1'][0], lw['cmp_w2'][0])
    vc = _compress_kv(full_cmp[:, :, 1], lw['cmp_pe'][1], lw['cmp_w1'][1], lw['cmp_w2'][1])
    cend = jnp.arange(kc.shape[1]) * CMP_STRIDE + (CMP_BLOCK - 1)
    n_slc = -(-t_all // SLC_BLOCK)
    slc = jnp.pad(full_slc, ((0, 0), (0, n_slc * SLC_BLOCK - t_all), (0, 0), (0, 0), (0, 0)))
    slc = jnp.transpose(slc.reshape(b, n_slc, SLC_BLOCK, 2, G, E), (3, 0, 4, 1, 2, 5))
    ksb, vsb = slc[0], slc[1]
    qb = min(SPARSE_Q_BLOCK, l)
    qpad = (-l) % qb
    nqb = (l + qpad) // qb
    q_blocks = jnp.moveaxis(jnp.pad(q, ((0, 0), (0, qpad), (0, 0), (0, 0), (0, 0))).reshape(b, nqb, qb, G, J, E), 1, 0)
    pos_blocks = (pos0 + jnp.arange(l + qpad)).reshape(nqb, qb)
    o_cmp, o_slc = lax.map(lambda a: _nsa_sparse_block(a[0], a[1], kc, vc, cend, ksb, vsb, slopes), (q_blocks, pos_blocks))

    def unblock(o):
        return jnp.moveaxis(o, 0, 1).reshape(b, l + qpad, G, J, E)[:, :l]
    if past is None:
        span = WINDOW // WIN_Q_BLOCK
        nwb = l // WIN_Q_BLOCK
        wpad = jnp.pad(new_win, ((0, 0), (WINDOW, 0), (0, 0), (0, 0), (0, 0)))
        wb = wpad.reshape(b, nwb + span, WIN_Q_BLOCK, 2, G, E)
        band = jnp.concatenate([wb[:, s:s + nwb] for s in range(span + 1)], axis=2)
        band_pos = jnp.arange(nwb)[:, None] * WIN_Q_BLOCK - WINDOW + jnp.arange((span + 1) * WIN_Q_BLOCK)[None, :]
        qw = jnp.moveaxis(q.reshape(b, nwb, WIN_Q_BLOCK, G, J, E), 1, 0)
        qw_pos = jnp.arange(l).reshape(nwb, WIN_Q_BLOCK)
        o_win = lax.map(lambda a: _window_attn(a[0], a[1], a[2][:, :, 0], a[2][:, :, 1], a[3], slopes),
                        (qw, qw_pos, jnp.moveaxis(band, 1, 0), band_pos))
        o_win = jnp.moveaxis(o_win, 0, 1).reshape(b, l, G, J, E)
        win_state = new_win[:, l - min(WINDOW, l):]
    else:
        buf = past['win']
        wall = jnp.concatenate([buf, new_win], axis=1)
        wpos = pos0 - buf.shape[1] + jnp.arange(wall.shape[1])
        o_win = _window_attn(q, pos0 + jnp.arange(l), wall[:, :, 0], wall[:, :, 1], wpos, slopes)
        win_state = wall[:, l:]
    gates = jax.nn.sigmoid(ngate.astype(f32)).reshape(b, l, G, J, 3)
    o_nsa = (gates[..., 0:1] * unblock(o_cmp).astype(f32) + gates[..., 1:2] * unblock(o_slc).astype(f32)
             + gates[..., 2:3] * o_win.astype(f32))
    o_nsa = o_nsa.reshape(b, l, NSA_WIDTH).astype(h.dtype) * jax.nn.silu(nz)

    lg = glr @ lw['gla_wg'] + lw['gla_bg']
    log_a = (jax.nn.log_sigmoid(lg.astype(f32)) / GLA_GATE_TAU).reshape(b, l, GLA_HEADS, GLA_DK)
    s0 = jnp.zeros((b, GLA_HEADS, GLA_DK, GLA_DV), f32) if past is None else past['gla']
    o_g, gla_state = _gla_chunked(gq.reshape(b, l, GLA_HEADS, GLA_DK), gk.reshape(b, l, GLA_HEADS, GLA_DK),
                                  gv.reshape(b, l, GLA_HEADS, GLA_DV), log_a, s0)
    o_gla = _rmsnorm(o_g, lw['gla_norm']).reshape(b, l, GLA_WIDTH).astype(h.dtype) * jax.nn.silu(gz)

    h0 = jnp.zeros((b, S5_GROUPS, S5_STATE, 2), f32) if past is None else past['s5']
    y_s5, s5_state = _s5_scan(su, h0, lw['s5_a_re'], lw['s5_a_im'], lw['s5_b_re'], lw['s5_b_im'],
                              lw['s5_c_re'], lw['s5_c_im'], lw['s5_d'], lw['s5_log_dt'])
    glu = jax.nn.gelu(y_s5).astype(h.dtype) @ lw['s5_w_glu']
    o_s5 = glu[..., :S5_WIDTH] * jax.nn.sigmoid(glu[..., S5_WIDTH:]) * jax.nn.silu(sz)

    h = h + jnp.concatenate([o_nsa, o_gla, o_s5], axis=-1) @ lw['w_out']
    gate = jax.nn.sigmoid(_rmsnorm(h, lw['ple_norm']) @ lw['ple_w_gate'])
    h = h + gate * (ple @ lw['ple_w_proj'])
    return h, (new_cmp, new_slc, win_state, gla_state, s5_state)


def kernel(x_prompt, x_sample, cache_cmp, cache_slc, cache_win, state_gla, state_s5, page_table,
           p_prompt, p_sample, norm_mix, w_in, w_out, cmp_pe, cmp_w1, cmp_w2, gla_wg, gla_bg, gla_norm,
           s5_a_re, s5_a_im, s5_b_re, s5_b_im, s5_c_re, s5_c_im, s5_d, s5_log_dt, s5_w_glu,
           ple_norm, ple_w_gate, ple_w_proj, final_norm):
    depth = w_in.shape[0]
    w_in_r = _relayout_w_in(w_in)
    hp, hs = x_prompt, x_sample
    st_p, st_s = [], []
    for i in range(depth):
        lw = {'norm': norm_mix[i], 'w_in_r': w_in_r[i], 'w_out': w_out[i],
              'cmp_pe': cmp_pe[i], 'cmp_w1': cmp_w1[i], 'cmp_w2': cmp_w2[i],
              'gla_wg': gla_wg[i], 'gla_bg': gla_bg[i], 'gla_norm': gla_norm[i],
              's5_a_re': s5_a_re[i], 's5_a_im': s5_a_im[i], 's5_b_re': s5_b_re[i], 's5_b_im': s5_b_im[i],
              's5_c_re': s5_c_re[i], 's5_c_im': s5_c_im[i], 's5_d': s5_d[i], 's5_log_dt': s5_log_dt[i],
              's5_w_glu': s5_w_glu[i], 'ple_norm': ple_norm[i], 'ple_w_gate': ple_w_gate[i],
              'ple_w_proj': ple_w_proj[i]}
        hp, sp = _hybrid_layer(hp, p_prompt[i], lw, None)
        past = {'page_table': page_table, 'cmp': cache_cmp[i], 'slc': cache_slc[i], 'win': cache_win[i],
                'gla': state_gla[i], 's5': state_s5[i]}
        hs, ss = _hybrid_layer(hs, p_sample[i], lw, past)
        st_p.append(sp)
        st_s.append(ss)
    y_prompt = _rmsnorm(hp, final_norm)
    y_sample = _rmsnorm(hs, final_norm)
    outs = [y_prompt, y_sample]
    for j in range(5):
        outs.append(jnp.stack([s[j] for s in st_p]))
        outs.append(jnp.stack([s[j] for s in st_s]))
    return tuple(outs)
```

```python
import functools
import math

import jax
import jax.numpy as jnp
from jax import lax
from jax.experimental import pallas as pl
from jax.experimental.pallas import tpu as pltpu

D_MODEL = 1024
PLE_DIM = 256
HEAD_DIM = 64
NSA_HEADS = 8
NSA_KV_HEADS = 2
NSA_HPG = NSA_HEADS // NSA_KV_HEADS
NSA_WIDTH = NSA_HEADS * HEAD_DIM
CMP_BLOCK = 32
CMP_STRIDE = 16
SLC_BLOCK = 64
N_SELECT = 16
WINDOW = 512
SPARSE_Q_BLOCK = 64
WIN_Q_BLOCK = 128
GLA_HEADS = 4
GLA_DK = 32
GLA_DV = 64
GLA_WIDTH = GLA_HEADS * GLA_DV
GLA_GATE_RANK = 16
GLA_GATE_TAU = 16.0
GLA_CHUNK = 32
S5_GROUPS = 16
S5_GROUP_CH = 16
S5_STATE = 64
S5_WIDTH = S5_GROUPS * S5_GROUP_CH
S5_CH = S5_GROUPS * S5_STATE
MIX_WIDTH = NSA_WIDTH + GLA_WIDTH + S5_WIDTH
KV_WIDTH = 6 * NSA_KV_HEADS * HEAD_DIM
RMS_EPS = 1e-6
NEG_BIG = 1e9
MASKED = -1e30

LANES = 128
VMEM_LIMIT = 48 * 1024 * 1024

F32 = jnp.float32
BF16 = jnp.bfloat16
ALIBI = tuple(2.0 ** (-8.0 * (h + 1) / NSA_HEADS) for h in range(NSA_HEADS))

_SRC = {}
_off = 0
for _name, _w in (("nq", NSA_WIDTH), ("nkv", KV_WIDTH), ("ngate", 3 * NSA_HEADS), ("nz", NSA_WIDTH),
                  ("gq", GLA_HEADS * GLA_DK), ("gk", GLA_HEADS * GLA_DK), ("gv", GLA_WIDTH),
                  ("glr", GLA_GATE_RANK), ("gz", GLA_WIDTH), ("su", S5_WIDTH), ("sz", S5_WIDTH)):
    _SRC[_name] = (_off, _w)
    _off += _w
IN_WIDTH = _off


def _cols(w, name):
    o, n = _SRC[name]
    return w[..., o:o + n]


_ORDER_A = ("nq", "nkv", "nz", "gq", "gk", "gv", "gz", "su", "sz", "ngate", "glr")
_DST_A = {}
_off = 0
for _name in _ORDER_A:
    _DST_A[_name] = (_off, _SRC[_name][1])
    _off += _SRC[_name][1]
IN_PAD_A = -(-_off // LANES) * LANES


def _relayout_w_in_a(w_in):
    w = jnp.concatenate([_cols(w_in, n) for n in _ORDER_A], axis=-1)
    w = jnp.pad(w, ((0, 0),) * (w.ndim - 1) + ((0, IN_PAD_A - w.shape[-1]),))
    return w.astype(BF16)


def _seg_a(u, name):
    o, w = _DST_A[name]
    return u[..., o:o + w]


QX_W = NSA_HEADS * LANES
GL_W = 2 * GLA_HEADS * GLA_DK + 2 * GLA_WIDTH
GT_W = LANES
_B_SEGS = (("qx", QX_W), ("kv", KV_WIDTH), ("nz", NSA_WIDTH), ("gl", GL_W), ("gt", GT_W),
           ("su", S5_WIDTH), ("sz", S5_WIDTH))
_B_OFF = {}
_off = 0
for _name, _w in _B_SEGS:
    _B_OFF[_name] = (_off, _w)
    _off += _w
IN_PAD_B = _off


def _relayout_w_in_b(w_in):
    d = w_in.shape[:-1]
    wq = _cols(w_in, "nq") * (HEAD_DIM ** -0.5)
    z = jnp.zeros(d + (HEAD_DIM,), w_in.dtype)
    qparts = []
    for h in range(NSA_HEADS):
        wh = wq[..., h * HEAD_DIM:(h + 1) * HEAD_DIM]
        qparts += [wh, z] if h < NSA_HPG else [z, wh]
    gt = jnp.concatenate([_cols(w_in, "ngate"), _cols(w_in, "glr")], axis=-1)
    gt = jnp.pad(gt, ((0, 0),) * len(d) + ((0, GT_W - gt.shape[-1]),))
    w = jnp.concatenate(qparts + [_cols(w_in, "nkv"), _cols(w_in, "nz"), _cols(w_in, "gq"), _cols(w_in, "gk"),
                                  _cols(w_in, "gv"), _cols(w_in, "gz"), gt, _cols(w_in, "su"), _cols(w_in, "sz")],
                        axis=-1)
    return w.astype(BF16)


def _params(*sem):
    return pltpu.CompilerParams(dimension_semantics=sem, vmem_limit_bytes=VMEM_LIMIT)


def _norm_rows(x_ref, g_ref):
    x = x_ref[...]
    ms = jnp.mean(x * x, axis=-1, keepdims=True)
    return (x * lax.rsqrt(ms + RMS_EPS) * g_ref[...]).astype(BF16)


def _in_proj_a_kernel(x_ref, g_ref, w_ref, o_ref):
    o_ref[...] = jnp.dot(_norm_rows(x_ref, g_ref), w_ref[...], preferred_element_type=F32)


def _in_proj_a(x2d, g, w_bf16):
    n, d = x2d.shape
    tm = min(512, n)
    return pl.pallas_call(
        _in_proj_a_kernel,
        out_shape=jax.ShapeDtypeStruct((n, IN_PAD_A), F32),
        grid=(n // tm,),
        in_specs=[pl.BlockSpec((tm, d), lambda i: (i, 0)),
                  pl.BlockSpec((1, d), lambda i: (0, 0)),
                  pl.BlockSpec((d, IN_PAD_A), lambda i: (0, 0))],
        out_specs=pl.BlockSpec((tm, IN_PAD_A), lambda i: (i, 0)),
        compiler_params=_params("parallel"),
        name="in_proj_a",
    )(x2d, g.reshape(1, d), w_bf16)


def _in_proj_b_kernel(x_ref, g_ref, w_ref, qx_ref, kv_ref, kvb_ref, nz_ref, gl_ref, gt_ref, su_ref, sz_ref):
    xn = _norm_rows(x_ref, g_ref)

    def seg(name):
        o, w = _B_OFF[name]
        return jnp.dot(xn, w_ref[:, o:o + w], preferred_element_type=F32)
    qx_ref[...] = seg("qx").astype(BF16)
    kv = seg("kv")
    kv_ref[...] = kv
    kvb_ref[...] = kv.astype(BF16)
    nz_ref[...] = seg("nz")
    gl_ref[...] = seg("gl")
    gt_ref[...] = seg("gt")
    su_ref[...] = seg("su")
    sz_ref[...] = seg("sz")


def _in_proj_b(x, g, w_bf16):
    b, l, d = x.shape
    n = b * l
    tm = min(512, l)
    nt = l // tm
    row = lambda bi, ti: (bi * nt + ti, 0)
    tcol = lambda bi, ti: (ti, bi)
    shapes = [((n, QX_W), BF16), ((n, KV_WIDTH), F32), ((n, KV_WIDTH), BF16), ((n, NSA_WIDTH), F32),
              ((n, GL_W), F32), ((n, GT_W), F32), ((l, b * S5_WIDTH), F32), ((l, b * S5_WIDTH), F32)]
    out_specs = [pl.BlockSpec((tm, s[0][1]), row) for s in shapes[:6]] + \
                [pl.BlockSpec((tm, S5_WIDTH), tcol)] * 2
    return pl.pallas_call(
        _in_proj_b_kernel,
        out_shape=[jax.ShapeDtypeStruct(*s) for s in shapes],
        grid=(b, nt),
        in_specs=[pl.BlockSpec((tm, d), row),
                  pl.BlockSpec((1, d), lambda bi, ti: (0, 0)),
                  pl.BlockSpec((d, IN_PAD_B), lambda bi, ti: (0, 0))],
        out_specs=out_specs,
        compiler_params=_params("parallel", "parallel"),
        name="in_proj_b",
    )(x.reshape(n, d), g.reshape(1, d), w_bf16)


def _compress_kernel(x_ref, pe_ref, w1_ref, w2_ref, o_ref, *, n_ch):
    def half(s):
        acc = jnp.zeros((n_ch, LANES), F32)
        for p in range(CMP_STRIDE):
            xp = x_ref[pl.ds(p, n_ch, stride=CMP_STRIDE), :] + pe_ref[0, s * CMP_STRIDE + p:s * CMP_STRIDE + p + 1, :]
            acc = acc + jnp.dot(xp.astype(BF16), w1_ref[0, s * CMP_STRIDE + p], preferred_element_type=F32)
        return acc
    first = half(0)
    second = pltpu.roll(half(1), n_ch - 1, 0)
    hid = jax.nn.gelu(first + second).astype(BF16)
    o_ref[0, 0] = jnp.dot(hid, w2_ref[0], preferred_element_type=F32).astype(BF16)


def _compress(kv, b, t, pe, w1, w2):
    n_ch = t // CMP_STRIDE
    eye = jnp.eye(NSA_KV_HEADS, dtype=F32)
    pe2 = jnp.tile(pe, (1, 1, NSA_KV_HEADS))
    w1r = w1.reshape(2, CMP_BLOCK, HEAD_DIM, HEAD_DIM)
    w1bd = jnp.einsum('gh,kpde->kpgdhe', eye, w1r).reshape(2, CMP_BLOCK, LANES, LANES).astype(BF16)
    w2bd = jnp.einsum('gh,kde->kgdhe', eye, w2).reshape(2, LANES, LANES).astype(BF16)
    return pl.pallas_call(
        functools.partial(_compress_kernel, n_ch=n_ch),
        out_shape=jax.ShapeDtypeStruct((2, b, n_ch, LANES), BF16),
        grid=(b, 2),
        in_specs=[pl.BlockSpec((t, LANES), lambda bi, s: (bi, s)),
                  pl.BlockSpec((1, CMP_BLOCK, LANES), lambda bi, s: (s, 0, 0)),
                  pl.BlockSpec((1, CMP_BLOCK, LANES, LANES), lambda bi, s: (s, 0, 0, 0)),
                  pl.BlockSpec((1, LANES, LANES), lambda bi, s: (s, 0, 0))],
        out_specs=pl.BlockSpec((1, 1, n_ch, LANES), lambda bi, s: (s, bi, 0, 0)),
        compiler_params=_params("parallel", "parallel"),
        name="nsa_compress",
    )(kv, pe2, w1bd, w2bd)


def _split3(x):
    hi = x.astype(BF16)
    r = x - hi.astype(F32)
    mid = r.astype(BF16)
    lo = (r - mid.astype(F32)).astype(BF16)
    return hi, mid, lo


SEL_ROWS = 64


def _cmp_sel_kernel(qx_ref, kc_ref, vc_ref, gt_ref, m2_ref, o_ref, sel_ref, sc_s, *, tq, nc):
    q0 = pl.program_id(1) * tq
    qpos = q0 + lax.broadcasted_iota(jnp.int32, (tq, 1), 0)
    cend = lax.broadcasted_iota(jnp.int32, (1, nc), 1) * CMP_STRIDE + (CMP_BLOCK - 1)
    dist = (qpos - cend).astype(F32)
    ok = dist >= 0.0
    kc = kc_ref[0, 0]
    vc = vc_ref[0, 0]
    gts = jax.nn.sigmoid(gt_ref[...])
    imps = []
    for g in range(NSA_KV_HEADS):
        imp = jnp.zeros((tq, nc), F32)
        for j in range(NSA_HPG):
            h = g * NSA_HPG + j
            s = lax.dot_general(qx_ref[:, h * LANES:(h + 1) * LANES], kc, (((1,), (1,)), ((), ())),
                                preferred_element_type=F32)
            s = jnp.where(ok, s - ALIBI[h] * dist, MASKED)
            e = jnp.where(ok, jnp.exp(s - jnp.max(s, axis=-1, keepdims=True)), 0.0)
            p = e / jnp.maximum(jnp.sum(e, axis=-1, keepdims=True), 1e-30)
            imp = imp + p
            o = jnp.dot(p.astype(BF16), vc, preferred_element_type=F32)
            o_ref[:, h * HEAD_DIM:(h + 1) * HEAD_DIM] = (o[:, g * HEAD_DIM:(g + 1) * HEAD_DIM]
                                                         * gts[:, 3 * h:3 * h + 1])
        imps.append(imp)
    m2 = m2_ref[...]
    score = sum(jnp.dot(piece, m2, preferred_element_type=F32) for piece in _split3(jnp.concatenate(imps, axis=1)))
    lane = lax.broadcasted_iota(jnp.int32, (1, LANES), 1)
    blk = lane & (SLC_BLOCK - 1)
    cur = qpos >> 6
    forced = (blk == 0) | (blk == cur) | (blk == cur - 1)
    elig = (blk * SLC_BLOCK) <= qpos
    sc_s[...] = jnp.where(elig, jnp.where(forced, NEG_BIG, score), -NEG_BIG)

    def rows(r, carry):
        r0 = pl.multiple_of(r * SEL_ROWS, SEL_ROWS)
        sc = sc_s[pl.ds(r0, SEL_ROWS), :]
        cnt = jnp.zeros((SEL_ROWS, LANES), jnp.int32)
        for i in range(SLC_BLOCK):
            col = jnp.where(lane < SLC_BLOCK, sc[:, i:i + 1], sc[:, SLC_BLOCK + i:SLC_BLOCK + i + 1])
            tie = jnp.where(blk > i, 1, 0)
            cnt = cnt + jnp.where(col > sc, 1, jnp.where(col == sc, tie, 0))
        keep = (cnt < N_SELECT) & (sc > -0.5 * NEG_BIG)
        sel_ref[pl.ds(r0, SEL_ROWS), :] = jnp.where(keep, 1.0, 0.0).astype(BF16)
        return carry
    lax.fori_loop(0, tq // SEL_ROWS, rows, 0)


def _slc_map_matrix(nc, n_cmp):
    n = jnp.arange(nc)[:, None]
    j = jnp.arange(SLC_BLOCK)[None, :]
    ratio = SLC_BLOCK // CMP_STRIDE
    m = ((n >= ratio * j) & (n <= ratio * j + ratio - 1)).astype(F32) + \
        ((n >= ratio * j - 1) & (n <= ratio * j + ratio - 2)).astype(F32)
    m = jnp.where(n < n_cmp, m, 0.0)
    z = jnp.zeros_like(m)
    return jnp.concatenate([jnp.concatenate([m, z], 1), jnp.concatenate([z, m], 1)], 0).astype(BF16)


def _cmp_select(qx, kvc, gt, b, l):
    nc = kvc.shape[2]
    tq = min(256, l)
    nt = l // tq
    row = lambda bi, ti: (bi * nt + ti, 0)
    return pl.pallas_call(
        functools.partial(_cmp_sel_kernel, tq=tq, nc=nc),
        out_shape=[jax.ShapeDtypeStruct((b * l, NSA_WIDTH), F32), jax.ShapeDtypeStruct((b * l, LANES), BF16)],
        grid=(b, nt),
        in_specs=[pl.BlockSpec((tq, QX_W), row),
                  pl.BlockSpec((1, 1, nc, LANES), lambda bi, ti: (0, bi, 0, 0)),
                  pl.BlockSpec((1, 1, nc, LANES), lambda bi, ti: (1, bi, 0, 0)),
                  pl.BlockSpec((tq, GT_W), row),
                  pl.BlockSpec((2 * nc, LANES), lambda bi, ti: (0, 0))],
        out_specs=[pl.BlockSpec((tq, NSA_WIDTH), row), pl.BlockSpec((tq, LANES), row)],
        scratch_shapes=[pltpu.VMEM((tq, LANES), F32)],
        compiler_params=_params("parallel", "parallel"),
        name="nsa_cmp_select",
    )(qx, kvc, kvc, gt, _slc_map_matrix(nc, nc - 1))


def _flash_kernel(qx_ref, k_ref, v_ref, sel_ref, e_ref, gt_ref, o_ref, acc_s, m_s, l_s, *, tq, mode, gate_col):
    i = pl.program_id(1)
    q0 = i * tq
    d0 = (lax.broadcasted_iota(jnp.int32, (tq, tq), 0) - lax.broadcasted_iota(jnp.int32, (tq, tq), 1)).astype(F32)
    gts = jax.nn.sigmoid(gt_ref[...])

    def tile(g, j, causal, winlow):
        k0 = pl.multiple_of(j * tq, tq)
        kt = k_ref[pl.ds(k0, tq), :]
        vt = v_ref[pl.ds(k0, tq), :]
        dist = d0 + (q0 - k0).astype(F32)
        km = None
        if mode == "slc":
            km = jnp.dot(sel_ref[...], e_ref[g, :, pl.ds(k0, tq)], preferred_element_type=F32) > 0.5
        if causal:
            c = dist >= 0.0
            km = c if km is None else (km & c)
        if winlow:
            c = dist < float(WINDOW)
            km = c if km is None else (km & c)
        for jh in range(NSA_HPG):
            h = g * NSA_HPG + jh
            s = lax.dot_general(qx_ref[:, h * LANES:(h + 1) * LANES], kt, (((1,), (1,)), ((), ())),
                                preferred_element_type=F32)
            s = s - ALIBI[h] * dist
            if km is not None:
                s = jnp.where(km, s, MASKED)
            m_prev = m_s[jh][:, :1]
            m_new = jnp.maximum(m_prev, jnp.max(s, axis=-1, keepdims=True))
            alpha = jnp.exp(m_prev - m_new)
            p = jnp.exp(s - m_new)
            l_s[jh] = jnp.broadcast_to(alpha * l_s[jh][:, :1] + jnp.sum(p, axis=-1, keepdims=True), (tq, LANES))
            acc_s[jh] = alpha * acc_s[jh] + jnp.dot(p.astype(BF16), vt, preferred_element_type=F32)
            m_s[jh] = jnp.broadcast_to(m_new, (tq, LANES))

    for g in range(NSA_KV_HEADS):
        m_s[...] = jnp.full(m_s.shape, MASKED, F32)
        l_s[...] = jnp.zeros(l_s.shape, F32)
        acc_s[...] = jnp.zeros(acc_s.shape, F32)
        if mode == "slc":
            def body(j, carry):
                tile(g, j, False, False)
                return carry
            lax.fori_loop(0, i, body, 0)
        else:
            span = WINDOW // tq
            for back in range(span, 0, -1):
                @pl.when(i >= back)
                def _():
                    tile(g, i - back, False, back == span)
        tile(g, i, True, False)
        for jh in range(NSA_HPG):
            h = g * NSA_HPG + jh
            c = 3 * h + gate_col
            o = acc_s[jh][:, g * HEAD_DIM:(g + 1) * HEAD_DIM] * (gts[:, c:c + 1] / l_s[jh][:, :1])
            o_ref[:, h * HEAD_DIM:(h + 1) * HEAD_DIM] = o


def _block_expand_matrix(l):
    blk = jnp.arange(SLC_BLOCK)[:, None] == (jnp.arange(l)[None, :] // SLC_BLOCK)
    z = jnp.zeros_like(blk)
    return jnp.stack([jnp.concatenate([blk, z], 0), jnp.concatenate([z, blk], 0)]).astype(BF16)


def _flash(qx, kvb, sel, gt, b, l, mode):
    tq = min(256, l)
    nt = l // tq
    row = lambda bi, ti: (bi * nt + ti, 0)
    branch = 1 if mode == "slc" else 2
    kcol, vcol = 2 * branch, 2 * branch + 1
    emat = _block_expand_matrix(l)
    return pl.pallas_call(
        functools.partial(_flash_kernel, tq=tq, mode=mode, gate_col=branch),
        out_shape=jax.ShapeDtypeStruct((b * l, NSA_WIDTH), F32),
        grid=(b, nt),
        in_specs=[pl.BlockSpec((tq, QX_W), row),
                  pl.BlockSpec((l, LANES), lambda bi, ti: (bi, kcol)),
                  pl.BlockSpec((l, LANES), lambda bi, ti: (bi, vcol)),
                  pl.BlockSpec((tq, LANES), row),
                  pl.BlockSpec((NSA_KV_HEADS, LANES, l), lambda bi, ti: (0, 0, 0)),
                  pl.BlockSpec((tq, GT_W), row)],
        out_specs=pl.BlockSpec((tq, NSA_WIDTH), row),
        scratch_shapes=[pltpu.VMEM((NSA_HPG, tq, LANES), F32)] * 3,
        compiler_params=_params("parallel", "parallel"),
        name="nsa_" + mode,
    )(qx, kvb, kvb, sel, emat, gt)


def _s5_kernel(u_ref, z_ref, bre_ref, bim_ref, cre_ref, cim_ref, ab_ref, d_ref, wglu_ref, h0_ref,
               o_ref, hout_ref, xr_s, xi_s, h_s, *, nb, tc):
    @pl.when(pl.program_id(0) == 0)
    def _():
        h_s[...] = h0_ref[...]
    bre, bim = bre_ref[...], bim_ref[...]
    for b in range(nb):
        ub = u_ref[:, b * S5_WIDTH:(b + 1) * S5_WIDTH].astype(BF16)
        xr_s[:, b * S5_CH:(b + 1) * S5_CH] = jnp.dot(ub, bre, preferred_element_type=F32)
        xi_s[:, b * S5_CH:(b + 1) * S5_CH] = jnp.dot(ub, bim, preferred_element_type=F32)
    abr = ab_ref[0:1, :]
    abi = ab_ref[1:2, :]

    def step(t, carry):
        hr, hi = carry
        nr = abr * hr - abi * hi + xr_s[pl.ds(t, 1), :]
        ni = abr * hi + abi * hr + xi_s[pl.ds(t, 1), :]
        xr_s[pl.ds(t, 1), :] = nr
        xi_s[pl.ds(t, 1), :] = ni
        return nr, ni
    hr, hi = lax.fori_loop(0, tc, step, (h_s[0:1, :], h_s[1:2, :]), unroll=8)
    h_s[0:1, :] = hr
    h_s[1:2, :] = hi
    hout_ref[...] = h_s[...]
    cre, cim, wglu = cre_ref[...], cim_ref[...], wglu_ref[...]
    for b in range(nb):
        u = u_ref[:, b * S5_WIDTH:(b + 1) * S5_WIDTH]
        y = (jnp.dot(xr_s[:, b * S5_CH:(b + 1) * S5_CH].astype(BF16), cre, preferred_element_type=F32)
             - jnp.dot(xi_s[:, b * S5_CH:(b + 1) * S5_CH].astype(BF16), cim, preferred_element_type=F32)
             + d_ref[...] * u)
        gl = jnp.dot(jax.nn.gelu(y).astype(BF16), wglu, preferred_element_type=F32)
        z = z_ref[:, b * S5_WIDTH:(b + 1) * S5_WIDTH]
        o_ref[:, b * S5_WIDTH:(b + 1) * S5_WIDTH] = (gl[:, :S5_WIDTH] * jax.nn.sigmoid(gl[:, S5_WIDTH:])
                                                       * (z * jax.nn.sigmoid(z)))


def _s5_discretise(lw):
    a_re, a_im = lw['s5_a_re'].astype(F32), lw['s5_a_im'].astype(F32)
    b_re, b_im = lw['s5_b_re'].astype(F32), lw['s5_b_im'].astype(F32)
    dt = jnp.exp(lw['s5_log_dt'].astype(F32))[:, None]
    mag = jnp.exp(a_re * dt)
    abr, abi = mag * jnp.cos(a_im * dt), mag * jnp.sin(a_im * dt)
    den = a_re * a_re + a_im * a_im
    nr = abr - 1.0
    cr = (nr * a_re + abi * a_im) / den
    ci = (abi * a_re - nr * a_im) / den
    bbr = cr[..., None] * b_re - ci[..., None] * b_im
    bbi = cr[..., None] * b_im + ci[..., None] * b_re
    eye = jnp.eye(S5_GROUPS, dtype=F32)
    bd_in = lambda m: jnp.einsum('gh,gpc->gchp', eye, m).reshape(S5_WIDTH, S5_CH).astype(BF16)
    bd_out = lambda m: jnp.einsum('gh,gcp->gphc', eye, m.astype(F32)).reshape(S5_CH, S5_WIDTH).astype(BF16)
    ab = jnp.stack([abr.reshape(S5_CH), abi.reshape(S5_CH)])
    return bd_in(bbr), bd_in(bbi), bd_out(lw['s5_c_re']), bd_out(lw['s5_c_im']), ab


def _s5(su, sz, lw, h0, nb, l):
    bre, bim, cre, cim, ab = _s5_discretise(lw)
    tc = min(256, l)
    const = lambda t: (0, 0)
    return pl.pallas_call(
        functools.partial(_s5_kernel, nb=nb, tc=tc),
        out_shape=[jax.ShapeDtypeStruct((l, nb * S5_WIDTH), F32), jax.ShapeDtypeStruct((2, nb * S5_CH), F32)],
        grid=(l // tc,),
        in_specs=[pl.BlockSpec((tc, nb * S5_WIDTH), lambda t: (t, 0)),
                  pl.BlockSpec((tc, nb * S5_WIDTH), lambda t: (t, 0)),
                  pl.BlockSpec((S5_WIDTH, S5_CH), const), pl.BlockSpec((S5_WIDTH, S5_CH), const),
                  pl.BlockSpec((S5_CH, S5_WIDTH), const), pl.BlockSpec((S5_CH, S5_WIDTH), const),
                  pl.BlockSpec((2, nb * S5_CH), const), pl.BlockSpec((1, S5_WIDTH), const),
                  pl.BlockSpec((S5_WIDTH, 2 * S5_WIDTH), const), pl.BlockSpec((2, nb * S5_CH), const)],
        out_specs=[pl.BlockSpec((tc, nb * S5_WIDTH), lambda t: (t, 0)), pl.BlockSpec((2, nb * S5_CH), const)],
        scratch_shapes=[pltpu.VMEM((tc, nb * S5_CH), F32), pltpu.VMEM((tc, nb * S5_CH), F32),
                        pltpu.VMEM((2, nb * S5_CH), F32)],
        compiler_params=_params("arbitrary"),
        name="s5",
    )(su, sz, bre, bim, cre, cim, jnp.tile(ab, (1, nb)), lw['s5_d'].astype(F32).reshape(1, S5_WIDTH),
      lw['s5_w_glu'].astype(BF16), h0)


def _rmsnorm(x, g):
    xf = x.astype(jnp.float32)
    y = xf * lax.rsqrt(jnp.mean(xf * xf, axis=-1, keepdims=True) + RMS_EPS)
    return (y * g.astype(jnp.float32)).astype(x.dtype)


def _masked_softmax(s, mask):
    s = jnp.where(mask, s, -1e30)
    e = jnp.where(mask, jnp.exp(s - jnp.max(s, axis=-1, keepdims=True)), 0.0)
    return e / jnp.maximum(jnp.sum(e, axis=-1, keepdims=True), 1e-30)


def _alibi_slopes():
    return jnp.exp2(-8.0 * jnp.arange(1, NSA_HEADS + 1, dtype=jnp.float32) / NSA_HEADS)


def _compress_kv(x, pe, w1, w2):
    b, t, g, d = x.shape
    ratio = CMP_BLOCK // CMP_STRIDE
    n_ch = t // CMP_STRIDE
    n_cmp = n_ch - ratio + 1
    ch = x[:, :n_ch * CMP_STRIDE].reshape(b, n_ch, CMP_STRIDE, g, d)
    blocks = jnp.concatenate([ch[:, s:s + n_cmp] for s in range(ratio)], axis=2)
    blocks = blocks + pe[None, None, :, None, :]
    flat = jnp.moveaxis(blocks, 2, 3).reshape(b, n_cmp, g, CMP_BLOCK * d)
    return jax.nn.gelu(flat @ w1) @ w2


def _nsa_sparse_block(q, qpos, kc, vc, cend, ksb, vsb, slopes):
    f32 = jnp.float32
    b, qb = q.shape[:2]
    sl = slopes.reshape(NSA_KV_HEADS, NSA_HPG)
    scale = HEAD_DIM ** -0.5
    s = jnp.einsum('bqghd,bngd->bqghn', q, kc).astype(f32) * scale
    dist = (qpos[:, None] - cend[None, :]).astype(f32)
    s = s - sl[:, :, None] * dist[:, None, None, :]
    p = _masked_softmax(s, (cend[None, :] <= qpos[:, None])[:, None, None, :])
    o_cmp = jnp.einsum('bqghn,bngd->bqghd', p.astype(vc.dtype), vc)
    imp = p.sum(axis=3)
    n_cmp = imp.shape[-1]
    n_slc = ksb.shape[2]
    run = imp
    for sft in range(1, CMP_BLOCK // CMP_STRIDE):
        run = run + jnp.pad(imp, ((0, 0), (0, 0), (0, 0), (sft, 0)))[..., :n_cmp]
    ratio = SLC_BLOCK // CMP_STRIDE
    run = jnp.pad(run, ((0, 0), (0, 0), (0, 0), (0, ratio * n_slc - n_cmp)))
    score = run.reshape(b, qb, NSA_KV_HEADS, n_slc, ratio).sum(-1)
    blk = jnp.arange(n_slc)
    cur = qpos // SLC_BLOCK
    eligible = blk[None, :] * SLC_BLOCK <= qpos[:, None]
    forced = (blk[None, :] == 0) | (blk[None, :] == cur[:, None]) | (blk[None, :] == cur[:, None] - 1)
    score = jnp.where(forced[:, None, :], NEG_BIG, score)
    score = jnp.where(eligible[:, None, :], score, -NEG_BIG)
    top_val, top_idx = lax.top_k(score, min(N_SELECT, n_slc))
    sel_ok = top_val > -0.5 * NEG_BIG
    b_ix = jnp.arange(b)[:, None, None, None]
    g_ix = jnp.arange(NSA_KV_HEADS)[None, None, :, None]
    ksel = ksb[b_ix, g_ix, top_idx]
    vsel = vsb[b_ix, g_ix, top_idx]
    kpos = top_idx[..., None] * SLC_BLOCK + jnp.arange(SLC_BLOCK)
    s2 = jnp.einsum('bqghd,bqgkjd->bqghkj', q, ksel).astype(f32) * scale
    dist2 = (qpos[None, :, None, None, None] - kpos).astype(f32)
    s2 = s2 - sl[None, None, :, :, None, None] * dist2[:, :, :, None]
    ok2 = (sel_ok[..., None] & (dist2 >= 0))[:, :, :, None]
    shp = s2.shape
    nk = shp[4] * shp[5]
    p2 = _masked_softmax(s2.reshape(shp[:4] + (nk,)), ok2.reshape(ok2.shape[:4] + (nk,))).reshape(shp)
    o_slc = jnp.einsum('bqghkj,bqgkjd->bqghd', p2.astype(vsel.dtype), vsel)
    return o_cmp, o_slc


def _window_attn(q, qpos, k, v, kpos, slopes):
    sl = slopes.reshape(NSA_KV_HEADS, NSA_HPG)
    s = jnp.einsum('bqghd,bkgd->bqghk', q, k).astype(jnp.float32) * HEAD_DIM ** -0.5
    dist = (qpos[:, None] - kpos[None, :]).astype(jnp.float32)
    s = s - sl[:, :, None] * dist[:, None, None, :]
    ok = ((dist >= 0) & (dist < WINDOW) & (kpos[None, :] >= 0))[:, None, None, :]
    p = _masked_softmax(s, ok)
    return jnp.einsum('bqghk,bkgd->bqghd', p.astype(v.dtype), v)


def _gla_chunked(q, k, v, log_a, s0):
    f32 = jnp.float32
    b, l = q.shape[:2]
    c = min(GLA_CHUNK, l)
    pad = (-l) % c
    n = (l + pad) // c

    def prep(t):
        t = jnp.pad(t.astype(f32), ((0, 0), (0, pad), (0, 0), (0, 0)))
        return t.reshape(b, n, c, GLA_HEADS, t.shape[-1])
    q = prep(q) * GLA_DK ** -0.5
    k, v, log_a = prep(k), prep(v), prep(log_a)
    cum = jnp.cumsum(log_a, axis=2)
    causal = jnp.tril(jnp.ones((c, c), dtype=bool))[:, :, None, None]
    diff = cum[:, :, :, None] - cum[:, :, None, :]
    decay = jnp.where(causal, jnp.exp(jnp.where(causal, diff, 0.0)), 0.0)
    att = jnp.einsum('bnthd,bnshd,bntshd->bnhts', q, k, decay)
    o_intra = jnp.einsum('bnhts,bnshe->bnthe', att, v)
    q_dec = q * jnp.exp(cum)
    k_dec = k * jnp.exp(cum[:, :, -1:] - cum)
    a_tot = jnp.exp(cum[:, :, -1])

    def step(state, inp):
        qd, kd, vv, at = inp
        o = jnp.einsum('bchd,bhde->bche', qd, state)
        state = at[..., None] * state + jnp.einsum('bchd,bche->bhde', kd, vv)
        return state, o
    xs = (jnp.moveaxis(q_dec, 1, 0), jnp.moveaxis(k_dec, 1, 0), jnp.moveaxis(v, 1, 0), jnp.moveaxis(a_tot, 1, 0))
    s_fin, o_inter = lax.scan(step, s0.astype(f32), xs)
    o = o_intra + jnp.moveaxis(o_inter, 0, 1)
    return o.reshape(b, n * c, GLA_HEADS, GLA_DV)[:, :l], s_fin


def _s5_combine(e1, e2):
    a1r, a1i, b1r, b1i = e1
    a2r, a2i, b2r, b2i = e2
    return (a2r * a1r - a2i * a1i, a2r * a1i + a2i * a1r,
            a2r * b1r - a2i * b1i + b2r, a2r * b1i + a2i * b1r + b2i)


def _s5_scan(u, h0, a_re, a_im, b_re, b_im, c_re, c_im, d, log_dt):
    f32 = jnp.float32
    b, l, _ = u.shape
    dt = jnp.exp(log_dt.astype(f32))[:, None]
    mag = jnp.exp(a_re * dt)
    abr, abi = mag * jnp.cos(a_im * dt), mag * jnp.sin(a_im * dt)
    den = a_re * a_re + a_im * a_im
    nr = abr - 1.0
    cr = (nr * a_re + abi * a_im) / den
    ci = (abi * a_re - nr * a_im) / den
    bbr = cr[..., None] * b_re - ci[..., None] * b_im
    bbi = cr[..., None] * b_im + ci[..., None] * b_re
    ug = u.reshape(b, l, S5_GROUPS, S5_GROUP_CH)
    xr = jnp.einsum('gpc,blgc->blgp', bbr, ug)
    xi = jnp.einsum('gpc,blgc->blgp', bbi, ug)
    h0r, h0i = h0[..., 0].astype(f32), h0[..., 1].astype(f32)
    xr = xr.at[:, 0].add(abr * h0r - abi * h0i)
    xi = xi.at[:, 0].add(abr * h0i + abi * h0r)
    ar = jnp.broadcast_to(abr, xr.shape)
    ai = jnp.broadcast_to(abi, xr.shape)
    _, _, hr, hi = lax.associative_scan(_s5_combine, (ar, ai, xr, xi), axis=1)
    y = jnp.einsum('gcp,blgp->blgc', c_re, hr) - jnp.einsum('gcp,blgp->blgc', c_im, hi)
    y = y.reshape(b, l, S5_WIDTH) + d.astype(f32) * u
    return y, jnp.stack([hr[:, -1], hi[:, -1]], axis=-1)


def _gla_mixer(gq, gk, gv, glr, gz, lw, s0, b, l, dtype):
    f32 = jnp.float32
    lg = glr @ lw['gla_wg'] + lw['gla_bg']
    log_a = (jax.nn.log_sigmoid(lg.astype(f32)) / GLA_GATE_TAU).reshape(b, l, GLA_HEADS, GLA_DK)
    o_g, gla_state = _gla_chunked(gq.reshape(b, l, GLA_HEADS, GLA_DK), gk.reshape(b, l, GLA_HEADS, GLA_DK),
                                  gv.reshape(b, l, GLA_HEADS, GLA_DV), log_a, s0)
    o_gla = _rmsnorm(o_g, lw['gla_norm']).reshape(b, l, GLA_WIDTH).astype(dtype) * jax.nn.silu(gz)
    return o_gla, gla_state


def _merge(h, o_nsa, o_gla, o_s5, ple, lw):
    h = h + jnp.concatenate([o_nsa, o_gla, o_s5], axis=-1) @ lw['w_out']
    gate = jax.nn.sigmoid(_rmsnorm(h, lw['ple_norm']) @ lw['ple_w_gate'])
    return h + gate * (ple @ lw['ple_w_proj'])


def _prompt_layer(h, ple, lw):
    b, l, _ = h.shape
    n = b * l
    kvrow = (2, NSA_KV_HEADS, HEAD_DIM)
    qx, kv, kvb, nz, gl, gt, su, sz = _in_proj_b(h, lw['norm'], lw['w_in_b'])
    kvc = _compress(kv, b, l, lw['cmp_pe'], lw['cmp_w1'], lw['cmp_w2'])
    o_cmp, sel = _cmp_select(qx, kvc, gt, b, l)
    o_slc = _flash(qx, kvb, sel, gt, b, l, "slc")
    o_win = _flash(qx, kvb, sel, gt, b, l, "win")
    o_nsa = ((o_cmp + o_slc + o_win) * jax.nn.silu(nz)).reshape(b, l, NSA_WIDTH)
    kv5 = kv.reshape(b, l, 3, *kvrow)
    new_cmp, new_slc, new_win = kv5[:, :, 0], kv5[:, :, 1], kv5[:, :, 2]
    win_state = new_win[:, l - min(WINDOW, l):]

    glr = gt[:, 3 * NSA_HEADS:3 * NSA_HEADS + GLA_GATE_RANK].reshape(b, l, GLA_GATE_RANK)
    dk = GLA_HEADS * GLA_DK
    gl3 = gl.reshape(b, l, GL_W)
    s0 = jnp.zeros((b, GLA_HEADS, GLA_DK, GLA_DV), F32)
    o_gla, gla_state = _gla_mixer(gl3[..., :dk], gl3[..., dk:2 * dk], gl3[..., 2 * dk:2 * dk + GLA_WIDTH], glr,
                                  gl3[..., 2 * dk + GLA_WIDTH:], lw, s0, b, l, h.dtype)

    o_s5_t, hfin = _s5(su, sz, lw, jnp.zeros((2, b * S5_CH), F32), b, l)
    o_s5 = jnp.transpose(o_s5_t.reshape(l, b, S5_WIDTH), (1, 0, 2))
    s5_state = jnp.transpose(hfin.reshape(2, b, S5_GROUPS, S5_STATE), (1, 2, 3, 0))

    h = _merge(h, o_nsa, o_gla, o_s5, ple, lw)
    return h, (new_cmp, new_slc, win_state, gla_state, s5_state)


def _sample_layer(h, ple, lw, past):
    f32 = jnp.float32
    b, l, _ = h.shape
    G, J, E = NSA_KV_HEADS, NSA_HPG, HEAD_DIM
    slopes = _alibi_slopes()
    u = _in_proj_a(h.reshape(b * l, D_MODEL), lw['norm'], lw['w_in_a']).reshape(b, l, IN_PAD_A)
    nq, nkv, ngate, nz = _seg_a(u, "nq"), _seg_a(u, "nkv"), _seg_a(u, "ngate"), _seg_a(u, "nz")
    gq, gk, gv, glr, gz = _seg_a(u, "gq"), _seg_a(u, "gk"), _seg_a(u, "gv"), _seg_a(u, "glr"), _seg_a(u, "gz")
    su, sz = _seg_a(u, "su"), _seg_a(u, "sz")

    q = nq.reshape(b, l, G, J, E)
    kv = nkv.reshape(b, l, 3, 2, G, E)
    new_cmp, new_slc, new_win = kv[:, :, 0], kv[:, :, 1], kv[:, :, 2]
    pt = past['page_table']
    pos0 = pt.shape[1] * past['cmp'].shape[1]

    def gather_past(pool):
        rows = pool[pt]
        return rows.reshape((b, pos0) + rows.shape[3:])
    full_cmp = jnp.concatenate([gather_past(past['cmp']), new_cmp], axis=1)
    full_slc = jnp.concatenate([gather_past(past['slc']), new_slc], axis=1)
    t_all = pos0 + l
    kc = _compress_kv(full_cmp[:, :, 0], lw['cmp_pe'][0], lw['cmp_w1'][0], lw['cmp_w2'][0])
    vc = _compress_kv(full_cmp[:, :, 1], lw['cmp_pe'][1], lw['cmp_w1'][1], lw['cmp_w2'][1])
    cend = jnp.arange(kc.shape[1]) * CMP_STRIDE + (CMP_BLOCK - 1)
    n_slc = -(-t_all // SLC_BLOCK)
    slc = jnp.pad(full_slc, ((0, 0), (0, n_slc * SLC_BLOCK - t_all), (0, 0), (0, 0), (0, 0)))
    slc = jnp.transpose(slc.reshape(b, n_slc, SLC_BLOCK, 2, G, E), (3, 0, 4, 1, 2, 5))
    ksb, vsb = slc[0], slc[1]
    qpos = pos0 + jnp.arange(l)
    o_cmp, o_slc = _nsa_sparse_block(q, qpos, kc, vc, cend, ksb, vsb, slopes)
    buf = past['win']
    wall = jnp.concatenate([buf, new_win], axis=1)
    wpos = pos0 - buf.shape[1] + jnp.arange(wall.shape[1])
    o_win = _window_attn(q, qpos, wall[:, :, 0], wall[:, :, 1], wpos, slopes)
    win_state = wall[:, l:]
    gates = jax.nn.sigmoid(ngate.astype(f32)).reshape(b, l, G, J, 3)
    o_nsa = (gates[..., 0:1] * o_cmp.astype(f32) + gates[..., 1:2] * o_slc.astype(f32)
             + gates[..., 2:3] * o_win.astype(f32))
    o_nsa = o_nsa.reshape(b, l, NSA_WIDTH).astype(h.dtype) * jax.nn.silu(nz)

    o_gla, gla_state = _gla_mixer(gq, gk, gv, glr, gz, lw, past['gla'], b, l, h.dtype)

    y_s5, s5_state = _s5_scan(su, past['s5'], lw['s5_a_re'], lw['s5_a_im'], lw['s5_b_re'], lw['s5_b_im'],
                              lw['s5_c_re'], lw['s5_c_im'], lw['s5_d'], lw['s5_log_dt'])
    glu = jax.nn.gelu(y_s5).astype(h.dtype) @ lw['s5_w_glu']
    o_s5 = glu[..., :S5_WIDTH] * jax.nn.sigmoid(glu[..., S5_WIDTH:]) * jax.nn.silu(sz)

    h = _merge(h, o_nsa, o_gla, o_s5, ple, lw)
    return h, (new_cmp, new_slc, win_state, gla_state, s5_state)


def kernel(x_prompt, x_sample, cache_cmp, cache_slc, cache_win, state_gla, state_s5, page_table,
           p_prompt, p_sample, norm_mix, w_in, w_out, cmp_pe, cmp_w1, cmp_w2, gla_wg, gla_bg, gla_norm,
           s5_a_re, s5_a_im, s5_b_re, s5_b_im, s5_c_re, s5_c_im, s5_d, s5_log_dt, s5_w_glu,
           ple_norm, ple_w_gate, ple_w_proj, final_norm):
    depth = w_in.shape[0]
    w_in_a = _relayout_w_in_a(w_in)
    w_in_b = _relayout_w_in_b(w_in)
    hp, hs = x_prompt, x_sample
    st_p, st_s = [], []
    for i in range(depth):
        lw = {'norm': norm_mix[i], 'w_in_a': w_in_a[i], 'w_in_b': w_in_b[i], 'w_out': w_out[i],
              'cmp_pe': cmp_pe[i], 'cmp_w1': cmp_w1[i], 'cmp_w2': cmp_w2[i],
              'gla_wg': gla_wg[i], 'gla_bg': gla_bg[i], 'gla_norm': gla_norm[i],
              's5_a_re': s5_a_re[i], 's5_a_im': s5_a_im[i], 's5_b_re': s5_b_re[i], 's5_b_im': s5_b_im[i],
              's5_c_re': s5_c_re[i], 's5_c_im': s5_c_im[i], 's5_d': s5_d[i], 's5_log_dt': s5_log_dt[i],
              's5_w_glu': s5_w_glu[i], 'ple_norm': ple_norm[i], 'ple_w_gate': ple_w_gate[i],
              'ple_w_proj': ple_w_proj[i]}
        hp, sp = _prompt_layer(hp, p_prompt[i], lw)
        past = {'page_table': page_table, 'cmp': cache_cmp[i], 'slc': cache_slc[i], 'win': cache_win[i],
                'gla': state_gla[i], 's5': state_s5[i]}
        hs, ss = _sample_layer(hs, p_sample[i], lw, past)
        st_p.append(sp)
        st_s.append(ss)
    y_prompt = _rmsnorm(hp, final_norm)
    y_sample = _rmsnorm(hs, final_norm)
    outs = [y_prompt, y_sample]
    for j in range(5):
        outs.append(jnp.stack([s[j] for s in st_p]))
        outs.append(jnp.stack([s[j] for s in st_s]))
    return tuple(outs)
```

```python
import functools
import math

import jax
import jax.numpy as jnp
from jax import lax
from jax.experimental import pallas as pl
from jax.experimental.pallas import tpu as pltpu

D_MODEL = 1024
PLE_DIM = 256
HEAD_DIM = 64
NSA_HEADS = 8
NSA_KV_HEADS = 2
NSA_HPG = NSA_HEADS // NSA_KV_HEADS
NSA_WIDTH = NSA_HEADS * HEAD_DIM
CMP_BLOCK = 32
CMP_STRIDE = 16
SLC_BLOCK = 64
N_SELECT = 16
WINDOW = 512
SPARSE_Q_BLOCK = 64
WIN_Q_BLOCK = 128
GLA_HEADS = 4
GLA_DK = 32
GLA_DV = 64
GLA_WIDTH = GLA_HEADS * GLA_DV
GLA_GATE_RANK = 16
GLA_GATE_TAU = 16.0
GLA_CHUNK = 32
S5_GROUPS = 16
S5_GROUP_CH = 16
S5_STATE = 64
S5_WIDTH = S5_GROUPS * S5_GROUP_CH
S5_CH = S5_GROUPS * S5_STATE
MIX_WIDTH = NSA_WIDTH + GLA_WIDTH + S5_WIDTH
KV_WIDTH = 6 * NSA_KV_HEADS * HEAD_DIM
RMS_EPS = 1e-6
NEG_BIG = 1e9
MASKED = -1e30

LANES = 128
VMEM_LIMIT = 48 * 1024 * 1024

F32 = jnp.float32
BF16 = jnp.bfloat16
ALIBI = tuple(2.0 ** (-8.0 * (h + 1) / NSA_HEADS) for h in range(NSA_HEADS))

_SRC = {}
_off = 0
for _name, _w in (("nq", NSA_WIDTH), ("nkv", KV_WIDTH), ("ngate", 3 * NSA_HEADS), ("nz", NSA_WIDTH),
                  ("gq", GLA_HEADS * GLA_DK), ("gk", GLA_HEADS * GLA_DK), ("gv", GLA_WIDTH),
                  ("glr", GLA_GATE_RANK), ("gz", GLA_WIDTH), ("su", S5_WIDTH), ("sz", S5_WIDTH)):
    _SRC[_name] = (_off, _w)
    _off += _w
IN_WIDTH = _off


def _cols(w, name):
    o, n = _SRC[name]
    return w[..., o:o + n]


_ORDER_A = ("nq", "nkv", "nz", "gq", "gk", "gv", "gz", "su", "sz", "ngate", "glr")
_DST_A = {}
_off = 0
for _name in _ORDER_A:
    _DST_A[_name] = (_off, _SRC[_name][1])
    _off += _SRC[_name][1]
IN_PAD_A = -(-_off // LANES) * LANES


def _relayout_w_in_a(w_in):
    w = jnp.concatenate([_cols(w_in, n) for n in _ORDER_A], axis=-1)
    w = jnp.pad(w, ((0, 0),) * (w.ndim - 1) + ((0, IN_PAD_A - w.shape[-1]),))
    return w.astype(BF16)


def _seg_a(u, name):
    o, w = _DST_A[name]
    return u[..., o:o + w]


QX_W = NSA_HEADS * LANES
GL_W = 2 * GLA_HEADS * GLA_DK + 2 * GLA_WIDTH
GT_W = LANES
_B_SEGS = (("qx", QX_W), ("kv", KV_WIDTH), ("nz", NSA_WIDTH), ("gl", GL_W), ("gt", GT_W),
           ("su", S5_WIDTH), ("sz", S5_WIDTH))
_B_OFF = {}
_off = 0
for _name, _w in _B_SEGS:
    _B_OFF[_name] = (_off, _w)
    _off += _w
IN_PAD_B = _off


def _relayout_w_in_b(w_in):
    d = w_in.shape[:-1]
    wq = _cols(w_in, "nq") * (HEAD_DIM ** -0.5)
    z = jnp.zeros(d + (HEAD_DIM,), w_in.dtype)
    qparts = []
    for h in range(NSA_HEADS):
        wh = wq[..., h * HEAD_DIM:(h + 1) * HEAD_DIM]
        qparts += [wh, z] if h < NSA_HPG else [z, wh]
    gt = jnp.concatenate([_cols(w_in, "ngate"), _cols(w_in, "glr")], axis=-1)
    gt = jnp.pad(gt, ((0, 0),) * len(d) + ((0, GT_W - gt.shape[-1]),))
    w = jnp.concatenate(qparts + [_cols(w_in, "nkv"), _cols(w_in, "nz"), _cols(w_in, "gq"), _cols(w_in, "gk"),
                                  _cols(w_in, "gv"), _cols(w_in, "gz"), gt, _cols(w_in, "su"), _cols(w_in, "sz")],
                        axis=-1)
    return w.astype(BF16)


def _params(*sem):
    return pltpu.CompilerParams(dimension_semantics=sem, vmem_limit_bytes=VMEM_LIMIT)


def _norm_rows(x_ref, g_ref):
    x = x_ref[...]
    ms = jnp.mean(x * x, axis=-1, keepdims=True)
    return (x * lax.rsqrt(ms + RMS_EPS) * g_ref[...]).astype(BF16)


def _in_proj_a_kernel(x_ref, g_ref, w_ref, o_ref):
    o_ref[...] = jnp.dot(_norm_rows(x_ref, g_ref), w_ref[...], preferred_element_type=F32)


def _in_proj_a(x2d, g, w_bf16):
    n, d = x2d.shape
    tm = min(512, n)
    return pl.pallas_call(
        _in_proj_a_kernel,
        out_shape=jax.ShapeDtypeStruct((n, IN_PAD_A), F32),
        grid=(n // tm,),
        in_specs=[pl.BlockSpec((tm, d), lambda i: (i, 0)),
                  pl.BlockSpec((1, d), lambda i: (0, 0)),
                  pl.BlockSpec((d, IN_PAD_A), lambda i: (0, 0))],
        out_specs=pl.BlockSpec((tm, IN_PAD_A), lambda i: (i, 0)),
        compiler_params=_params("parallel"),
        name="in_proj_a",
    )(x2d, g.reshape(1, d), w_bf16)


def _in_proj_b_kernel(x_ref, g_ref, w_ref, qx_ref, kv_ref, kb_ref, vb_ref, nz_ref, gl_ref, gt_ref, su_ref, sz_ref):
    xn = _norm_rows(x_ref, g_ref)

    def seg(name):
        o, w = _B_OFF[name]
        return jnp.dot(xn, w_ref[:, o:o + w], preferred_element_type=F32)
    qx_ref[...] = seg("qx").astype(BF16)
    kv = seg("kv")
    kv_ref[...] = kv
    kvb = kv.astype(BF16)
    ones = jnp.ones((kv.shape[0], LANES), BF16)
    for br in range(3):
        kb_ref[:, br * LANES:(br + 1) * LANES] = kvb[:, 2 * br * LANES:(2 * br + 1) * LANES]
        vb_ref[:, 2 * br * LANES:(2 * br + 1) * LANES] = kvb[:, (2 * br + 1) * LANES:(2 * br + 2) * LANES]
        vb_ref[:, (2 * br + 1) * LANES:(2 * br + 2) * LANES] = ones
    nz_ref[...] = seg("nz")
    gl_ref[...] = seg("gl")
    gt_ref[...] = seg("gt")
    su_ref[...] = seg("su")
    sz_ref[...] = seg("sz")


def _in_proj_b(x2d, b, l, g, w_bf16):
    n, d = x2d.shape
    tm = min(512, l)
    nt = l // tm
    row = lambda bi, ti: (bi * nt + ti, 0)
    tcol = lambda bi, ti: (ti, bi)
    shapes = [((n, QX_W), BF16), ((n, KV_WIDTH), F32), ((n, 3 * LANES), BF16), ((n, 6 * LANES), BF16),
              ((n, NSA_WIDTH), F32), ((n, GL_W), F32), ((n, GT_W), F32),
              ((l, b * S5_WIDTH), F32), ((l, b * S5_WIDTH), F32)]
    out_specs = [pl.BlockSpec((tm, s[0][1]), row) for s in shapes[:7]] + \
                [pl.BlockSpec((tm, S5_WIDTH), tcol)] * 2
    return pl.pallas_call(
        _in_proj_b_kernel,
        out_shape=[jax.ShapeDtypeStruct(*s) for s in shapes],
        grid=(b, nt),
        in_specs=[pl.BlockSpec((tm, d), row),
                  pl.BlockSpec((1, d), lambda bi, ti: (0, 0)),
                  pl.BlockSpec((d, IN_PAD_B), lambda bi, ti: (0, 0))],
        out_specs=out_specs,
        compiler_params=_params("parallel", "parallel"),
        name="in_proj_b",
    )(x2d, g.reshape(1, d), w_bf16)


def _compress_kernel(x_ref, pe_ref, w1_ref, w2_ref, o_ref, *, n_ch):
    def half(s):
        acc = jnp.zeros((n_ch, LANES), F32)
        for p in range(CMP_STRIDE):
            xp = x_ref[pl.ds(p, n_ch, stride=CMP_STRIDE), :] + pe_ref[0, s * CMP_STRIDE + p:s * CMP_STRIDE + p + 1, :]
            acc = acc + jnp.dot(xp.astype(BF16), w1_ref[0, s * CMP_STRIDE + p], preferred_element_type=F32)
        return acc
    first = half(0)
    second = pltpu.roll(half(1), n_ch - 1, 0)
    hid = jax.nn.gelu(first + second).astype(BF16)
    o_ref[0, 0] = jnp.dot(hid, w2_ref[0], preferred_element_type=F32).astype(BF16)


def _compress(kv, b, t, pe, w1, w2):
    n_ch = t // CMP_STRIDE
    eye = jnp.eye(NSA_KV_HEADS, dtype=F32)
    pe2 = jnp.tile(pe, (1, 1, NSA_KV_HEADS))
    w1r = w1.reshape(2, CMP_BLOCK, HEAD_DIM, HEAD_DIM)
    w1bd = jnp.einsum('gh,kpde->kpgdhe', eye, w1r).reshape(2, CMP_BLOCK, LANES, LANES).astype(BF16)
    w2bd = jnp.einsum('gh,kde->kgdhe', eye, w2).reshape(2, LANES, LANES).astype(BF16)
    return pl.pallas_call(
        functools.partial(_compress_kernel, n_ch=n_ch),
        out_shape=jax.ShapeDtypeStruct((2, b, n_ch, LANES), BF16),
        grid=(b, 2),
        in_specs=[pl.BlockSpec((t, LANES), lambda bi, s: (bi, s)),
                  pl.BlockSpec((1, CMP_BLOCK, LANES), lambda bi, s: (s, 0, 0)),
                  pl.BlockSpec((1, CMP_BLOCK, LANES, LANES), lambda bi, s: (s, 0, 0, 0)),
                  pl.BlockSpec((1, LANES, LANES), lambda bi, s: (s, 0, 0))],
        out_specs=pl.BlockSpec((1, 1, n_ch, LANES), lambda bi, s: (s, bi, 0, 0)),
        compiler_params=_params("parallel", "parallel"),
        name="nsa_compress",
    )(kv, pe2, w1bd, w2bd)


def _split3(x):
    hi = x.astype(BF16)
    r = x - hi.astype(F32)
    mid = r.astype(BF16)
    lo = (r - mid.astype(F32)).astype(BF16)
    return hi, mid, lo


SEL_ROWS = 64


def _cmp_sel_kernel(qx_ref, kc_ref, vc_ref, gt_ref, m2_ref, o_ref, sel_ref, sc_s, *, tq, nc):
    q0 = pl.program_id(1) * tq
    qpos = q0 + lax.broadcasted_iota(jnp.int32, (tq, 1), 0)
    cend = lax.broadcasted_iota(jnp.int32, (1, nc), 1) * CMP_STRIDE + (CMP_BLOCK - 1)
    dist = (qpos - cend).astype(F32)
    ok = dist >= 0.0
    kc = kc_ref[0, 0]
    vc = vc_ref[0, 0]
    gts = jax.nn.sigmoid(gt_ref[...])
    imps = []
    for g in range(NSA_KV_HEADS):
        imp = jnp.zeros((tq, nc), F32)
        for j in range(NSA_HPG):
            h = g * NSA_HPG + j
            s = lax.dot_general(qx_ref[:, h * LANES:(h + 1) * LANES], kc, (((1,), (1,)), ((), ())),
                                preferred_element_type=F32)
            s = jnp.where(ok, s - ALIBI[h] * dist, MASKED)
            e = jnp.where(ok, jnp.exp(s - jnp.max(s, axis=-1, keepdims=True)), 0.0)
            p = e / jnp.maximum(jnp.sum(e, axis=-1, keepdims=True), 1e-30)
            imp = imp + p
            o = jnp.dot(p.astype(BF16), vc, preferred_element_type=F32)
            o_ref[:, h * HEAD_DIM:(h + 1) * HEAD_DIM] = (o[:, g * HEAD_DIM:(g + 1) * HEAD_DIM]
                                                         * gts[:, 3 * h:3 * h + 1])
        imps.append(imp)
    m2 = m2_ref[...]
    score = sum(jnp.dot(piece, m2, preferred_element_type=F32) for piece in _split3(jnp.concatenate(imps, axis=1)))
    lane = lax.broadcasted_iota(jnp.int32, (1, LANES), 1)
    blk = lane & (SLC_BLOCK - 1)
    cur = qpos >> 6
    forced = (blk == 0) | (blk == cur) | (blk == cur - 1)
    elig = (blk * SLC_BLOCK) <= qpos
    sc_s[...] = jnp.where(elig, jnp.where(forced, NEG_BIG, score), -NEG_BIG)

    def rows(r, carry):
        r0 = pl.multiple_of(r * SEL_ROWS, SEL_ROWS)
        sc = sc_s[pl.ds(r0, SEL_ROWS), :]
        cnt = jnp.zeros((SEL_ROWS, LANES), jnp.int32)
        for i in range(SLC_BLOCK):
            col = jnp.where(lane < SLC_BLOCK, sc[:, i:i + 1], sc[:, SLC_BLOCK + i:SLC_BLOCK + i + 1])
            tie = jnp.where(blk > i, 1, 0)
            cnt = cnt + jnp.where(col > sc, 1, jnp.where(col == sc, tie, 0))
        keep = (cnt < N_SELECT) & (sc > -0.5 * NEG_BIG)
        sel_ref[pl.ds(r0, SEL_ROWS), :] = jnp.where(keep, 1.0, 0.0).astype(BF16)
        return carry
    lax.fori_loop(0, tq // SEL_ROWS, rows, 0)


def _slc_map_matrix(nc, n_cmp):
    n = jnp.arange(nc)[:, None]
    j = jnp.arange(SLC_BLOCK)[None, :]
    ratio = SLC_BLOCK // CMP_STRIDE
    m = ((n >= ratio * j) & (n <= ratio * j + ratio - 1)).astype(F32) + \
        ((n >= ratio * j - 1) & (n <= ratio * j + ratio - 2)).astype(F32)
    m = jnp.where(n < n_cmp, m, 0.0)
    z = jnp.zeros_like(m)
    return jnp.concatenate([jnp.concatenate([m, z], 1), jnp.concatenate([z, m], 1)], 0).astype(BF16)


def _cmp_select(qx, kvc, gt, b, l):
    nc = kvc.shape[2]
    tq = min(256, l)
    nt = l // tq
    row = lambda bi, ti: (bi * nt + ti, 0)
    return pl.pallas_call(
        functools.partial(_cmp_sel_kernel, tq=tq, nc=nc),
        out_shape=[jax.ShapeDtypeStruct((b * l, NSA_WIDTH), F32), jax.ShapeDtypeStruct((b * l, LANES), BF16)],
        grid=(b, nt),
        in_specs=[pl.BlockSpec((tq, QX_W), row),
                  pl.BlockSpec((1, 1, nc, LANES), lambda bi, ti: (0, bi, 0, 0)),
                  pl.BlockSpec((1, 1, nc, LANES), lambda bi, ti: (1, bi, 0, 0)),
                  pl.BlockSpec((tq, GT_W), row),
                  pl.BlockSpec((2 * nc, LANES), lambda bi, ti: (0, 0))],
        out_specs=[pl.BlockSpec((tq, NSA_WIDTH), row), pl.BlockSpec((tq, LANES), row)],
        scratch_shapes=[pltpu.VMEM((tq, LANES), F32)],
        compiler_params=_params("parallel", "parallel"),
        name="nsa_cmp_select",
    )(qx, kvc, kvc, gt, _slc_map_matrix(nc, nc - 1))


def _flash_kernel(qx_ref, k_ref, v_ref, sel_ref, e_ref, gt_ref, o_ref, qs_s, ad_s, sl_s, acc_s, m_s,
                  *, tq, tk, mode, gate_col):
    i = pl.program_id(1)
    q0 = i * tq
    rows = NSA_HPG * tq
    d0 = (lax.broadcasted_iota(jnp.int32, (tq, tk), 0) - lax.broadcasted_iota(jnp.int32, (tq, tk), 1)).astype(F32)
    gts = jax.nn.sigmoid(gt_ref[...])

    def tile(g, k0, causal, winlow):
        kt = k_ref[pl.ds(k0, tk), :]
        vt = v_ref[pl.ds(k0, tk), :]
        c = (q0 - k0).astype(F32)
        s = lax.dot_general(qs_s[...], kt, (((1,), (1,)), ((), ())), preferred_element_type=F32) - ad_s[...]
        bias = None
        if mode == "slc":
            bias = (jnp.dot(sel_ref[...], e_ref[g, :, pl.ds(k0, tk)], preferred_element_type=F32) - 1.0) * (-MASKED)
        if causal:
            cb = jnp.where(d0 + c >= 0.0, 0.0, MASKED)
            bias = cb if bias is None else bias + cb
        if winlow:
            wb = jnp.where(d0 + c < float(WINDOW), 0.0, MASKED)
            bias = wb if bias is None else bias + wb
        if bias is not None:
            s = (s.reshape(NSA_HPG, tq, tk) + bias[None]).reshape(rows, tk)
        shift = sl_s[...] * c
        m_prev = m_s[...]
        m_new = jnp.maximum(m_prev, jnp.max(s, axis=1, keepdims=True) - shift)
        p = jnp.exp(s - pltpu.repeat(m_new + shift, tk // LANES, axis=1))
        alpha = jnp.exp(m_prev - m_new)
        acc_s[...] = pltpu.repeat(alpha, 2, axis=1) * acc_s[...] + jnp.dot(p.astype(BF16), vt,
                                                                          preferred_element_type=F32)
        m_s[...] = m_new

    for g in range(NSA_KV_HEADS):
        for jh in range(NSA_HPG):
            h = g * NSA_HPG + jh
            qs_s[jh * tq:(jh + 1) * tq, :] = qx_ref[:, h * LANES:(h + 1) * LANES]
            ad_s[jh * tq:(jh + 1) * tq, :] = ALIBI[h] * d0
            sl_s[jh * tq:(jh + 1) * tq, :] = jnp.full((tq, LANES), ALIBI[h], F32)
        m_s[...] = jnp.full(m_s.shape, MASKED, F32)
        acc_s[...] = jnp.zeros(acc_s.shape, F32)
        if mode == "slc":
            last = (q0 + tq - 1) // tk

            def body(j, carry):
                tile(g, pl.multiple_of(j * tk, tk), False, False)
                return carry
            lax.fori_loop(0, last, body, 0)
            tile(g, pl.multiple_of(last * tk, tk), True, False)
        else:
            span = WINDOW // tk
            for back in range(span, 0, -1):
                @pl.when(i >= back)
                def _():
                    tile(g, pl.multiple_of((i - back) * tk, tk), False, back == span)
            tile(g, pl.multiple_of(i * tk, tk), True, False)
        for jh in range(NSA_HPG):
            h = g * NSA_HPG + jh
            c = 3 * h + gate_col
            a = acc_s[jh * tq:(jh + 1) * tq, :]
            o = a[:, g * HEAD_DIM:(g + 1) * HEAD_DIM] / a[:, LANES + g * HEAD_DIM:LANES + (g + 1) * HEAD_DIM]
            o_ref[:, h * HEAD_DIM:(h + 1) * HEAD_DIM] = o * gts[:, c:c + 1]


def _block_expand_matrix(l):
    blk = jnp.arange(SLC_BLOCK)[:, None] == (jnp.arange(l)[None, :] // SLC_BLOCK)
    z = jnp.zeros_like(blk)
    return jnp.stack([jnp.concatenate([blk, z], 0), jnp.concatenate([z, blk], 0)]).astype(BF16)


def _flash(qx, kb, vb, sel, gt, b, l, mode):
    tq = min(256, l)
    tk = 2 * tq if mode == "slc" else tq
    nt = l // tq
    rows = NSA_HPG * tq
    row = lambda bi, ti: (bi * nt + ti, 0)
    branch = 1 if mode == "slc" else 2
    emat = _block_expand_matrix(l)
    return pl.pallas_call(
        functools.partial(_flash_kernel, tq=tq, tk=tk, mode=mode, gate_col=branch),
        out_shape=jax.ShapeDtypeStruct((b * l, NSA_WIDTH), F32),
        grid=(b, nt),
        in_specs=[pl.BlockSpec((tq, QX_W), row),
                  pl.BlockSpec((l, LANES), lambda bi, ti: (bi, branch)),
                  pl.BlockSpec((l, 2 * LANES), lambda bi, ti: (bi, branch)),
                  pl.BlockSpec((tq, LANES), row),
                  pl.BlockSpec((NSA_KV_HEADS, LANES, l), lambda bi, ti: (0, 0, 0)),
                  pl.BlockSpec((tq, GT_W), row)],
        out_specs=pl.BlockSpec((tq, NSA_WIDTH), row),
        scratch_shapes=[pltpu.VMEM((rows, LANES), BF16), pltpu.VMEM((rows, tk), F32), pltpu.VMEM((rows, LANES), F32),
                        pltpu.VMEM((rows, 2 * LANES), F32), pltpu.VMEM((rows, LANES), F32)],
        compiler_params=_params("parallel", "parallel"),
        name="nsa_" + mode,
    )(qx, kb, vb, sel, emat, gt)


def _s5_kernel(u_ref, z_ref, bre_ref, bim_ref, cre_ref, cim_ref, ab_ref, d_ref, wglu_ref, h0_ref,
               o_ref, hout_ref, xr_s, xi_s, h_s, *, nb, tc):
    @pl.when(pl.program_id(0) == 0)
    def _():
        h_s[...] = h0_ref[...]
    bre, bim = bre_ref[...], bim_ref[...]
    for b in range(nb):
        ub = u_ref[:, b * S5_WIDTH:(b + 1) * S5_WIDTH].astype(BF16)
        xr_s[:, b * S5_CH:(b + 1) * S5_CH] = jnp.dot(ub, bre, preferred_element_type=F32)
        xi_s[:, b * S5_CH:(b + 1) * S5_CH] = jnp.dot(ub, bim, preferred_element_type=F32)
    abr = ab_ref[0:1, :]
    abi = ab_ref[1:2, :]

    def step(t, carry):
        hr, hi = carry
        nr = abr * hr - abi * hi + xr_s[pl.ds(t, 1), :]
        ni = abr * hi + abi * hr + xi_s[pl.ds(t, 1), :]
        xr_s[pl.ds(t, 1), :] = nr
        xi_s[pl.ds(t, 1), :] = ni
        return nr, ni
    hr, hi = lax.fori_loop(0, tc, step, (h_s[0:1, :], h_s[1:2, :]), unroll=8)
    h_s[0:1, :] = hr
    h_s[1:2, :] = hi
    hout_ref[...] = h_s[...]
    cre, cim, wglu = cre_ref[...], cim_ref[...], wglu_ref[...]
    for b in range(nb):
        u = u_ref[:, b * S5_WIDTH:(b + 1) * S5_WIDTH]
        y = (jnp.dot(xr_s[:, b * S5_CH:(b + 1) * S5_CH].astype(BF16), cre, preferred_element_type=F32)
             - jnp.dot(xi_s[:, b * S5_CH:(b + 1) * S5_CH].astype(BF16), cim, preferred_element_type=F32)
             + d_ref[...] * u)
        gl = jnp.dot(jax.nn.gelu(y).astype(BF16), wglu, preferred_element_type=F32)
        z = z_ref[:, b * S5_WIDTH:(b + 1) * S5_WIDTH]
        o_ref[:, b * S5_WIDTH:(b + 1) * S5_WIDTH] = (gl[:, :S5_WIDTH] * jax.nn.sigmoid(gl[:, S5_WIDTH:])
                                                       * (z * jax.nn.sigmoid(z)))


def _s5_discretise(lw):
    a_re, a_im = lw['s5_a_re'].astype(F32), lw['s5_a_im'].astype(F32)
    b_re, b_im = lw['s5_b_re'].astype(F32), lw['s5_b_im'].astype(F32)
    dt = jnp.exp(lw['s5_log_dt'].astype(F32))[:, None]
    mag = jnp.exp(a_re * dt)
    abr, abi = mag * jnp.cos(a_im * dt), mag * jnp.sin(a_im * dt)
    den = a_re * a_re + a_im * a_im
    nr = abr - 1.0
    cr = (nr * a_re + abi * a_im) / den
    ci = (abi * a_re - nr * a_im) / den
    bbr = cr[..., None] * b_re - ci[..., None] * b_im
    bbi = cr[..., None] * b_im + ci[..., None] * b_re
    eye = jnp.eye(S5_GROUPS, dtype=F32)
    bd_in = lambda m: jnp.einsum('gh,gpc->gchp', eye, m).reshape(S5_WIDTH, S5_CH).astype(BF16)
    bd_out = lambda m: jnp.einsum('gh,gcp->gphc', eye, m.astype(F32)).reshape(S5_CH, S5_WIDTH).astype(BF16)
    ab = jnp.stack([abr.reshape(S5_CH), abi.reshape(S5_CH)])
    return bd_in(bbr), bd_in(bbi), bd_out(lw['s5_c_re']), bd_out(lw['s5_c_im']), ab


def _s5(su, sz, lw, h0, nb, l):
    bre, bim, cre, cim, ab = _s5_discretise(lw)
    tc = min(256, l)
    const = lambda t: (0, 0)
    return pl.pallas_call(
        functools.partial(_s5_kernel, nb=nb, tc=tc),
        out_shape=[jax.ShapeDtypeStruct((l, nb * S5_WIDTH), F32), jax.ShapeDtypeStruct((2, nb * S5_CH), F32)],
        grid=(l // tc,),
        in_specs=[pl.BlockSpec((tc, nb * S5_WIDTH), lambda t: (t, 0)),
                  pl.BlockSpec((tc, nb * S5_WIDTH), lambda t: (t, 0)),
                  pl.BlockSpec((S5_WIDTH, S5_CH), const), pl.BlockSpec((S5_WIDTH, S5_CH), const),
                  pl.BlockSpec((S5_CH, S5_WIDTH), const), pl.BlockSpec((S5_CH, S5_WIDTH), const),
                  pl.BlockSpec((2, nb * S5_CH), const), pl.BlockSpec((1, S5_WIDTH), const),
                  pl.BlockSpec((S5_WIDTH, 2 * S5_WIDTH), const), pl.BlockSpec((2, nb * S5_CH), const)],
        out_specs=[pl.BlockSpec((tc, nb * S5_WIDTH), lambda t: (t, 0)), pl.BlockSpec((2, nb * S5_CH), const)],
        scratch_shapes=[pltpu.VMEM((tc, nb * S5_CH), F32), pltpu.VMEM((tc, nb * S5_CH), F32),
                        pltpu.VMEM((2, nb * S5_CH), F32)],
        compiler_params=_params("arbitrary"),
        name="s5",
    )(su, sz, bre, bim, cre, cim, jnp.tile(ab, (1, nb)), lw['s5_d'].astype(F32).reshape(1, S5_WIDTH),
      lw['s5_w_glu'].astype(BF16), h0)


GLA_QK = GLA_HEADS * GLA_DK


def _gla_kernel(gl_ref, gt_ref, wg_ref, bg_ref, gn_ref, pall_ref, lm_ref, seg_ref, s0_ref, o_ref, sout_ref, st_s,
                *, r, t, nl):
    nseq = r // t

    @pl.when(pl.program_id(1) == 0)
    def _():
        st_s[...] = s0_ref[...]
    q = gl_ref[:, 0:GLA_QK] * (GLA_DK ** -0.5)
    k = gl_ref[:, GLA_QK:2 * GLA_QK]
    v = gl_ref[:, 2 * GLA_QK:2 * GLA_QK + GLA_WIDTH].astype(BF16)
    gz = gl_ref[:, 2 * GLA_QK + GLA_WIDTH:]
    lg = jnp.dot(gt_ref[...].astype(BF16), wg_ref[...], preferred_element_type=F32) + bg_ref[...]
    la = (jnp.minimum(lg, 0.0) - jnp.log(1.0 + jnp.exp(-jnp.abs(lg)))) * (1.0 / GLA_GATE_TAU)
    pall = pall_ref[...]
    cums = sum(jnp.dot(pall, piece, preferred_element_type=F32) for piece in _split3(la))
    cum = cums[0:r]
    cl = cums[(nl + 1) * r:(nl + 2) * r]
    lane_qk = lax.broadcasted_iota(jnp.int32, (1, GLA_QK), 1) // GLA_DK
    lane_v = lax.broadcasted_iota(jnp.int32, (1, GLA_WIDTH), 1) // GLA_DV

    def stack(x):
        return jnp.concatenate([jnp.where(lane_qk == h, x, 0.0) for h in range(GLA_HEADS)], axis=0).astype(BF16)

    def level(l, qf, kf):
        s = lax.dot_general(stack(qf), kf.astype(BF16), (((1,), (1,)), ((), ())), preferred_element_type=F32)
        return (s.reshape(GLA_HEADS, r, r) * lm_ref[l][None]).reshape(GLA_HEADS * r, r)
    att = level(0, q, k)
    for l in range(1, nl + 1):
        cm = cums[l * r:(l + 1) * r]
        att = att + level(l, q * jnp.exp(jnp.minimum(cum - cm, 0.0)), k * jnp.exp(jnp.minimum(cm - cum, 0.0)))
    res = jnp.dot(att.astype(BF16), v, preferred_element_type=F32)
    o = sum(jnp.where(lane_v == h, res[h * r:(h + 1) * r], 0.0) for h in range(GLA_HEADS))
    qd = (q * jnp.exp(cum)).astype(BF16)
    kd = k * jnp.exp(cl - cum)
    bd = (lax.broadcasted_iota(jnp.int32, (GLA_QK, GLA_WIDTH), 0) // GLA_DK
          == lax.broadcasted_iota(jnp.int32, (GLA_QK, GLA_WIDTH), 1) // GLA_DV)
    row_seq = lax.broadcasted_iota(jnp.int32, (r, 1), 0) // t
    inter = []
    for i in range(nseq):
        st = st_s[i]
        inter.append(jnp.dot(qd[i * t:(i + 1) * t], st.astype(BF16), preferred_element_type=F32))
        kdi = kd if nseq == 1 else jnp.where(row_seq == i, kd, 0.0)
        upd = jnp.dot(jnp.transpose(kdi).astype(BF16), v, preferred_element_type=F32)
        a_col = jnp.transpose(jnp.broadcast_to(jnp.exp(cl[i * t:i * t + 1, :]), (GLA_QK, GLA_QK)))
        st_s[i] = jnp.where(bd, pltpu.repeat(a_col, GLA_WIDTH // GLA_QK, axis=1) * st + upd, 0.0)
    o = o + (inter[0] if nseq == 1 else jnp.concatenate(inter, axis=0))
    sout_ref[...] = st_s[...]
    hi = (o * o).astype(BF16)
    lo = (o * o - hi.astype(F32)).astype(BF16)
    ms = (jnp.dot(hi, seg_ref[...], preferred_element_type=F32) + jnp.dot(lo, seg_ref[...], preferred_element_type=F32))
    o_ref[...] = o * lax.rsqrt(ms + RMS_EPS) * gn_ref[...] * (gz * jax.nn.sigmoid(gz))


def _gla_tables(r, t, nl):
    idx = jnp.arange(r)
    seq = idx // t
    rows = [(seq[:, None] == seq[None, :]) & (idx[None, :] <= idx[:, None])]
    masks = [idx[:, None] == idx[None, :]]
    for l in range(1, nl + 1):
        mid = (idx >> l << l) + (1 << (l - 1)) - 1
        rows.append((seq[:, None] == seq[None, :]) & (idx[None, :] <= mid[:, None]))
        masks.append(((idx[:, None] >> l) == (idx[None, :] >> l)) & (((idx[:, None] >> (l - 1)) & 1) == 1)
                     & (((idx[None, :] >> (l - 1)) & 1) == 0))
    rows.append(seq[:, None] == seq[None, :])
    return jnp.concatenate(rows, 0).astype(BF16), jnp.stack(masks).astype(F32)


def _gla(gl, gt, lw, s0, nb, nchunk, r, t):
    n = gl.shape[0]
    nseq = r // t
    nl = t.bit_length() - 1
    pall, lm = _gla_tables(r, t, nl)
    wg = jnp.zeros((GT_W, GLA_QK), F32).at[3 * NSA_HEADS:3 * NSA_HEADS + GLA_GATE_RANK].set(lw['gla_wg']).astype(BF16)
    eye = jnp.eye(GLA_HEADS, dtype=F32)
    seg = (jnp.einsum('gh,de->gdhe', eye, jnp.ones((GLA_DV, GLA_DV), F32)) / GLA_DV).reshape(GLA_WIDTH, GLA_WIDTH)
    s0bd = jnp.einsum('gh,bgde->bgdhe', eye, s0.astype(F32)).reshape(-1, GLA_QK, GLA_WIDTH)
    row = lambda bi, ci: (bi * nchunk + ci, 0)
    const2 = lambda bi, ci: (0, 0)
    o, sfin = pl.pallas_call(
        functools.partial(_gla_kernel, r=r, t=t, nl=nl),
        out_shape=[jax.ShapeDtypeStruct((n, GLA_WIDTH), F32),
                   jax.ShapeDtypeStruct((nb * nseq, GLA_QK, GLA_WIDTH), F32)],
        grid=(nb, nchunk),
        in_specs=[pl.BlockSpec((r, GL_W), row), pl.BlockSpec((r, GT_W), row),
                  pl.BlockSpec((GT_W, GLA_QK), const2), pl.BlockSpec((1, GLA_QK), const2),
                  pl.BlockSpec((1, GLA_WIDTH), const2), pl.BlockSpec(((nl + 2) * r, r), const2),
                  pl.BlockSpec((nl + 1, r, r), lambda bi, ci: (0, 0, 0)),
                  pl.BlockSpec((GLA_WIDTH, GLA_WIDTH), const2),
                  pl.BlockSpec((nseq, GLA_QK, GLA_WIDTH), lambda bi, ci: (bi, 0, 0))],
        out_specs=[pl.BlockSpec((r, GLA_WIDTH), row),
                   pl.BlockSpec((nseq, GLA_QK, GLA_WIDTH), lambda bi, ci: (bi, 0, 0))],
        scratch_shapes=[pltpu.VMEM((nseq, GLA_QK, GLA_WIDTH), F32)],
        compiler_params=_params("parallel", "arbitrary"),
        name="gla",
    )(gl, gt, wg, lw['gla_bg'].astype(F32).reshape(1, GLA_QK),
      jnp.tile(lw['gla_norm'].astype(F32), GLA_HEADS).reshape(1, GLA_WIDTH), pall, lm, seg.astype(BF16), s0bd)
    sf = sfin.reshape(-1, GLA_HEADS, GLA_DK, GLA_HEADS, GLA_DV)
    state = jnp.stack([sf[:, h, :, h, :] for h in range(GLA_HEADS)], axis=1)
    return o, state


def _merge_kernel(h_ref, oc_ref, os_ref, ow_ref, nz_ref, og_ref, o5_ref, ple_ref, wo_ref, pn_ref, wgate_ref, wproj_ref,
                  fn_ref, out_ref, *, final):
    nz = nz_ref[...]
    o_nsa = (oc_ref[...] + os_ref[...] + ow_ref[...]) * (nz * jax.nn.sigmoid(nz))
    mix = jnp.concatenate([o_nsa, og_ref[...], o5_ref[...]], axis=1).astype(BF16)
    h = h_ref[...] + jnp.dot(mix, wo_ref[...], preferred_element_type=F32)

    def norm(x, g_ref):
        return x * lax.rsqrt(jnp.mean(x * x, axis=-1, keepdims=True) + RMS_EPS) * g_ref[...]
    gate = jax.nn.sigmoid(jnp.dot(norm(h, pn_ref).astype(BF16), wgate_ref[...], preferred_element_type=F32))
    h = h + gate * jnp.dot(ple_ref[...].astype(BF16), wproj_ref[...], preferred_element_type=F32)
    out_ref[...] = norm(h, fn_ref) if final else h


def _merge_call(h2d, o_cmp, o_slc, o_win, nz, o_gla, o_s5, s5_time_major, ple2d, lw, final_norm, nb):
    n, d = h2d.shape
    l = n // nb
    tm = min(512, l)
    nt = l // tm
    row = lambda bi, ti: (bi * nt + ti, 0)
    const = lambda bi, ti: (0, 0)
    o5_spec = pl.BlockSpec((tm, S5_WIDTH), (lambda bi, ti: (ti, bi)) if s5_time_major else row)
    fn = jnp.ones((d,), F32) if final_norm is None else final_norm
    return pl.pallas_call(
        functools.partial(_merge_kernel, final=final_norm is not None),
        out_shape=jax.ShapeDtypeStruct((n, d), F32),
        grid=(nb, nt),
        in_specs=[pl.BlockSpec((tm, d), row)] + [pl.BlockSpec((tm, NSA_WIDTH), row)] * 4 +
                 [pl.BlockSpec((tm, GLA_WIDTH), row), o5_spec, pl.BlockSpec((tm, PLE_DIM), row),
                  pl.BlockSpec((MIX_WIDTH, d), const), pl.BlockSpec((1, d), const), pl.BlockSpec((d, d), const),
                  pl.BlockSpec((PLE_DIM, d), const), pl.BlockSpec((1, d), const)],
        out_specs=pl.BlockSpec((tm, d), row),
        compiler_params=_params("parallel", "parallel"),
        name="merge",
    )(h2d, o_cmp, o_slc, o_win, nz, o_gla, o_s5, ple2d, lw['w_out'].astype(BF16),
      lw['ple_norm'].astype(F32).reshape(1, d), lw['ple_w_gate'].astype(BF16), lw['ple_w_proj'].astype(BF16),
      fn.astype(F32).reshape(1, d))


def _rmsnorm(x, g):
    xf = x.astype(jnp.float32)
    y = xf * lax.rsqrt(jnp.mean(xf * xf, axis=-1, keepdims=True) + RMS_EPS)
    return (y * g.astype(jnp.float32)).astype(x.dtype)


def _masked_softmax(s, mask):
    s = jnp.where(mask, s, -1e30)
    e = jnp.where(mask, jnp.exp(s - jnp.max(s, axis=-1, keepdims=True)), 0.0)
    return e / jnp.maximum(jnp.sum(e, axis=-1, keepdims=True), 1e-30)


def _alibi_slopes():
    return jnp.exp2(-8.0 * jnp.arange(1, NSA_HEADS + 1, dtype=jnp.float32) / NSA_HEADS)


def _compress_kv(x, pe, w1, w2):
    b, t, g, d = x.shape
    ratio = CMP_BLOCK // CMP_STRIDE
    n_ch = t // CMP_STRIDE
    n_cmp = n_ch - ratio + 1
    ch = x[:, :n_ch * CMP_STRIDE].reshape(b, n_ch, CMP_STRIDE, g, d)
    blocks = jnp.concatenate([ch[:, s:s + n_cmp] for s in range(ratio)], axis=2)
    blocks = blocks + pe[None, None, :, None, :]
    flat = jnp.moveaxis(blocks, 2, 3).reshape(b, n_cmp, g, CMP_BLOCK * d)
    return jax.nn.gelu(flat @ w1) @ w2


def _nsa_sparse_block(q, qpos, kc, vc, cend, ksb, vsb, slopes):
    f32 = jnp.float32
    b, qb = q.shape[:2]
    sl = slopes.reshape(NSA_KV_HEADS, NSA_HPG)
    scale = HEAD_DIM ** -0.5
    s = jnp.einsum('bqghd,bngd->bqghn', q, kc).astype(f32) * scale
    dist = (qpos[:, None] - cend[None, :]).astype(f32)
    s = s - sl[:, :, None] * dist[:, None, None, :]
    p = _masked_softmax(s, (cend[None, :] <= qpos[:, None])[:, None, None, :])
    o_cmp = jnp.einsum('bqghn,bngd->bqghd', p.astype(vc.dtype), vc)
    imp = p.sum(axis=3)
    n_cmp = imp.shape[-1]
    n_slc = ksb.shape[2]
    run = imp
    for sft in range(1, CMP_BLOCK // CMP_STRIDE):
        run = run + jnp.pad(imp, ((0, 0), (0, 0), (0, 0), (sft, 0)))[..., :n_cmp]
    ratio = SLC_BLOCK // CMP_STRIDE
    run = jnp.pad(run, ((0, 0), (0, 0), (0, 0), (0, ratio * n_slc - n_cmp)))
    score = run.reshape(b, qb, NSA_KV_HEADS, n_slc, ratio).sum(-1)
    blk = jnp.arange(n_slc)
    cur = qpos // SLC_BLOCK
    eligible = blk[None, :] * SLC_BLOCK <= qpos[:, None]
    forced = (blk[None, :] == 0) | (blk[None, :] == cur[:, None]) | (blk[None, :] == cur[:, None] - 1)
    score = jnp.where(forced[:, None, :], NEG_BIG, score)
    score = jnp.where(eligible[:, None, :], score, -NEG_BIG)
    top_val, top_idx = lax.top_k(score, min(N_SELECT, n_slc))
    sel_ok = top_val > -0.5 * NEG_BIG
    b_ix = jnp.arange(b)[:, None, None, None]
    g_ix = jnp.arange(NSA_KV_HEADS)[None, None, :, None]
    ksel = ksb[b_ix, g_ix, top_idx]
    vsel = vsb[b_ix, g_ix, top_idx]
    kpos = top_idx[..., None] * SLC_BLOCK + jnp.arange(SLC_BLOCK)
    s2 = jnp.einsum('bqghd,bqgkjd->bqghkj', q, ksel).astype(f32) * scale
    dist2 = (qpos[None, :, None, None, None] - kpos).astype(f32)
    s2 = s2 - sl[None, None, :, :, None, None] * dist2[:, :, :, None]
    ok2 = (sel_ok[..., None] & (dist2 >= 0))[:, :, :, None]
    shp = s2.shape
    nk = shp[4] * shp[5]
    p2 = _masked_softmax(s2.reshape(shp[:4] + (nk,)), ok2.reshape(ok2.shape[:4] + (nk,))).reshape(shp)
    o_slc = jnp.einsum('bqghkj,bqgkjd->bqghd', p2.astype(vsel.dtype), vsel)
    return o_cmp, o_slc


def _window_attn(q, qpos, k, v, kpos, slopes):
    sl = slopes.reshape(NSA_KV_HEADS, NSA_HPG)
    s = jnp.einsum('bqghd,bkgd->bqghk', q, k).astype(jnp.float32) * HEAD_DIM ** -0.5
    dist = (qpos[:, None] - kpos[None, :]).astype(jnp.float32)
    s = s - sl[:, :, None] * dist[:, None, None, :]
    ok = ((dist >= 0) & (dist < WINDOW) & (kpos[None, :] >= 0))[:, None, None, :]
    p = _masked_softmax(s, ok)
    return jnp.einsum('bqghk,bkgd->bqghd', p.astype(v.dtype), v)


def _gla_chunked(q, k, v, log_a, s0):
    f32 = jnp.float32
    b, l = q.shape[:2]
    c = min(GLA_CHUNK, l)
    pad = (-l) % c
    n = (l + pad) // c

    def prep(t):
        t = jnp.pad(t.astype(f32), ((0, 0), (0, pad), (0, 0), (0, 0)))
        return t.reshape(b, n, c, GLA_HEADS, t.shape[-1])
    q = prep(q) * GLA_DK ** -0.5
    k, v, log_a = prep(k), prep(v), prep(log_a)
    cum = jnp.cumsum(log_a, axis=2)
    causal = jnp.tril(jnp.ones((c, c), dtype=bool))[:, :, None, None]
    diff = cum[:, :, :, None] - cum[:, :, None, :]
    decay = jnp.where(causal, jnp.exp(jnp.where(causal, diff, 0.0)), 0.0)
    att = jnp.einsum('bnthd,bnshd,bntshd->bnhts', q, k, decay)
    o_intra = jnp.einsum('bnhts,bnshe->bnthe', att, v)
    q_dec = q * jnp.exp(cum)
    k_dec = k * jnp.exp(cum[:, :, -1:] - cum)
    a_tot = jnp.exp(cum[:, :, -1])

    def step(state, inp):
        qd, kd, vv, at = inp
        o = jnp.einsum('bchd,bhde->bche', qd, state)
        state = at[..., None] * state + jnp.einsum('bchd,bche->bhde', kd, vv)
        return state, o
    xs = (jnp.moveaxis(q_dec, 1, 0), jnp.moveaxis(k_dec, 1, 0), jnp.moveaxis(v, 1, 0), jnp.moveaxis(a_tot, 1, 0))
    s_fin, o_inter = lax.scan(step, s0.astype(f32), xs)
    o = o_intra + jnp.moveaxis(o_inter, 0, 1)
    return o.reshape(b, n * c, GLA_HEADS, GLA_DV)[:, :l], s_fin


def _s5_combine(e1, e2):
    a1r, a1i, b1r, b1i = e1
    a2r, a2i, b2r, b2i = e2
    return (a2r * a1r - a2i * a1i, a2r * a1i + a2i * a1r,
            a2r * b1r - a2i * b1i + b2r, a2r * b1i + a2i * b1r + b2i)


def _s5_scan(u, h0, a_re, a_im, b_re, b_im, c_re, c_im, d, log_dt):
    f32 = jnp.float32
    b, l, _ = u.shape
    dt = jnp.exp(log_dt.astype(f32))[:, None]
    mag = jnp.exp(a_re * dt)
    abr, abi = mag * jnp.cos(a_im * dt), mag * jnp.sin(a_im * dt)
    den = a_re * a_re + a_im * a_im
    nr = abr - 1.0
    cr = (nr * a_re + abi * a_im) / den
    ci = (abi * a_re - nr * a_im) / den
    bbr = cr[..., None] * b_re - ci[..., None] * b_im
    bbi = cr[..., None] * b_im + ci[..., None] * b_re
    ug = u.reshape(b, l, S5_GROUPS, S5_GROUP_CH)
    xr = jnp.einsum('gpc,blgc->blgp', bbr, ug)
    xi = jnp.einsum('gpc,blgc->blgp', bbi, ug)
    h0r, h0i = h0[..., 0].astype(f32), h0[..., 1].astype(f32)
    xr = xr.at[:, 0].add(abr * h0r - abi * h0i)
    xi = xi.at[:, 0].add(abr * h0i + abi * h0r)
    ar = jnp.broadcast_to(abr, xr.shape)
    ai = jnp.broadcast_to(abi, xr.shape)
    _, _, hr, hi = lax.associative_scan(_s5_combine, (ar, ai, xr, xi), axis=1)
    y = jnp.einsum('gcp,blgp->blgc', c_re, hr) - jnp.einsum('gcp,blgp->blgc', c_im, hi)
    y = y.reshape(b, l, S5_WIDTH) + d.astype(f32) * u
    return y, jnp.stack([hr[:, -1], hi[:, -1]], axis=-1)


GLA_CHUNK_ROWS = 256


def _split_kv(kv, b, l):
    kv5 = kv.reshape(b, l, 3, 2, NSA_KV_HEADS, HEAD_DIM)
    return kv5[:, :, 0], kv5[:, :, 1], kv5[:, :, 2]


def _prompt_layer(h2d, ple2d, lw, b, l, final_norm):
    qx, kv, kb, vb, nz, gl, gt, su, sz = _in_proj_b(h2d, b, l, lw['norm'], lw['w_in_b'])
    kvc = _compress(kv, b, l, lw['cmp_pe'], lw['cmp_w1'], lw['cmp_w2'])
    o_cmp, sel = _cmp_select(qx, kvc, gt, b, l)
    o_slc = _flash(qx, kb, vb, sel, gt, b, l, "slc")
    o_win = _flash(qx, kb, vb, sel, gt, b, l, "win")
    new_cmp, new_slc, new_win = _split_kv(kv, b, l)
    win_state = new_win[:, l - min(WINDOW, l):]

    t = min(GLA_CHUNK_ROWS, l)
    o_gla, gla_state = _gla(gl, gt, lw, jnp.zeros((b, GLA_HEADS, GLA_DK, GLA_DV), F32), b, l // t, t, t)

    o_s5, hfin = _s5(su, sz, lw, jnp.zeros((2, b * S5_CH), F32), b, l)
    s5_state = jnp.transpose(hfin.reshape(2, b, S5_GROUPS, S5_STATE), (1, 2, 3, 0))

    h2d = _merge_call(h2d, o_cmp, o_slc, o_win, nz, o_gla, o_s5, True, ple2d, lw, final_norm, b)
    return h2d, (new_cmp, new_slc, win_state, gla_state, s5_state)


def _sample_layer(h2d, ple2d, lw, past, b, l, final_norm):
    f32 = jnp.float32
    n = b * l
    G, J, E = NSA_KV_HEADS, NSA_HPG, HEAD_DIM
    slopes = _alibi_slopes()
    qx, kv, kb, vb, nz, gl, gt, su, sz = _in_proj_b(h2d, 1, n, lw['norm'], lw['w_in_b'])
    su, sz = [jnp.transpose(a.reshape(b, l, S5_WIDTH), (1, 0, 2)).reshape(l, b * S5_WIDTH) for a in (su, sz)]
    qh = qx.astype(f32).reshape(n, G, J, NSA_KV_HEADS, E) * (HEAD_DIM ** 0.5)
    q = jnp.stack([qh[:, g, :, g, :] for g in range(G)], axis=1).reshape(b, l, G, J, E)
    ngate = gt[:, :3 * NSA_HEADS]
    new_cmp, new_slc, new_win = _split_kv(kv, b, l)
    pt = past['page_table']
    pos0 = pt.shape[1] * past['cmp'].shape[1]
    t_all = pos0 + l
    assert t_all // CMP_STRIDE == pos0 // CMP_STRIDE and pos0 % CMP_STRIDE == 0

    def gather_past(pool):
        rows = pool[pt]
        return rows.reshape((b, pos0) + rows.shape[3:])
    full_slc = jnp.concatenate([gather_past(past['slc']), new_slc], axis=1)
    kvc = _compress(gather_past(past['cmp']).reshape(b * pos0, 2 * NSA_KV_HEADS * HEAD_DIM), b, pos0,
                    lw['cmp_pe'], lw['cmp_w1'], lw['cmp_w2'])
    n_cmp = t_all // CMP_STRIDE - CMP_BLOCK // CMP_STRIDE + 1
    kc = kvc[0, :, :n_cmp].astype(f32).reshape(b, n_cmp, G, E)
    vc = kvc[1, :, :n_cmp].astype(f32).reshape(b, n_cmp, G, E)
    cend = jnp.arange(kc.shape[1]) * CMP_STRIDE + (CMP_BLOCK - 1)
    n_slc = -(-t_all // SLC_BLOCK)
    slc = jnp.pad(full_slc, ((0, 0), (0, n_slc * SLC_BLOCK - t_all), (0, 0), (0, 0), (0, 0)))
    slc = jnp.transpose(slc.reshape(b, n_slc, SLC_BLOCK, 2, G, E), (3, 0, 4, 1, 2, 5))
    ksb, vsb = slc[0], slc[1]
    qpos = pos0 + jnp.arange(l)
    o_cmp, o_slc = _nsa_sparse_block(q, qpos, kc, vc, cend, ksb, vsb, slopes)
    buf = past['win']
    wall = jnp.concatenate([buf, new_win], axis=1)
    wpos = pos0 - buf.shape[1] + jnp.arange(wall.shape[1])
    o_win = _window_attn(q, qpos, wall[:, :, 0], wall[:, :, 1], wpos, slopes)
    win_state = wall[:, l:]
    gates = jax.nn.sigmoid(ngate.astype(f32)).reshape(b, l, G, J, 3)
    gated = [(gates[..., c:c + 1] * o.astype(f32)).reshape(n, NSA_WIDTH) for c, o in enumerate((o_cmp, o_slc, o_win))]

    seqs = GLA_SAMPLE_ROWS // l
    o_gla, gla_state = _gla(gl, gt, lw, past['gla'], b // seqs, 1, GLA_SAMPLE_ROWS, l)

    h0 = jnp.transpose(past['s5'].astype(f32), (3, 0, 1, 2)).reshape(2, b * S5_CH)
    o_s5, hfin = _s5(su, sz, lw, h0, b, l)
    s5_state = jnp.transpose(hfin.reshape(2, b, S5_GROUPS, S5_STATE), (1, 2, 3, 0))

    o_s5 = jnp.transpose(o_s5.reshape(l, b, S5_WIDTH), (1, 0, 2)).reshape(n, S5_WIDTH)
    h2d = _merge_call(h2d, gated[0], gated[1], gated[2], nz, o_gla, o_s5, False, ple2d, lw, final_norm, 1)
    return h2d, (new_cmp, new_slc, win_state, gla_state, s5_state)


GLA_SAMPLE_ROWS = 128


def kernel(x_prompt, x_sample, cache_cmp, cache_slc, cache_win, state_gla, state_s5, page_table,
           p_prompt, p_sample, norm_mix, w_in, w_out, cmp_pe, cmp_w1, cmp_w2, gla_wg, gla_bg, gla_norm,
           s5_a_re, s5_a_im, s5_b_re, s5_b_im, s5_c_re, s5_c_im, s5_d, s5_log_dt, s5_w_glu,
           ple_norm, ple_w_gate, ple_w_proj, final_norm):
    depth = w_in.shape[0]
    w_in_b = _relayout_w_in_b(w_in)
    bp, lp, d = x_prompt.shape
    bs, ls, _ = x_sample.shape
    hp, hs = x_prompt.reshape(bp * lp, d), x_sample.reshape(bs * ls, d)
    st_p, st_s = [], []
    for i in range(depth):
        fin = final_norm if i == depth - 1 else None
        lw = {'norm': norm_mix[i], 'w_in_b': w_in_b[i], 'w_out': w_out[i],
              'cmp_pe': cmp_pe[i], 'cmp_w1': cmp_w1[i], 'cmp_w2': cmp_w2[i],
              'gla_wg': gla_wg[i], 'gla_bg': gla_bg[i], 'gla_norm': gla_norm[i],
              's5_a_re': s5_a_re[i], 's5_a_im': s5_a_im[i], 's5_b_re': s5_b_re[i], 's5_b_im': s5_b_im[i],
              's5_c_re': s5_c_re[i], 's5_c_im': s5_c_im[i], 's5_d': s5_d[i], 's5_log_dt': s5_log_dt[i],
              's5_w_glu': s5_w_glu[i], 'ple_norm': ple_norm[i], 'ple_w_gate': ple_w_gate[i],
              'ple_w_proj': ple_w_proj[i]}
        hp, sp = _prompt_layer(hp, p_prompt[i].reshape(bp * lp, PLE_DIM), lw, bp, lp, fin)
        past = {'page_table': page_table, 'cmp': cache_cmp[i], 'slc': cache_slc[i], 'win': cache_win[i],
                'gla': state_gla[i], 's5': state_s5[i]}
        hs, ss = _sample_layer(hs, p_sample[i].reshape(bs * ls, PLE_DIM), lw, past, bs, ls, fin)
        st_p.append(sp)
        st_s.append(ss)
    outs = [hp.reshape(bp, lp, d), hs.reshape(bs, ls, d)]
    for j in range(5):
        outs.append(jnp.stack([s[j] for s in st_p]))
        outs.append(jnp.stack([s[j] for s in st_s]))
    return tuple(outs)
```

```python
import functools
import math

import jax
import jax.numpy as jnp
from jax import lax
from jax.experimental import pallas as pl
from jax.experimental.pallas import tpu as pltpu

D_MODEL = 1024
PLE_DIM = 256
HEAD_DIM = 64
NSA_HEADS = 8
NSA_KV_HEADS = 2
NSA_HPG = NSA_HEADS // NSA_KV_HEADS
NSA_WIDTH = NSA_HEADS * HEAD_DIM
CMP_BLOCK = 32
CMP_STRIDE = 16
SLC_BLOCK = 64
N_SELECT = 16
WINDOW = 512
SPARSE_Q_BLOCK = 64
WIN_Q_BLOCK = 128
GLA_HEADS = 4
GLA_DK = 32
GLA_DV = 64
GLA_WIDTH = GLA_HEADS * GLA_DV
GLA_GATE_RANK = 16
GLA_GATE_TAU = 16.0
GLA_CHUNK = 32
S5_GROUPS = 16
S5_GROUP_CH = 16
S5_STATE = 64
S5_WIDTH = S5_GROUPS * S5_GROUP_CH
S5_CH = S5_GROUPS * S5_STATE
MIX_WIDTH = NSA_WIDTH + GLA_WIDTH + S5_WIDTH
KV_WIDTH = 6 * NSA_KV_HEADS * HEAD_DIM
RMS_EPS = 1e-6
NEG_BIG = 1e9
MASKED = -1e30

LANES = 128
SUBLANES = 8
VMEM_LIMIT = 48 * 1024 * 1024

F32 = jnp.float32
BF16 = jnp.bfloat16
ALIBI = tuple(2.0 ** (-8.0 * (h + 1) / NSA_HEADS) for h in range(NSA_HEADS))

_SRC = {}
_off = 0
for _name, _w in (("nq", NSA_WIDTH), ("nkv", KV_WIDTH), ("ngate", 3 * NSA_HEADS), ("nz", NSA_WIDTH),
                  ("gq", GLA_HEADS * GLA_DK), ("gk", GLA_HEADS * GLA_DK), ("gv", GLA_WIDTH),
                  ("glr", GLA_GATE_RANK), ("gz", GLA_WIDTH), ("su", S5_WIDTH), ("sz", S5_WIDTH)):
    _SRC[_name] = (_off, _w)
    _off += _w
IN_WIDTH = _off


def _cols(w, name):
    o, n = _SRC[name]
    return w[..., o:o + n]


_ORDER_A = ("nq", "nkv", "nz", "gq", "gk", "gv", "gz", "su", "sz", "ngate", "glr")
_DST_A = {}
_off = 0
for _name in _ORDER_A:
    _DST_A[_name] = (_off, _SRC[_name][1])
    _off += _SRC[_name][1]
IN_PAD_A = -(-_off // LANES) * LANES


def _relayout_w_in_a(w_in):
    w = jnp.concatenate([_cols(w_in, n) for n in _ORDER_A], axis=-1)
    w = jnp.pad(w, ((0, 0),) * (w.ndim - 1) + ((0, IN_PAD_A - w.shape[-1]),))
    return w.astype(BF16)


def _seg_a(u, name):
    o, w = _DST_A[name]
    return u[..., o:o + w]


QX_W = NSA_HEADS * LANES
GL_W = 2 * GLA_HEADS * GLA_DK + 2 * GLA_WIDTH
GT_W = LANES
_B_SEGS = (("qx", QX_W), ("kv", KV_WIDTH), ("nz", NSA_WIDTH), ("gl", GL_W), ("gt", GT_W),
           ("su", S5_WIDTH), ("sz", S5_WIDTH))
_B_OFF = {}
_off = 0
for _name, _w in _B_SEGS:
    _B_OFF[_name] = (_off, _w)
    _off += _w
IN_PAD_B = _off


def _relayout_w_in_b(w_in):
    d = w_in.shape[:-1]
    wq = _cols(w_in, "nq") * (HEAD_DIM ** -0.5)
    z = jnp.zeros(d + (HEAD_DIM,), w_in.dtype)
    qparts = []
    for h in range(NSA_HEADS):
        wh = wq[..., h * HEAD_DIM:(h + 1) * HEAD_DIM]
        qparts += [wh, z] if h < NSA_HPG else [z, wh]
    gt = jnp.concatenate([_cols(w_in, "ngate"), _cols(w_in, "glr")], axis=-1)
    gt = jnp.pad(gt, ((0, 0),) * len(d) + ((0, GT_W - gt.shape[-1]),))
    w = jnp.concatenate(qparts + [_cols(w_in, "nkv"), _cols(w_in, "nz"), _cols(w_in, "gq"), _cols(w_in, "gk"),
                                  _cols(w_in, "gv"), _cols(w_in, "gz"), gt, _cols(w_in, "su"), _cols(w_in, "sz")],
                        axis=-1)
    return w.astype(BF16)


def _lane_tile(x, k):
    return jnp.concatenate([x] * k, axis=1)


def _params(*sem):
    return pltpu.CompilerParams(dimension_semantics=sem, vmem_limit_bytes=VMEM_LIMIT)


def _norm_rows(x_ref, g_ref):
    x = x_ref[...]
    ms = jnp.mean(x * x, axis=-1, keepdims=True)
    return (x * lax.rsqrt(ms + RMS_EPS) * g_ref[...]).astype(BF16)


def _in_proj_a_kernel(x_ref, g_ref, w_ref, o_ref):
    o_ref[...] = jnp.dot(_norm_rows(x_ref, g_ref), w_ref[...], preferred_element_type=F32)


def _in_proj_a(x2d, g, w_bf16):
    n, d = x2d.shape
    tm = min(512, n)
    return pl.pallas_call(
        _in_proj_a_kernel,
        out_shape=jax.ShapeDtypeStruct((n, IN_PAD_A), F32),
        grid=(n // tm,),
        in_specs=[pl.BlockSpec((tm, d), lambda i: (i, 0)),
                  pl.BlockSpec((1, d), lambda i: (0, 0)),
                  pl.BlockSpec((d, IN_PAD_A), lambda i: (0, 0))],
        out_specs=pl.BlockSpec((tm, IN_PAD_A), lambda i: (i, 0)),
        compiler_params=_params("parallel"),
        name="in_proj_a",
    )(x2d, g.reshape(1, d), w_bf16)


def _in_proj_b_kernel(x_ref, g_ref, w_ref, qx_ref, kv_ref, kb_ref, vb_ref, nz_ref, gl_ref, gt_ref, su_ref, sz_ref):
    xn = _norm_rows(x_ref, g_ref)

    def seg(name):
        o, w = _B_OFF[name]
        return jnp.dot(xn, w_ref[:, o:o + w], preferred_element_type=F32)
    qx_ref[...] = seg("qx").astype(BF16)
    kv = seg("kv")
    kv_ref[...] = kv
    kvb = kv.astype(BF16)
    ones = jnp.ones((kv.shape[0], LANES), BF16)
    for br in range(3):
        kb_ref[:, br * LANES:(br + 1) * LANES] = kvb[:, 2 * br * LANES:(2 * br + 1) * LANES]
        vb_ref[:, 2 * br * LANES:(2 * br + 1) * LANES] = kvb[:, (2 * br + 1) * LANES:(2 * br + 2) * LANES]
        vb_ref[:, (2 * br + 1) * LANES:(2 * br + 2) * LANES] = ones
    nz_ref[...] = seg("nz")
    gl_ref[...] = seg("gl")
    gt_ref[...] = seg("gt")
    su_ref[...] = seg("su")
    sz_ref[...] = seg("sz")


def _in_proj_b(x2d, b, l, g, w_bf16):
    n, d = x2d.shape
    tm = min(512, l)
    nt = l // tm
    row = lambda bi, ti: (bi * nt + ti, 0)
    tcol = lambda bi, ti: (ti, bi)
    shapes = [((n, QX_W), BF16), ((n, KV_WIDTH), F32), ((n, 3 * LANES), BF16), ((n, 6 * LANES), BF16),
              ((n, NSA_WIDTH), F32), ((n, GL_W), F32), ((n, GT_W), F32),
              ((l, b * S5_WIDTH), F32), ((l, b * S5_WIDTH), F32)]
    out_specs = [pl.BlockSpec((tm, s[0][1]), row) for s in shapes[:7]] + \
                [pl.BlockSpec((tm, S5_WIDTH), tcol)] * 2
    return pl.pallas_call(
        _in_proj_b_kernel,
        out_shape=[jax.ShapeDtypeStruct(*s) for s in shapes],
        grid=(b, nt),
        in_specs=[pl.BlockSpec((tm, d), row),
                  pl.BlockSpec((1, d), lambda bi, ti: (0, 0)),
                  pl.BlockSpec((d, IN_PAD_B), lambda bi, ti: (0, 0))],
        out_specs=out_specs,
        compiler_params=_params("parallel", "parallel"),
        name="in_proj_b",
    )(x2d, g.reshape(1, d), w_bf16)


def _compress_kernel(x_ref, pe_ref, w1_ref, w2_ref, o_ref, *, n_ch):
    def half(s):
        acc = jnp.zeros((n_ch, LANES), F32)
        for p in range(CMP_STRIDE):
            xp = x_ref[pl.ds(p, n_ch, stride=CMP_STRIDE), :] + pe_ref[0, s * CMP_STRIDE + p:s * CMP_STRIDE + p + 1, :]
            acc = acc + jnp.dot(xp.astype(BF16), w1_ref[0, s * CMP_STRIDE + p], preferred_element_type=F32)
        return acc
    first = half(0)
    second = pltpu.roll(half(1), n_ch - 1, 0)
    hid = jax.nn.gelu(first + second).astype(BF16)
    o_ref[0, 0] = jnp.dot(hid, w2_ref[0], preferred_element_type=F32).astype(BF16)


def _compress_weights(pe, w1, w2):
    eye = jnp.eye(NSA_KV_HEADS, dtype=F32)
    pe2 = jnp.tile(pe, (1, 1, NSA_KV_HEADS))
    w1r = w1.reshape(2, CMP_BLOCK, HEAD_DIM, HEAD_DIM)
    w1bd = jnp.einsum('gh,kpde->kpgdhe', eye, w1r).reshape(2, CMP_BLOCK, LANES, LANES).astype(BF16)
    w2bd = jnp.einsum('gh,kde->kgdhe', eye, w2).reshape(2, LANES, LANES).astype(BF16)
    return pe2, w1bd, w2bd


def _page_copies(pt_ref, pages_ref, buf, sem, layer, bi, part, slot, n_pages):
    out = []
    for j in range(n_pages):
        src = pages_ref.at[layer, pt_ref[bi, j]]
        out.append(pltpu.make_async_copy(src if part is None else src.at[part], buf.at[slot, j], sem.at[slot]))
    return out


def _paged_compress_kernel(pt_ref, pages_ref, pe_ref, w1_ref, w2_ref, o_ref, buf, x_s, sem,
                           *, layer, n_pages, n_steps, page):
    bi, s = pl.program_id(0), pl.program_id(1)
    step = bi * 2 + s
    slot = step % 2

    @pl.when(step == 0)
    def _():
        for c in _page_copies(pt_ref, pages_ref, buf, sem, layer, bi, s, slot, n_pages):
            c.start()

    @pl.when(step + 1 < n_steps)
    def _():
        nxt = step + 1
        for c in _page_copies(pt_ref, pages_ref, buf, sem, layer, nxt // 2, nxt % 2, 1 - slot, n_pages):
            c.start()
    for c in _page_copies(pt_ref, pages_ref, buf, sem, layer, bi, s, slot, n_pages):
        c.wait()

    def to_rows(j, carry):
        x_s[pl.ds(pl.multiple_of(j * page, page), page), :] = jnp.transpose(buf[slot, j].reshape(LANES, page))
        return carry
    lax.fori_loop(0, n_pages, to_rows, 0)
    _compress_kernel(x_s, pe_ref, w1_ref, w2_ref, o_ref, n_ch=n_pages * page // CMP_STRIDE)


def _paged_compress(pages_t, layer, page_table, pe, w1, w2):
    b, n_pages = page_table.shape
    page = pages_t.shape[-1]
    t = n_pages * page
    n_ch = t // CMP_STRIDE
    pe2, w1bd, w2bd = _compress_weights(pe, w1, w2)
    return pl.pallas_call(
        functools.partial(_paged_compress_kernel, layer=layer, n_pages=n_pages, n_steps=2 * b, page=page),
        out_shape=jax.ShapeDtypeStruct((2, b, n_ch, LANES), BF16),
        grid_spec=pltpu.PrefetchScalarGridSpec(
            num_scalar_prefetch=1,
            grid=(b, 2),
            in_specs=[pl.BlockSpec(memory_space=pl.ANY),
                      pl.BlockSpec((1, CMP_BLOCK, LANES), lambda bi, s, pt: (s, 0, 0)),
                      pl.BlockSpec((1, CMP_BLOCK, LANES, LANES), lambda bi, s, pt: (s, 0, 0, 0)),
                      pl.BlockSpec((1, LANES, LANES), lambda bi, s, pt: (s, 0, 0))],
            out_specs=pl.BlockSpec((1, 1, n_ch, LANES), lambda bi, s, pt: (s, bi, 0, 0)),
            scratch_shapes=[pltpu.VMEM((2, n_pages, NSA_KV_HEADS, HEAD_DIM, page), F32),
                            pltpu.VMEM((t, LANES), F32), pltpu.SemaphoreType.DMA((2,))]),
        compiler_params=_params("arbitrary", "arbitrary"),
        name="nsa_paged_compress",
    )(page_table, pages_t, pe2, w1bd, w2bd)


def _compress(kv, b, t, pe, w1, w2):
    n_ch = t // CMP_STRIDE
    pe2, w1bd, w2bd = _compress_weights(pe, w1, w2)
    return pl.pallas_call(
        functools.partial(_compress_kernel, n_ch=n_ch),
        out_shape=jax.ShapeDtypeStruct((2, b, n_ch, LANES), BF16),
        grid=(b, 2),
        in_specs=[pl.BlockSpec((t, LANES), lambda bi, s: (bi, s)),
                  pl.BlockSpec((1, CMP_BLOCK, LANES), lambda bi, s: (s, 0, 0)),
                  pl.BlockSpec((1, CMP_BLOCK, LANES, LANES), lambda bi, s: (s, 0, 0, 0)),
                  pl.BlockSpec((1, LANES, LANES), lambda bi, s: (s, 0, 0))],
        out_specs=pl.BlockSpec((1, 1, n_ch, LANES), lambda bi, s: (s, bi, 0, 0)),
        compiler_params=_params("parallel", "parallel"),
        name="nsa_compress",
    )(kv, pe2, w1bd, w2bd)


def _split3(x):
    hi = x.astype(BF16)
    r = x - hi.astype(F32)
    mid = r.astype(BF16)
    lo = (r - mid.astype(F32)).astype(BF16)
    return hi, mid, lo


def _cmp_sel_kernel(qx_ref, kc_ref, vc_ref, gt_ref, m2_ref, o_ref, sel_ref, sc_s, *, tq, nc):
    q0 = pl.program_id(1) * tq
    qpos = q0 + lax.broadcasted_iota(jnp.int32, (tq, 1), 0)
    cend = lax.broadcasted_iota(jnp.int32, (1, nc), 1) * CMP_STRIDE + (CMP_BLOCK - 1)
    dist = (qpos - cend).astype(F32)
    ok = dist >= 0.0
    kc = kc_ref[0, 0]
    vc = vc_ref[0, 0]
    gts = jax.nn.sigmoid(gt_ref[...])
    imps = []
    for g in range(NSA_KV_HEADS):
        imp = jnp.zeros((tq, nc), F32)
        for j in range(NSA_HPG):
            h = g * NSA_HPG + j
            s = lax.dot_general(qx_ref[:, h * LANES:(h + 1) * LANES], kc, (((1,), (1,)), ((), ())),
                                preferred_element_type=F32)
            s = jnp.where(ok, s - ALIBI[h] * dist, MASKED)
            e = jnp.where(ok, jnp.exp(s - jnp.max(s, axis=-1, keepdims=True)), 0.0)
            p = e / jnp.maximum(jnp.sum(e, axis=-1, keepdims=True), 1e-30)
            imp = imp + p
            o = jnp.dot(p.astype(BF16), vc, preferred_element_type=F32)
            o_ref[:, h * HEAD_DIM:(h + 1) * HEAD_DIM] = (o[:, g * HEAD_DIM:(g + 1) * HEAD_DIM]
                                                         * gts[:, 3 * h:3 * h + 1])
        imps.append(imp)
    m2 = m2_ref[...]
    score = sum(jnp.dot(piece, m2, preferred_element_type=F32) for piece in _split3(jnp.concatenate(imps, axis=1)))
    lane = lax.broadcasted_iota(jnp.int32, (1, LANES), 1)
    blk = lane & (SLC_BLOCK - 1)
    cur = qpos >> 6
    forced = (blk == 0) | (blk == cur) | (blk == cur - 1)
    elig = (blk * SLC_BLOCK) <= qpos
    sc_s[...] = jnp.transpose(jnp.where(elig, jnp.where(forced, NEG_BIG, score), -NEG_BIG))
    sub = lax.broadcasted_iota(jnp.int32, (SUBLANES, 1), 0)
    kept = []
    for g in range(NSA_KV_HEADS):
        for r in range(SLC_BLOCK // SUBLANES):
            mine = sc_s[g * SLC_BLOCK + r * SUBLANES:g * SLC_BLOCK + (r + 1) * SUBLANES, :]
            cnt = jnp.zeros(mine.shape, jnp.int32)
            for i in range(SLC_BLOCK):
                cand = sc_s[g * SLC_BLOCK + i:g * SLC_BLOCK + i + 1, :]
                if i < r * SUBLANES:
                    inc = jnp.where(cand >= mine, 1, 0)
                elif i >= (r + 1) * SUBLANES:
                    inc = jnp.where(cand > mine, 1, 0)
                else:
                    tie = jnp.where(sub > i - r * SUBLANES, 1, 0)
                    inc = jnp.where(cand > mine, 1, jnp.where(cand == mine, tie, 0))
                cnt = cnt + inc
            kept.append(jnp.where((cnt < N_SELECT) & (mine > -0.5 * NEG_BIG), 1.0, 0.0))
    sel_ref[...] = jnp.transpose(jnp.concatenate(kept, axis=0)).astype(BF16)


def _slc_map_matrix(nc, n_cmp):
    n = jnp.arange(nc)[:, None]
    j = jnp.arange(SLC_BLOCK)[None, :]
    ratio = SLC_BLOCK // CMP_STRIDE
    m = ((n >= ratio * j) & (n <= ratio * j + ratio - 1)).astype(F32) + \
        ((n >= ratio * j - 1) & (n <= ratio * j + ratio - 2)).astype(F32)
    m = jnp.where(n < n_cmp, m, 0.0)
    z = jnp.zeros_like(m)
    return jnp.concatenate([jnp.concatenate([m, z], 1), jnp.concatenate([z, m], 1)], 0).astype(BF16)


def _cmp_select(qx, kvc, gt, b, l):
    nc = kvc.shape[2]
    tq = min(256, l)
    nt = l // tq
    row = lambda bi, ti: (bi * nt + ti, 0)
    return pl.pallas_call(
        functools.partial(_cmp_sel_kernel, tq=tq, nc=nc),
        out_shape=[jax.ShapeDtypeStruct((b * l, NSA_WIDTH), F32), jax.ShapeDtypeStruct((b * l, LANES), BF16)],
        grid=(b, nt),
        in_specs=[pl.BlockSpec((tq, QX_W), row),
                  pl.BlockSpec((1, 1, nc, LANES), lambda bi, ti: (0, bi, 0, 0)),
                  pl.BlockSpec((1, 1, nc, LANES), lambda bi, ti: (1, bi, 0, 0)),
                  pl.BlockSpec((tq, GT_W), row),
                  pl.BlockSpec((2 * nc, LANES), lambda bi, ti: (0, 0))],
        out_specs=[pl.BlockSpec((tq, NSA_WIDTH), row), pl.BlockSpec((tq, LANES), row)],
        scratch_shapes=[pltpu.VMEM((LANES, tq), F32)],
        compiler_params=_params("parallel", "parallel"),
        name="nsa_cmp_select",
    )(qx, kvc, kvc, gt, _slc_map_matrix(nc, nc - 1))


def _flash_kernel(qx_ref, k_ref, v_ref, sel_ref, e_ref, gt_ref, o_ref, qs_s, ad_s, sl_s, acc_s, m_s,
                  *, tq, tk, mode, gate_col):
    i = pl.program_id(1)
    q0 = i * tq
    rows = NSA_HPG * tq
    d0 = (lax.broadcasted_iota(jnp.int32, (tq, tk), 0) - lax.broadcasted_iota(jnp.int32, (tq, tk), 1)).astype(F32)
    gts = jax.nn.sigmoid(gt_ref[...])

    def tile(g, k0, causal, winlow):
        kt = k_ref[pl.ds(k0, tk), :]
        vt = v_ref[pl.ds(k0, tk), :]
        c = (q0 - k0).astype(F32)
        s = lax.dot_general(qs_s[...], kt, (((1,), (1,)), ((), ())), preferred_element_type=F32) - ad_s[...]
        bias = None
        if mode == "slc":
            bias = (jnp.dot(sel_ref[...], e_ref[g, :, pl.ds(k0, tk)], preferred_element_type=F32) - 1.0) * (-MASKED)
        if causal:
            cb = jnp.where(d0 + c >= 0.0, 0.0, MASKED)
            bias = cb if bias is None else bias + cb
        if winlow:
            wb = jnp.where(d0 + c < float(WINDOW), 0.0, MASKED)
            bias = wb if bias is None else bias + wb
        if bias is not None:
            s = (s.reshape(NSA_HPG, tq, tk) + bias[None]).reshape(rows, tk)
        shift = sl_s[...] * c
        m_prev = m_s[...]
        m_new = jnp.maximum(m_prev, jnp.max(s, axis=1, keepdims=True) - shift)
        p = jnp.exp(s - _lane_tile(m_new + shift, tk // LANES))
        alpha = jnp.exp(m_prev - m_new)
        acc_s[...] = _lane_tile(alpha, 2) * acc_s[...] + jnp.dot(p.astype(BF16), vt,
                                                                          preferred_element_type=F32)
        m_s[...] = m_new

    for g in range(NSA_KV_HEADS):
        for jh in range(NSA_HPG):
            h = g * NSA_HPG + jh
            qs_s[jh * tq:(jh + 1) * tq, :] = qx_ref[:, h * LANES:(h + 1) * LANES]
            ad_s[jh * tq:(jh + 1) * tq, :] = ALIBI[h] * d0
            sl_s[jh * tq:(jh + 1) * tq, :] = jnp.full((tq, LANES), ALIBI[h], F32)
        m_s[...] = jnp.full(m_s.shape, MASKED, F32)
        acc_s[...] = jnp.zeros(acc_s.shape, F32)
        if mode == "slc":
            last = (q0 + tq - 1) // tk

            def body(j, carry):
                tile(g, pl.multiple_of(j * tk, tk), False, False)
                return carry
            lax.fori_loop(0, last, body, 0)
            tile(g, pl.multiple_of(last * tk, tk), True, False)
        else:
            span = WINDOW // tk
            for back in range(span, 0, -1):
                @pl.when(i >= back)
                def _():
                    tile(g, pl.multiple_of((i - back) * tk, tk), False, back == span)
            tile(g, pl.multiple_of(i * tk, tk), True, False)
        for jh in range(NSA_HPG):
            h = g * NSA_HPG + jh
            c = 3 * h + gate_col
            a = acc_s[jh * tq:(jh + 1) * tq, :]
            o = a[:, g * HEAD_DIM:(g + 1) * HEAD_DIM] / a[:, LANES + g * HEAD_DIM:LANES + (g + 1) * HEAD_DIM]
            o_ref[:, h * HEAD_DIM:(h + 1) * HEAD_DIM] = o * gts[:, c:c + 1]


def _block_expand_matrix(l):
    blk = jnp.arange(SLC_BLOCK)[:, None] == (jnp.arange(l)[None, :] // SLC_BLOCK)
    z = jnp.zeros_like(blk)
    return jnp.stack([jnp.concatenate([blk, z], 0), jnp.concatenate([z, blk], 0)]).astype(BF16)


def _flash(qx, kb, vb, sel, gt, b, l, mode):
    tq = min(256, l)
    tk = 2 * tq if mode == "slc" else tq
    nt = l // tq
    rows = NSA_HPG * tq
    row = lambda bi, ti: (bi * nt + ti, 0)
    branch = 1 if mode == "slc" else 2
    emat = _block_expand_matrix(l)
    return pl.pallas_call(
        functools.partial(_flash_kernel, tq=tq, tk=tk, mode=mode, gate_col=branch),
        out_shape=jax.ShapeDtypeStruct((b * l, NSA_WIDTH), F32),
        grid=(b, nt),
        in_specs=[pl.BlockSpec((tq, QX_W), row),
                  pl.BlockSpec((l, LANES), lambda bi, ti: (bi, branch)),
                  pl.BlockSpec((l, 2 * LANES), lambda bi, ti: (bi, branch)),
                  pl.BlockSpec((tq, LANES), row),
                  pl.BlockSpec((NSA_KV_HEADS, LANES, l), lambda bi, ti: (0, 0, 0)),
                  pl.BlockSpec((tq, GT_W), row)],
        out_specs=pl.BlockSpec((tq, NSA_WIDTH), row),
        scratch_shapes=[pltpu.VMEM((rows, LANES), BF16), pltpu.VMEM((rows, tk), F32), pltpu.VMEM((rows, LANES), F32),
                        pltpu.VMEM((rows, 2 * LANES), F32), pltpu.VMEM((rows, LANES), F32)],
        compiler_params=_params("parallel", "parallel"),
        name="nsa_" + mode,
    )(qx, kb, vb, sel, emat, gt)


def _decode_nsa_kernel(pt_ref, slc_ref, win_ref, kc_ref, vc_ref, q_ref, kvn_ref, gt_ref, m2_ref,
                       oc_ref, os_ref, ow_ref, buf, sem, *, layer, n_pages, page, nb, l, n_slc, nsp):
    bi = pl.program_id(0)
    slot = bi % 2
    pos0 = n_pages * page
    nc = kc_ref.shape[2]
    rows = NSA_HPG * l

    @pl.when(bi == 0)
    def _():
        for c in _page_copies(pt_ref, slc_ref, buf, sem, layer, bi, None, slot, n_pages):
            c.start()

    @pl.when(bi + 1 < nb)
    def _():
        for c in _page_copies(pt_ref, slc_ref, buf, sem, layer, bi + 1, None, 1 - slot, n_pages):
            c.start()

    qf = q_ref[...]
    kvn = kvn_ref[...]
    gts = jax.nn.sigmoid(gt_ref[...])
    t_col = lax.broadcasted_iota(jnp.int32, (rows, 1), 0) % l
    t_f = t_col.astype(F32)
    lane = lax.broadcasted_iota(jnp.int32, (1, LANES), 1)
    lane_f = lane.astype(F32)

    def pad_rows(x):
        return jnp.concatenate([x, jnp.zeros((LANES - l, x.shape[1]), x.dtype)], axis=0)

    def group_q(g):
        q = jnp.concatenate([qf[:, (g * NSA_HPG + j) * LANES:(g * NSA_HPG + j + 1) * LANES]
                             for j in range(NSA_HPG)], axis=0).astype(BF16)
        sl = jnp.concatenate([jnp.full((l, 1), ALIBI[g * NSA_HPG + j], F32) for j in range(NSA_HPG)], axis=0)
        return q, sl

    def new_keys(q, sl, branch):
        kn = pad_rows(kvn[:, 2 * branch * LANES:(2 * branch + 1) * LANES]).astype(BF16)
        vn = pad_rows(kvn[:, (2 * branch + 1) * LANES:(2 * branch + 2) * LANES]).astype(BF16)
        s = lax.dot_general(q, kn, (((1,), (1,)), ((), ())), preferred_element_type=F32)
        s = jnp.where(lane <= t_col, s - sl * (t_f - lane_f), MASKED)
        return s, vn

    def write(o_ref, o, den, g, col):
        for j in range(NSA_HPG):
            h = g * NSA_HPG + j
            c = 3 * h + col
            oh = o[j * l:(j + 1) * l, g * HEAD_DIM:(g + 1) * HEAD_DIM] / den[j * l:(j + 1) * l]
            o_ref[:, h * HEAD_DIM:(h + 1) * HEAD_DIM] = oh * gts[:, c:c + 1]

    cend = (lax.broadcasted_iota(jnp.int32, (1, nc), 1) * CMP_STRIDE + (CMP_BLOCK - 1)).astype(F32)
    imps = []
    qs = []
    for g in range(NSA_KV_HEADS):
        q, sl = group_q(g)
        qs.append((q, sl))
        dist = (t_f + float(pos0)) - cend
        ok = dist >= 0.0
        s = lax.dot_general(q, kc_ref[0, 0], (((1,), (1,)), ((), ())), preferred_element_type=F32)
        s = jnp.where(ok, s - sl * dist, MASKED)
        e = jnp.where(ok, jnp.exp(s - jnp.max(s, axis=-1, keepdims=True)), 0.0)
        p = e / jnp.maximum(jnp.sum(e, axis=-1, keepdims=True), 1e-30)
        write(oc_ref, jnp.dot(p.astype(BF16), vc_ref[0, 0], preferred_element_type=F32),
              jnp.ones((rows, 1), F32), g, 0)
        imps.append(sum(p[j * l:(j + 1) * l] for j in range(NSA_HPG)))
    m2 = m2_ref[...]
    score = sum(jnp.dot(piece, m2, preferred_element_type=F32) for piece in _split3(jnp.concatenate(imps, axis=1)))
    lane2 = lax.broadcasted_iota(jnp.int32, (1, NSA_KV_HEADS * nsp), 1)
    blk = lane2 & (nsp - 1)
    qpos = pos0 + lax.broadcasted_iota(jnp.int32, (l, 1), 0)
    cur = qpos // SLC_BLOCK
    forced = (blk == 0) | (blk == cur) | (blk == cur - 1)
    elig = ((blk * SLC_BLOCK) <= qpos) & (blk < n_slc)
    sc = jnp.where(elig, jnp.where(forced, NEG_BIG, score), -NEG_BIG)
    cnt = jnp.zeros(sc.shape, jnp.int32)
    for i in range(n_slc):
        col = jnp.where(lane2 < nsp, sc[:, i:i + 1], sc[:, nsp + i:nsp + i + 1])
        cnt = cnt + jnp.where(col > sc, 1, jnp.where(col == sc, jnp.where(blk > i, 1, 0), 0))
    sel = jnp.where((cnt < N_SELECT) & (sc > -0.5 * NEG_BIG), 1.0, 0.0)

    w = win_ref.shape[-1]
    wpos = lax.broadcasted_iota(jnp.int32, (1, w), 1)
    for g in range(NSA_KV_HEADS):
        q, sl = qs[g]
        kt = win_ref[0, 0, 0].reshape(LANES, w).astype(BF16)
        vt = win_ref[0, 0, 1].reshape(LANES, w).astype(BF16)
        dist = (t_f + float(w)) - wpos.astype(F32)
        s = jnp.dot(q, kt, preferred_element_type=F32)
        s = jnp.where(dist < float(WINDOW), s - sl * dist, MASKED)
        sn, vn = new_keys(q, sl, 2)
        m = jnp.maximum(jnp.max(s, axis=-1, keepdims=True), jnp.max(sn, axis=-1, keepdims=True))
        p, pn = jnp.exp(s - m), jnp.exp(sn - m)
        o = (lax.dot_general(p.astype(BF16), vt, (((1,), (1,)), ((), ())), preferred_element_type=F32)
             + jnp.dot(pn.astype(BF16), vn, preferred_element_type=F32))
        write(ow_ref, o, jnp.sum(p, axis=-1, keepdims=True) + jnp.sum(pn, axis=-1, keepdims=True), g, 2)

    for c in _page_copies(pt_ref, slc_ref, buf, sem, layer, bi, None, slot, n_pages):
        c.wait()
    half = page // SLC_BLOCK
    for g in range(NSA_KV_HEADS):
        q, sl = qs[g]
        ss = []
        for j in range(n_pages):
            kt = buf[slot, j, 0].reshape(LANES, page).astype(BF16)
            s = jnp.dot(q, kt, preferred_element_type=F32)
            keep = sel[:, g * nsp + half * j:g * nsp + half * j + 1]
            for hb in range(1, half):
                keep = jnp.where(lane < hb * SLC_BLOCK, keep, sel[:, g * nsp + half * j + hb:g * nsp + half * j + hb + 1])
            keep = jnp.concatenate([keep] * NSA_HPG, axis=0) > 0.5
            dist = (t_f + float(pos0 - j * page)) - lane_f
            ss.append(jnp.where(keep, s - sl * dist, MASKED))
        sn, vn = new_keys(q, sl, 1)
        m = jnp.max(sn, axis=-1, keepdims=True)
        for s in ss:
            m = jnp.maximum(m, jnp.max(s, axis=-1, keepdims=True))
        pn = jnp.exp(sn - m)
        den = jnp.sum(pn, axis=-1, keepdims=True)
        o = jnp.dot(pn.astype(BF16), vn, preferred_element_type=F32)
        for j in range(n_pages):
            p = jnp.exp(ss[j] - m)
            den = den + jnp.sum(p, axis=-1, keepdims=True)
            vt = buf[slot, j, 1].reshape(LANES, page).astype(BF16)
            o = o + lax.dot_general(p.astype(BF16), vt, (((1,), (1,)), ((), ())), preferred_element_type=F32)
        write(os_ref, o, den, g, 1)


def _decode_nsa(slc_t, win_t, layer, page_table, kvc, qf, kvn, gt, b, l):
    n_pages, page = page_table.shape[1], slc_t.shape[-1]
    pos0 = n_pages * page
    nc = kvc.shape[2]
    n_slc = -(-(pos0 + l) // SLC_BLOCK)
    nsp = 1 << (n_slc - 1).bit_length()
    assert pos0 % SLC_BLOCK == 0 and l <= SLC_BLOCK and page % SLC_BLOCK == 0 and l % 8 == 0
    n = jnp.arange(nc)[:, None]
    jj = jnp.arange(nsp)[None, :]
    ratio = SLC_BLOCK // CMP_STRIDE
    m = ((n >= ratio * jj) & (n <= ratio * jj + ratio - 1)).astype(F32) + \
        ((n >= ratio * jj - 1) & (n <= ratio * jj + ratio - 2)).astype(F32)
    m = jnp.where((n < nc - 1) & (jj < n_slc), m, 0.0)
    z = jnp.zeros_like(m)
    m2 = jnp.concatenate([jnp.concatenate([m, z], 1), jnp.concatenate([z, m], 1)], 0).astype(BF16)
    row = lambda bi, pt: (bi, 0)
    w = win_t.shape[-1]
    out = jax.ShapeDtypeStruct((b * l, NSA_WIDTH), F32)
    return pl.pallas_call(
        functools.partial(_decode_nsa_kernel, layer=layer, n_pages=n_pages, page=page, nb=b, l=l, n_slc=n_slc,
                          nsp=nsp),
        out_shape=[out, out, out],
        grid_spec=pltpu.PrefetchScalarGridSpec(
            num_scalar_prefetch=1,
            grid=(b,),
            in_specs=[pl.BlockSpec(memory_space=pl.ANY),
                      pl.BlockSpec((1, 1, 2, NSA_KV_HEADS, HEAD_DIM, w), lambda bi, pt: (layer, bi, 0, 0, 0, 0)),
                      pl.BlockSpec((1, 1, nc, LANES), lambda bi, pt: (0, bi, 0, 0)),
                      pl.BlockSpec((1, 1, nc, LANES), lambda bi, pt: (1, bi, 0, 0)),
                      pl.BlockSpec((l, QX_W), row), pl.BlockSpec((l, KV_WIDTH), row), pl.BlockSpec((l, GT_W), row),
                      pl.BlockSpec((NSA_KV_HEADS * nc, NSA_KV_HEADS * nsp), lambda bi, pt: (0, 0))],
            out_specs=[pl.BlockSpec((l, NSA_WIDTH), row)] * 3,
            scratch_shapes=[pltpu.VMEM((2, n_pages, 2, NSA_KV_HEADS, HEAD_DIM, page), F32),
                            pltpu.SemaphoreType.DMA((2,))]),
        compiler_params=_params("arbitrary"),
        name="nsa_decode",
    )(page_table, slc_t, win_t, kvc, kvc, qf, kvn, gt, m2)


def _s5_kernel(u_ref, z_ref, bre_ref, bim_ref, cre_ref, cim_ref, ab_ref, d_ref, wglu_ref, h0_ref,
               o_ref, hout_ref, xr_s, xi_s, h_s, *, nb, tc):
    @pl.when(pl.program_id(0) == 0)
    def _():
        h_s[...] = h0_ref[...]
    bre, bim = bre_ref[...], bim_ref[...]
    for b in range(nb):
        ub = u_ref[:, b * S5_WIDTH:(b + 1) * S5_WIDTH].astype(BF16)
        xr_s[:, b * S5_CH:(b + 1) * S5_CH] = jnp.dot(ub, bre, preferred_element_type=F32)
        xi_s[:, b * S5_CH:(b + 1) * S5_CH] = jnp.dot(ub, bim, preferred_element_type=F32)
    abr = ab_ref[0:1, :]
    abi = ab_ref[1:2, :]

    def step(t, carry):
        hr, hi = carry
        nr = abr * hr - abi * hi + xr_s[pl.ds(t, 1), :]
        ni = abr * hi + abi * hr + xi_s[pl.ds(t, 1), :]
        xr_s[pl.ds(t, 1), :] = nr
        xi_s[pl.ds(t, 1), :] = ni
        return nr, ni
    hr, hi = lax.fori_loop(0, tc, step, (h_s[0:1, :], h_s[1:2, :]), unroll=8)
    h_s[0:1, :] = hr
    h_s[1:2, :] = hi
    hout_ref[...] = h_s[...]
    cre, cim, wglu = cre_ref[...], cim_ref[...], wglu_ref[...]
    for b in range(nb):
        u = u_ref[:, b * S5_WIDTH:(b + 1) * S5_WIDTH]
        y = (jnp.dot(xr_s[:, b * S5_CH:(b + 1) * S5_CH].astype(BF16), cre, preferred_element_type=F32)
             - jnp.dot(xi_s[:, b * S5_CH:(b + 1) * S5_CH].astype(BF16), cim, preferred_element_type=F32)
             + d_ref[...] * u)
        gl = jnp.dot(jax.nn.gelu(y).astype(BF16), wglu, preferred_element_type=F32)
        z = z_ref[:, b * S5_WIDTH:(b + 1) * S5_WIDTH]
        o_ref[:, b * S5_WIDTH:(b + 1) * S5_WIDTH] = (gl[:, :S5_WIDTH] * jax.nn.sigmoid(gl[:, S5_WIDTH:])
                                                       * (z * jax.nn.sigmoid(z)))


def _s5_discretise(lw):
    a_re, a_im = lw['s5_a_re'].astype(F32), lw['s5_a_im'].astype(F32)
    b_re, b_im = lw['s5_b_re'].astype(F32), lw['s5_b_im'].astype(F32)
    dt = jnp.exp(lw['s5_log_dt'].astype(F32))[:, None]
    mag = jnp.exp(a_re * dt)
    abr, abi = mag * jnp.cos(a_im * dt), mag * jnp.sin(a_im * dt)
    den = a_re * a_re + a_im * a_im
    nr = abr - 1.0
    cr = (nr * a_re + abi * a_im) / den
    ci = (abi * a_re - nr * a_im) / den
    bbr = cr[..., None] * b_re - ci[..., None] * b_im
    bbi = cr[..., None] * b_im + ci[..., None] * b_re
    eye = jnp.eye(S5_GROUPS, dtype=F32)
    bd_in = lambda m: jnp.einsum('gh,gpc->gchp', eye, m).reshape(S5_WIDTH, S5_CH).astype(BF16)
    bd_out = lambda m: jnp.einsum('gh,gcp->gphc', eye, m.astype(F32)).reshape(S5_CH, S5_WIDTH).astype(BF16)
    ab = jnp.stack([abr.reshape(S5_CH), abi.reshape(S5_CH)])
    return bd_in(bbr), bd_in(bbi), bd_out(lw['s5_c_re']), bd_out(lw['s5_c_im']), ab


def _s5(su, sz, lw, h0, nb, l):
    bre, bim, cre, cim, ab = _s5_discretise(lw)
    tc = min(256, l)
    const = lambda t: (0, 0)
    return pl.pallas_call(
        functools.partial(_s5_kernel, nb=nb, tc=tc),
        out_shape=[jax.ShapeDtypeStruct((l, nb * S5_WIDTH), F32), jax.ShapeDtypeStruct((2, nb * S5_CH), F32)],
        grid=(l // tc,),
        in_specs=[pl.BlockSpec((tc, nb * S5_WIDTH), lambda t: (t, 0)),
                  pl.BlockSpec((tc, nb * S5_WIDTH), lambda t: (t, 0)),
                  pl.BlockSpec((S5_WIDTH, S5_CH), const), pl.BlockSpec((S5_WIDTH, S5_CH), const),
                  pl.BlockSpec((S5_CH, S5_WIDTH), const), pl.BlockSpec((S5_CH, S5_WIDTH), const),
                  pl.BlockSpec((2, nb * S5_CH), const), pl.BlockSpec((1, S5_WIDTH), const),
                  pl.BlockSpec((S5_WIDTH, 2 * S5_WIDTH), const), pl.BlockSpec((2, nb * S5_CH), const)],
        out_specs=[pl.BlockSpec((tc, nb * S5_WIDTH), lambda t: (t, 0)), pl.BlockSpec((2, nb * S5_CH), const)],
        scratch_shapes=[pltpu.VMEM((tc, nb * S5_CH), F32), pltpu.VMEM((tc, nb * S5_CH), F32),
                        pltpu.VMEM((2, nb * S5_CH), F32)],
        compiler_params=_params("arbitrary"),
        name="s5",
    )(su, sz, bre, bim, cre, cim, jnp.tile(ab, (1, nb)), lw['s5_d'].astype(F32).reshape(1, S5_WIDTH),
      lw['s5_w_glu'].astype(BF16), h0)


GLA_QK = GLA_HEADS * GLA_DK


def _gla_kernel(gl_ref, gt_ref, wg_ref, bg_ref, gn_ref, pall_ref, lm_ref, seg_ref, s0_ref, o_ref, sout_ref, st_s,
                *, r, t, nl):
    nseq = r // t

    @pl.when(pl.program_id(1) == 0)
    def _():
        st_s[...] = s0_ref[...]
    q = gl_ref[:, 0:GLA_QK] * (GLA_DK ** -0.5)
    k = gl_ref[:, GLA_QK:2 * GLA_QK]
    v = gl_ref[:, 2 * GLA_QK:2 * GLA_QK + GLA_WIDTH].astype(BF16)
    gz = gl_ref[:, 2 * GLA_QK + GLA_WIDTH:]
    lg = jnp.dot(gt_ref[...].astype(BF16), wg_ref[...], preferred_element_type=F32) + bg_ref[...]
    la = (jnp.minimum(lg, 0.0) - jnp.log(1.0 + jnp.exp(-jnp.abs(lg)))) * (1.0 / GLA_GATE_TAU)
    pall = pall_ref[...]
    cums = sum(jnp.dot(pall, piece, preferred_element_type=F32) for piece in _split3(la))
    cum = cums[0:r]
    cl = cums[(nl + 1) * r:(nl + 2) * r]
    lane_qk = lax.broadcasted_iota(jnp.int32, (1, GLA_QK), 1) // GLA_DK
    lane_v = lax.broadcasted_iota(jnp.int32, (1, GLA_WIDTH), 1) // GLA_DV

    def stack(x):
        return jnp.concatenate([jnp.where(lane_qk == h, x, 0.0) for h in range(GLA_HEADS)], axis=0).astype(BF16)

    def level(l, qf, kf):
        s = lax.dot_general(stack(qf), kf.astype(BF16), (((1,), (1,)), ((), ())), preferred_element_type=F32)
        return (s.reshape(GLA_HEADS, r, r) * lm_ref[l][None]).reshape(GLA_HEADS * r, r)
    att = level(0, q, k)
    for l in range(1, nl + 1):
        cm = cums[l * r:(l + 1) * r]
        att = att + level(l, q * jnp.exp(jnp.minimum(cum - cm, 0.0)), k * jnp.exp(jnp.minimum(cm - cum, 0.0)))
    res = jnp.dot(att.astype(BF16), v, preferred_element_type=F32)
    o = sum(jnp.where(lane_v == h, res[h * r:(h + 1) * r], 0.0) for h in range(GLA_HEADS))
    qd = (q * jnp.exp(cum)).astype(BF16)
    kd = k * jnp.exp(cl - cum)
    bd = (lax.broadcasted_iota(jnp.int32, (GLA_QK, GLA_WIDTH), 0) // GLA_DK
          == lax.broadcasted_iota(jnp.int32, (GLA_QK, GLA_WIDTH), 1) // GLA_DV)
    row_seq = lax.broadcasted_iota(jnp.int32, (r, 1), 0) // t
    inter = []
    for i in range(nseq):
        st = st_s[i]
        inter.append(jnp.dot(qd[i * t:(i + 1) * t], st.astype(BF16), preferred_element_type=F32))
        kdi = kd if nseq == 1 else jnp.where(row_seq == i, kd, 0.0)
        upd = jnp.dot(jnp.transpose(kdi).astype(BF16), v, preferred_element_type=F32)
        a_col = jnp.transpose(jnp.broadcast_to(jnp.exp(cl[i * t:i * t + 1, :]), (GLA_QK, GLA_QK)))
        st_s[i] = jnp.where(bd, _lane_tile(a_col, GLA_WIDTH // GLA_QK) * st + upd, 0.0)
    o = o + (inter[0] if nseq == 1 else jnp.concatenate(inter, axis=0))
    sout_ref[...] = st_s[...]
    hi = (o * o).astype(BF16)
    lo = (o * o - hi.astype(F32)).astype(BF16)
    ms = (jnp.dot(hi, seg_ref[...], preferred_element_type=F32) + jnp.dot(lo, seg_ref[...], preferred_element_type=F32))
    o_ref[...] = o * lax.rsqrt(ms + RMS_EPS) * gn_ref[...] * (gz * jax.nn.sigmoid(gz))


def _gla_tables(r, t, nl):
    idx = jnp.arange(r)
    seq = idx // t
    rows = [(seq[:, None] == seq[None, :]) & (idx[None, :] <= idx[:, None])]
    masks = [idx[:, None] == idx[None, :]]
    for l in range(1, nl + 1):
        mid = (idx >> l << l) + (1 << (l - 1)) - 1
        rows.append((seq[:, None] == seq[None, :]) & (idx[None, :] <= mid[:, None]))
        masks.append(((idx[:, None] >> l) == (idx[None, :] >> l)) & (((idx[:, None] >> (l - 1)) & 1) == 1)
                     & (((idx[None, :] >> (l - 1)) & 1) == 0))
    rows.append(seq[:, None] == seq[None, :])
    return jnp.concatenate(rows, 0).astype(BF16), jnp.stack(masks).astype(F32)


def _gla(gl, gt, lw, s0, nb, nchunk, r, t):
    n = gl.shape[0]
    nseq = r // t
    nl = t.bit_length() - 1
    pall, lm = _gla_tables(r, t, nl)
    wg = jnp.zeros((GT_W, GLA_QK), F32).at[3 * NSA_HEADS:3 * NSA_HEADS + GLA_GATE_RANK].set(lw['gla_wg']).astype(BF16)
    eye = jnp.eye(GLA_HEADS, dtype=F32)
    seg = (jnp.einsum('gh,de->gdhe', eye, jnp.ones((GLA_DV, GLA_DV), F32)) / GLA_DV).reshape(GLA_WIDTH, GLA_WIDTH)
    s0bd = jnp.einsum('gh,bgde->bgdhe', eye, s0.astype(F32)).reshape(-1, GLA_QK, GLA_WIDTH)
    row = lambda bi, ci: (bi * nchunk + ci, 0)
    const2 = lambda bi, ci: (0, 0)
    o, sfin = pl.pallas_call(
        functools.partial(_gla_kernel, r=r, t=t, nl=nl),
        out_shape=[jax.ShapeDtypeStruct((n, GLA_WIDTH), F32),
                   jax.ShapeDtypeStruct((nb * nseq, GLA_QK, GLA_WIDTH), F32)],
        grid=(nb, nchunk),
        in_specs=[pl.BlockSpec((r, GL_W), row), pl.BlockSpec((r, GT_W), row),
                  pl.BlockSpec((GT_W, GLA_QK), const2), pl.BlockSpec((1, GLA_QK), const2),
                  pl.BlockSpec((1, GLA_WIDTH), const2), pl.BlockSpec(((nl + 2) * r, r), const2),
                  pl.BlockSpec((nl + 1, r, r), lambda bi, ci: (0, 0, 0)),
                  pl.BlockSpec((GLA_WIDTH, GLA_WIDTH), const2),
                  pl.BlockSpec((nseq, GLA_QK, GLA_WIDTH), lambda bi, ci: (bi, 0, 0))],
        out_specs=[pl.BlockSpec((r, GLA_WIDTH), row),
                   pl.BlockSpec((nseq, GLA_QK, GLA_WIDTH), lambda bi, ci: (bi, 0, 0))],
        scratch_shapes=[pltpu.VMEM((nseq, GLA_QK, GLA_WIDTH), F32)],
        compiler_params=_params("parallel", "arbitrary"),
        name="gla",
    )(gl, gt, wg, lw['gla_bg'].astype(F32).reshape(1, GLA_QK),
      jnp.tile(lw['gla_norm'].astype(F32), GLA_HEADS).reshape(1, GLA_WIDTH), pall, lm, seg.astype(BF16), s0bd)
    sf = sfin.reshape(-1, GLA_HEADS, GLA_DK, GLA_HEADS, GLA_DV)
    state = jnp.stack([sf[:, h, :, h, :] for h in range(GLA_HEADS)], axis=1)
    return o, state


def _merge_kernel(h_ref, oc_ref, os_ref, ow_ref, nz_ref, og_ref, o5_ref, ple_ref, wo_ref, pn_ref, wgate_ref, wproj_ref,
                  fn_ref, out_ref, *, final):
    nz = nz_ref[...]
    o_nsa = (oc_ref[...] + os_ref[...] + ow_ref[...]) * (nz * jax.nn.sigmoid(nz))
    mix = jnp.concatenate([o_nsa, og_ref[...], o5_ref[...]], axis=1).astype(BF16)
    h = h_ref[...] + jnp.dot(mix, wo_ref[...], preferred_element_type=F32)

    def norm(x, g_ref):
        return x * lax.rsqrt(jnp.mean(x * x, axis=-1, keepdims=True) + RMS_EPS) * g_ref[...]
    gate = jax.nn.sigmoid(jnp.dot(norm(h, pn_ref).astype(BF16), wgate_ref[...], preferred_element_type=F32))
    h = h + gate * jnp.dot(ple_ref[...].astype(BF16), wproj_ref[...], preferred_element_type=F32)
    out_ref[...] = norm(h, fn_ref) if final else h


def _merge_call(h2d, o_cmp, o_slc, o_win, nz, o_gla, o_s5, s5_time_major, ple2d, lw, final_norm, nb):
    n, d = h2d.shape
    l = n // nb
    tm = min(512, l)
    nt = l // tm
    row = lambda bi, ti: (bi * nt + ti, 0)
    const = lambda bi, ti: (0, 0)
    o5_spec = pl.BlockSpec((tm, S5_WIDTH), (lambda bi, ti: (ti, bi)) if s5_time_major else row)
    fn = jnp.ones((d,), F32) if final_norm is None else final_norm
    return pl.pallas_call(
        functools.partial(_merge_kernel, final=final_norm is not None),
        out_shape=jax.ShapeDtypeStruct((n, d), F32),
        grid=(nb, nt),
        in_specs=[pl.BlockSpec((tm, d), row)] + [pl.BlockSpec((tm, NSA_WIDTH), row)] * 4 +
                 [pl.BlockSpec((tm, GLA_WIDTH), row), o5_spec, pl.BlockSpec((tm, PLE_DIM), row),
                  pl.BlockSpec((MIX_WIDTH, d), const), pl.BlockSpec((1, d), const), pl.BlockSpec((d, d), const),
                  pl.BlockSpec((PLE_DIM, d), const), pl.BlockSpec((1, d), const)],
        out_specs=pl.BlockSpec((tm, d), row),
        compiler_params=_params("parallel", "parallel"),
        name="merge",
    )(h2d, o_cmp, o_slc, o_win, nz, o_gla, o_s5, ple2d, lw['w_out'].astype(BF16),
      lw['ple_norm'].astype(F32).reshape(1, d), lw['ple_w_gate'].astype(BF16), lw['ple_w_proj'].astype(BF16),
      fn.astype(F32).reshape(1, d))


def _rmsnorm(x, g):
    xf = x.astype(jnp.float32)
    y = xf * lax.rsqrt(jnp.mean(xf * xf, axis=-1, keepdims=True) + RMS_EPS)
    return (y * g.astype(jnp.float32)).astype(x.dtype)


def _masked_softmax(s, mask):
    s = jnp.where(mask, s, -1e30)
    e = jnp.where(mask, jnp.exp(s - jnp.max(s, axis=-1, keepdims=True)), 0.0)
    return e / jnp.maximum(jnp.sum(e, axis=-1, keepdims=True), 1e-30)


def _alibi_slopes():
    return jnp.exp2(-8.0 * jnp.arange(1, NSA_HEADS + 1, dtype=jnp.float32) / NSA_HEADS)


def _compress_kv(x, pe, w1, w2):
    b, t, g, d = x.shape
    ratio = CMP_BLOCK // CMP_STRIDE
    n_ch = t // CMP_STRIDE
    n_cmp = n_ch - ratio + 1
    ch = x[:, :n_ch * CMP_STRIDE].reshape(b, n_ch, CMP_STRIDE, g, d)
    blocks = jnp.concatenate([ch[:, s:s + n_cmp] for s in range(ratio)], axis=2)
    blocks = blocks + pe[None, None, :, None, :]
    flat = jnp.moveaxis(blocks, 2, 3).reshape(b, n_cmp, g, CMP_BLOCK * d)
    return jax.nn.gelu(flat @ w1) @ w2


def _nsa_sparse_block(q, qpos, kc, vc, cend, ksb, vsb, slopes):
    f32 = jnp.float32
    b, qb = q.shape[:2]
    sl = slopes.reshape(NSA_KV_HEADS, NSA_HPG)
    scale = HEAD_DIM ** -0.5
    s = jnp.einsum('bqghd,bngd->bqghn', q, kc).astype(f32) * scale
    dist = (qpos[:, None] - cend[None, :]).astype(f32)
    s = s - sl[:, :, None] * dist[:, None, None, :]
    p = _masked_softmax(s, (cend[None, :] <= qpos[:, None])[:, None, None, :])
    o_cmp = jnp.einsum('bqghn,bngd->bqghd', p.astype(vc.dtype), vc)
    imp = p.sum(axis=3)
    n_cmp = imp.shape[-1]
    n_slc = ksb.shape[2]
    run = imp
    for sft in range(1, CMP_BLOCK // CMP_STRIDE):
        run = run + jnp.pad(imp, ((0, 0), (0, 0), (0, 0), (sft, 0)))[..., :n_cmp]
    ratio = SLC_BLOCK // CMP_STRIDE
    run = jnp.pad(run, ((0, 0), (0, 0), (0, 0), (0, ratio * n_slc - n_cmp)))
    score = run.reshape(b, qb, NSA_KV_HEADS, n_slc, ratio).sum(-1)
    blk = jnp.arange(n_slc)
    cur = qpos // SLC_BLOCK
    eligible = blk[None, :] * SLC_BLOCK <= qpos[:, None]
    forced = (blk[None, :] == 0) | (blk[None, :] == cur[:, None]) | (blk[None, :] == cur[:, None] - 1)
    score = jnp.where(forced[:, None, :], NEG_BIG, score)
    score = jnp.where(eligible[:, None, :], score, -NEG_BIG)
    top_val, top_idx = lax.top_k(score, min(N_SELECT, n_slc))
    sel_ok = top_val > -0.5 * NEG_BIG
    b_ix = jnp.arange(b)[:, None, None, None]
    g_ix = jnp.arange(NSA_KV_HEADS)[None, None, :, None]
    ksel = ksb[b_ix, g_ix, top_idx]
    vsel = vsb[b_ix, g_ix, top_idx]
    kpos = top_idx[..., None] * SLC_BLOCK + jnp.arange(SLC_BLOCK)
    s2 = jnp.einsum('bqghd,bqgkjd->bqghkj', q, ksel).astype(f32) * scale
    dist2 = (qpos[None, :, None, None, None] - kpos).astype(f32)
    s2 = s2 - sl[None, None, :, :, None, None] * dist2[:, :, :, None]
    ok2 = (sel_ok[..., None] & (dist2 >= 0))[:, :, :, None]
    shp = s2.shape
    nk = shp[4] * shp[5]
    p2 = _masked_softmax(s2.reshape(shp[:4] + (nk,)), ok2.reshape(ok2.shape[:4] + (nk,))).reshape(shp)
    o_slc = jnp.einsum('bqghkj,bqgkjd->bqghd', p2.astype(vsel.dtype), vsel)
    return o_cmp, o_slc


def _window_attn(q, qpos, k, v, kpos, slopes):
    sl = slopes.reshape(NSA_KV_HEADS, NSA_HPG)
    s = jnp.einsum('bqghd,bkgd->bqghk', q, k).astype(jnp.float32) * HEAD_DIM ** -0.5
    dist = (qpos[:, None] - kpos[None, :]).astype(jnp.float32)
    s = s - sl[:, :, None] * dist[:, None, None, :]
    ok = ((dist >= 0) & (dist < WINDOW) & (kpos[None, :] >= 0))[:, None, None, :]
    p = _masked_softmax(s, ok)
    return jnp.einsum('bqghk,bkgd->bqghd', p.astype(v.dtype), v)


def _gla_chunked(q, k, v, log_a, s0):
    f32 = jnp.float32
    b, l = q.shape[:2]
    c = min(GLA_CHUNK, l)
    pad = (-l) % c
    n = (l + pad) // c

    def prep(t):
        t = jnp.pad(t.astype(f32), ((0, 0), (0, pad), (0, 0), (0, 0)))
        return t.reshape(b, n, c, GLA_HEADS, t.shape[-1])
    q = prep(q) * GLA_DK ** -0.5
    k, v, log_a = prep(k), prep(v), prep(log_a)
    cum = jnp.cumsum(log_a, axis=2)
    causal = jnp.tril(jnp.ones((c, c), dtype=bool))[:, :, None, None]
    diff = cum[:, :, :, None] - cum[:, :, None, :]
    decay = jnp.where(causal, jnp.exp(jnp.where(causal, diff, 0.0)), 0.0)
    att = jnp.einsum('bnthd,bnshd,bntshd->bnhts', q, k, decay)
    o_intra = jnp.einsum('bnhts,bnshe->bnthe', att, v)
    q_dec = q * jnp.exp(cum)
    k_dec = k * jnp.exp(cum[:, :, -1:] - cum)
    a_tot = jnp.exp(cum[:, :, -1])

    def step(state, inp):
        qd, kd, vv, at = inp
        o = jnp.einsum('bchd,bhde->bche', qd, state)
        state = at[..., None] * state + jnp.einsum('bchd,bche->bhde', kd, vv)
        return state, o
    xs = (jnp.moveaxis(q_dec, 1, 0), jnp.moveaxis(k_dec, 1, 0), jnp.moveaxis(v, 1, 0), jnp.moveaxis(a_tot, 1, 0))
    s_fin, o_inter = lax.scan(step, s0.astype(f32), xs)
    o = o_intra + jnp.moveaxis(o_inter, 0, 1)
    return o.reshape(b, n * c, GLA_HEADS, GLA_DV)[:, :l], s_fin


def _s5_combine(e1, e2):
    a1r, a1i, b1r, b1i = e1
    a2r, a2i, b2r, b2i = e2
    return (a2r * a1r - a2i * a1i, a2r * a1i + a2i * a1r,
            a2r * b1r - a2i * b1i + b2r, a2r * b1i + a2i * b1r + b2i)


def _s5_scan(u, h0, a_re, a_im, b_re, b_im, c_re, c_im, d, log_dt):
    f32 = jnp.float32
    b, l, _ = u.shape
    dt = jnp.exp(log_dt.astype(f32))[:, None]
    mag = jnp.exp(a_re * dt)
    abr, abi = mag * jnp.cos(a_im * dt), mag * jnp.sin(a_im * dt)
    den = a_re * a_re + a_im * a_im
    nr = abr - 1.0
    cr = (nr * a_re + abi * a_im) / den
    ci = (abi * a_re - nr * a_im) / den
    bbr = cr[..., None] * b_re - ci[..., None] * b_im
    bbi = cr[..., None] * b_im + ci[..., None] * b_re
    ug = u.reshape(b, l, S5_GROUPS, S5_GROUP_CH)
    xr = jnp.einsum('gpc,blgc->blgp', bbr, ug)
    xi = jnp.einsum('gpc,blgc->blgp', bbi, ug)
    h0r, h0i = h0[..., 0].astype(f32), h0[..., 1].astype(f32)
    xr = xr.at[:, 0].add(abr * h0r - abi * h0i)
    xi = xi.at[:, 0].add(abr * h0i + abi * h0r)
    ar = jnp.broadcast_to(abr, xr.shape)
    ai = jnp.broadcast_to(abi, xr.shape)
    _, _, hr, hi = lax.associative_scan(_s5_combine, (ar, ai, xr, xi), axis=1)
    y = jnp.einsum('gcp,blgp->blgc', c_re, hr) - jnp.einsum('gcp,blgp->blgc', c_im, hi)
    y = y.reshape(b, l, S5_WIDTH) + d.astype(f32) * u
    return y, jnp.stack([hr[:, -1], hi[:, -1]], axis=-1)


GLA_CHUNK_ROWS = 256


def _split_kv(kv, b, l):
    kv5 = kv.reshape(b, l, 3, 2, NSA_KV_HEADS, HEAD_DIM)
    return kv5[:, :, 0], kv5[:, :, 1], kv5[:, :, 2]


def _prompt_layer(h2d, ple2d, lw, b, l, final_norm):
    qx, kv, kb, vb, nz, gl, gt, su, sz = _in_proj_b(h2d, b, l, lw['norm'], lw['w_in_b'])
    kvc = _compress(kv, b, l, lw['cmp_pe'], lw['cmp_w1'], lw['cmp_w2'])
    o_cmp, sel = _cmp_select(qx, kvc, gt, b, l)
    o_slc = _flash(qx, kb, vb, sel, gt, b, l, "slc")
    o_win = _flash(qx, kb, vb, sel, gt, b, l, "win")
    new_cmp, new_slc, new_win = _split_kv(kv, b, l)
    win_state = new_win[:, l - min(WINDOW, l):]

    t = min(GLA_CHUNK_ROWS, l)
    o_gla, gla_state = _gla(gl, gt, lw, jnp.zeros((b, GLA_HEADS, GLA_DK, GLA_DV), F32), b, l // t, t, t)

    o_s5, hfin = _s5(su, sz, lw, jnp.zeros((2, b * S5_CH), F32), b, l)
    s5_state = jnp.transpose(hfin.reshape(2, b, S5_GROUPS, S5_STATE), (1, 2, 3, 0))

    h2d = _merge_call(h2d, o_cmp, o_slc, o_win, nz, o_gla, o_s5, True, ple2d, lw, final_norm, b)
    return h2d, (new_cmp, new_slc, win_state, gla_state, s5_state)


def _sample_layer(h2d, ple2d, lw, past, b, l, final_norm):
    f32 = jnp.float32
    n = b * l
    qx, kv, kb, vb, nz, gl, gt, su, sz = _in_proj_b(h2d, 1, n, lw['norm'], lw['w_in_b'])
    su, sz = [jnp.transpose(a.reshape(b, l, S5_WIDTH), (1, 0, 2)).reshape(l, b * S5_WIDTH) for a in (su, sz)]
    new_cmp, new_slc, new_win = _split_kv(kv, b, l)
    pt = past['page_table']
    pos0 = pt.shape[1] * past['cmp_t'].shape[-1]
    assert (pos0 + l) // CMP_STRIDE == pos0 // CMP_STRIDE and pos0 % CMP_STRIDE == 0
    assert past['win'].shape[1] == WINDOW <= pos0
    kvc = _paged_compress(past['cmp_t'], past['layer'], pt, lw['cmp_pe'], lw['cmp_w1'], lw['cmp_w2'])
    gated = _decode_nsa(past['slc_t'], past['win_t'], past['layer'], pt, kvc, qx.astype(f32), kv, gt, b, l)
    win_state = jnp.concatenate([past['win'], new_win], axis=1)[:, l:]

    seqs = GLA_SAMPLE_ROWS // l
    o_gla, gla_state = _gla(gl, gt, lw, past['gla'], b // seqs, 1, GLA_SAMPLE_ROWS, l)

    h0 = jnp.transpose(past['s5'].astype(f32), (3, 0, 1, 2)).reshape(2, b * S5_CH)
    o_s5, hfin = _s5(su, sz, lw, h0, b, l)
    s5_state = jnp.transpose(hfin.reshape(2, b, S5_GROUPS, S5_STATE), (1, 2, 3, 0))

    o_s5 = jnp.transpose(o_s5.reshape(l, b, S5_WIDTH), (1, 0, 2)).reshape(n, S5_WIDTH)
    h2d = _merge_call(h2d, gated[0], gated[1], gated[2], nz, o_gla, o_s5, False, ple2d, lw, final_norm, 1)
    return h2d, (new_cmp, new_slc, win_state, gla_state, s5_state)


GLA_SAMPLE_ROWS = 128


def kernel(x_prompt, x_sample, cache_cmp, cache_slc, cache_win, state_gla, state_s5, page_table,
           p_prompt, p_sample, norm_mix, w_in, w_out, cmp_pe, cmp_w1, cmp_w2, gla_wg, gla_bg, gla_norm,
           s5_a_re, s5_a_im, s5_b_re, s5_b_im, s5_c_re, s5_c_im, s5_d, s5_log_dt, s5_w_glu,
           ple_norm, ple_w_gate, ple_w_proj, final_norm):
    depth = w_in.shape[0]
    w_in_b = _relayout_w_in_b(w_in)
    cmp_t = jnp.transpose(cache_cmp, (0, 1, 3, 4, 5, 2))
    slc_t = jnp.transpose(cache_slc, (0, 1, 3, 4, 5, 2))
    win_t = jnp.transpose(cache_win, (0, 1, 3, 4, 5, 2))
    bp, lp, d = x_prompt.shape
    bs, ls, _ = x_sample.shape
    hp, hs = x_prompt.reshape(bp * lp, d), x_sample.reshape(bs * ls, d)
    st_p, st_s = [], []
    for i in range(depth):
        fin = final_norm if i == depth - 1 else None
        lw = {'norm': norm_mix[i], 'w_in_b': w_in_b[i], 'w_out': w_out[i],
              'cmp_pe': cmp_pe[i], 'cmp_w1': cmp_w1[i], 'cmp_w2': cmp_w2[i],
              'gla_wg': gla_wg[i], 'gla_bg': gla_bg[i], 'gla_norm': gla_norm[i],
              's5_a_re': s5_a_re[i], 's5_a_im': s5_a_im[i], 's5_b_re': s5_b_re[i], 's5_b_im': s5_b_im[i],
              's5_c_re': s5_c_re[i], 's5_c_im': s5_c_im[i], 's5_d': s5_d[i], 's5_log_dt': s5_log_dt[i],
              's5_w_glu': s5_w_glu[i], 'ple_norm': ple_norm[i], 'ple_w_gate': ple_w_gate[i],
              'ple_w_proj': ple_w_proj[i]}
        hp, sp = _prompt_layer(hp, p_prompt[i].reshape(bp * lp, PLE_DIM), lw, bp, lp, fin)
        past = {'page_table': page_table, 'win': cache_win[i], 'gla': state_gla[i], 's5': state_s5[i],
                'cmp_t': cmp_t, 'slc_t': slc_t, 'win_t': win_t, 'layer': i}
        hs, ss = _sample_layer(hs, p_sample[i].reshape(bs * ls, PLE_DIM), lw, past, bs, ls, fin)
        st_p.append(sp)
        st_s.append(ss)
    outs = [hp.reshape(bp, lp, d), hs.reshape(bs, ls, d)]
    for j in range(5):
        outs.append(jnp.stack([s[j] for s in st_p]))
        outs.append(jnp.stack([s[j] for s in st_s]))
    return tuple(outs)
```

```python
import functools
import math

import jax
import jax.numpy as jnp
from jax import lax
from jax.experimental import pallas as pl
from jax.experimental.pallas import tpu as pltpu

D_MODEL = 1024
PLE_DIM = 256
HEAD_DIM = 64
NSA_HEADS = 8
NSA_KV_HEADS = 2
NSA_HPG = NSA_HEADS // NSA_KV_HEADS
NSA_WIDTH = NSA_HEADS * HEAD_DIM
CMP_BLOCK = 32
CMP_STRIDE = 16
SLC_BLOCK = 64
N_SELECT = 16
WINDOW = 512
SPARSE_Q_BLOCK = 64
WIN_Q_BLOCK = 128
GLA_HEADS = 4
GLA_DK = 32
GLA_DV = 64
GLA_WIDTH = GLA_HEADS * GLA_DV
GLA_GATE_RANK = 16
GLA_GATE_TAU = 16.0
GLA_CHUNK = 32
S5_GROUPS = 16
S5_GROUP_CH = 16
S5_STATE = 64
S5_WIDTH = S5_GROUPS * S5_GROUP_CH
S5_CH = S5_GROUPS * S5_STATE
MIX_WIDTH = NSA_WIDTH + GLA_WIDTH + S5_WIDTH
KV_WIDTH = 6 * NSA_KV_HEADS * HEAD_DIM
RMS_EPS = 1e-6
NEG_BIG = 1e9
MASKED = -1e30

LANES = 128
SUBLANES = 8
VMEM_LIMIT = 48 * 1024 * 1024

F32 = jnp.float32
BF16 = jnp.bfloat16
ALIBI = tuple(2.0 ** (-8.0 * (h + 1) / NSA_HEADS) for h in range(NSA_HEADS))

_SRC = {}
_off = 0
for _name, _w in (("nq", NSA_WIDTH), ("nkv", KV_WIDTH), ("ngate", 3 * NSA_HEADS), ("nz", NSA_WIDTH),
                  ("gq", GLA_HEADS * GLA_DK), ("gk", GLA_HEADS * GLA_DK), ("gv", GLA_WIDTH),
                  ("glr", GLA_GATE_RANK), ("gz", GLA_WIDTH), ("su", S5_WIDTH), ("sz", S5_WIDTH)):
    _SRC[_name] = (_off, _w)
    _off += _w
IN_WIDTH = _off


def _cols(w, name):
    o, n = _SRC[name]
    return w[..., o:o + n]


_ORDER_A = ("nq", "nkv", "nz", "gq", "gk", "gv", "gz", "su", "sz", "ngate", "glr")
_DST_A = {}
_off = 0
for _name in _ORDER_A:
    _DST_A[_name] = (_off, _SRC[_name][1])
    _off += _SRC[_name][1]
IN_PAD_A = -(-_off // LANES) * LANES


def _relayout_w_in_a(w_in):
    w = jnp.concatenate([_cols(w_in, n) for n in _ORDER_A], axis=-1)
    w = jnp.pad(w, ((0, 0),) * (w.ndim - 1) + ((0, IN_PAD_A - w.shape[-1]),))
    return w.astype(BF16)


def _seg_a(u, name):
    o, w = _DST_A[name]
    return u[..., o:o + w]


QX_W = NSA_HEADS * LANES
GL_W = 2 * GLA_HEADS * GLA_DK + 2 * GLA_WIDTH
GT_W = LANES
_B_SEGS = (("qx", QX_W), ("kv", KV_WIDTH), ("nz", NSA_WIDTH), ("gl", GL_W), ("gt", GT_W),
           ("su", S5_WIDTH), ("sz", S5_WIDTH))
_B_OFF = {}
_off = 0
for _name, _w in _B_SEGS:
    _B_OFF[_name] = (_off, _w)
    _off += _w
IN_PAD_B = _off


def _relayout_w_in_b(w_in):
    d = w_in.shape[:-1]
    wq = _cols(w_in, "nq") * (HEAD_DIM ** -0.5)
    z = jnp.zeros(d + (HEAD_DIM,), w_in.dtype)
    qparts = []
    for h in range(NSA_HEADS):
        wh = wq[..., h * HEAD_DIM:(h + 1) * HEAD_DIM]
        qparts += [wh, z] if h < NSA_HPG else [z, wh]
    gt = jnp.concatenate([_cols(w_in, "ngate"), _cols(w_in, "glr")], axis=-1)
    gt = jnp.pad(gt, ((0, 0),) * len(d) + ((0, GT_W - gt.shape[-1]),))
    w = jnp.concatenate(qparts + [_cols(w_in, "nkv"), _cols(w_in, "nz"), _cols(w_in, "gq"), _cols(w_in, "gk"),
                                  _cols(w_in, "gv"), _cols(w_in, "gz"), gt, _cols(w_in, "su"), _cols(w_in, "sz")],
                        axis=-1)
    return w.astype(BF16)


def _lane_tile(x, k):
    return jnp.concatenate([x] * k, axis=1)


def _params(*sem):
    return pltpu.CompilerParams(dimension_semantics=sem, vmem_limit_bytes=VMEM_LIMIT)


def _norm_rows(x_ref, g_ref):
    x = x_ref[...]
    ms = jnp.mean(x * x, axis=-1, keepdims=True)
    return (x * lax.rsqrt(ms + RMS_EPS) * g_ref[...]).astype(BF16)


def _in_proj_a_kernel(x_ref, g_ref, w_ref, o_ref):
    o_ref[...] = jnp.dot(_norm_rows(x_ref, g_ref), w_ref[...], preferred_element_type=F32)


def _in_proj_a(x2d, g, w_bf16):
    n, d = x2d.shape
    tm = min(512, n)
    return pl.pallas_call(
        _in_proj_a_kernel,
        out_shape=jax.ShapeDtypeStruct((n, IN_PAD_A), F32),
        grid=(n // tm,),
        in_specs=[pl.BlockSpec((tm, d), lambda i: (i, 0)),
                  pl.BlockSpec((1, d), lambda i: (0, 0)),
                  pl.BlockSpec((d, IN_PAD_A), lambda i: (0, 0))],
        out_specs=pl.BlockSpec((tm, IN_PAD_A), lambda i: (i, 0)),
        compiler_params=_params("parallel"),
        name="in_proj_a",
    )(x2d, g.reshape(1, d), w_bf16)


def _in_proj_b_kernel(x_ref, g_ref, w_ref, qx_ref, kv_ref, kb_ref, vb_ref, nz_ref, gl_ref, gt_ref, su_ref, sz_ref):
    xn = _norm_rows(x_ref, g_ref)

    def seg(name):
        o, w = _B_OFF[name]
        return jnp.dot(xn, w_ref[:, o:o + w], preferred_element_type=F32)
    qx_ref[...] = seg("qx").astype(BF16)
    kv = seg("kv")
    kv_ref[...] = kv
    kvb = kv.astype(BF16)
    ones = jnp.ones((kv.shape[0], LANES), BF16)
    for br in range(3):
        kb_ref[:, br * LANES:(br + 1) * LANES] = kvb[:, 2 * br * LANES:(2 * br + 1) * LANES]
        vb_ref[:, 2 * br * LANES:(2 * br + 1) * LANES] = kvb[:, (2 * br + 1) * LANES:(2 * br + 2) * LANES]
        vb_ref[:, (2 * br + 1) * LANES:(2 * br + 2) * LANES] = ones
    nz_ref[...] = seg("nz")
    gl_ref[...] = seg("gl")
    gt_ref[...] = seg("gt")
    su_ref[...] = seg("su")
    sz_ref[...] = seg("sz")


def _in_proj_b(x2d, b, l, g, w_bf16):
    n, d = x2d.shape
    tm = min(512, l)
    nt = l // tm
    row = lambda bi, ti: (bi * nt + ti, 0)
    tcol = lambda bi, ti: (ti, bi)
    shapes = [((n, QX_W), BF16), ((n, KV_WIDTH), F32), ((n, 3 * LANES), BF16), ((n, 6 * LANES), BF16),
              ((n, NSA_WIDTH), F32), ((n, GL_W), F32), ((n, GT_W), F32),
              ((l, b * S5_WIDTH), F32), ((l, b * S5_WIDTH), F32)]
    out_specs = [pl.BlockSpec((tm, s[0][1]), row) for s in shapes[:7]] + \
                [pl.BlockSpec((tm, S5_WIDTH), tcol)] * 2
    return pl.pallas_call(
        _in_proj_b_kernel,
        out_shape=[jax.ShapeDtypeStruct(*s) for s in shapes],
        grid=(b, nt),
        in_specs=[pl.BlockSpec((tm, d), row),
                  pl.BlockSpec((1, d), lambda bi, ti: (0, 0)),
                  pl.BlockSpec((d, IN_PAD_B), lambda bi, ti: (0, 0))],
        out_specs=out_specs,
        compiler_params=_params("parallel", "parallel"),
        name="in_proj_b",
    )(x2d, g.reshape(1, d), w_bf16)


def _compress_kernel(x_ref, pe_ref, w1_ref, w2_ref, o_ref, *, n_ch):
    def half(s):
        acc = jnp.zeros((n_ch, LANES), F32)
        for p in range(CMP_STRIDE):
            xp = x_ref[pl.ds(p, n_ch, stride=CMP_STRIDE), :] + pe_ref[0, s * CMP_STRIDE + p:s * CMP_STRIDE + p + 1, :]
            acc = acc + jnp.dot(xp.astype(BF16), w1_ref[0, s * CMP_STRIDE + p], preferred_element_type=F32)
        return acc
    first = half(0)
    second = pltpu.roll(half(1), n_ch - 1, 0)
    hid = jax.nn.gelu(first + second).astype(BF16)
    o_ref[0, 0] = jnp.dot(hid, w2_ref[0], preferred_element_type=F32).astype(BF16)


def _compress_weights(pe, w1, w2):
    eye = jnp.eye(NSA_KV_HEADS, dtype=F32)
    pe2 = jnp.tile(pe, (1, 1, NSA_KV_HEADS))
    w1r = w1.reshape(2, CMP_BLOCK, HEAD_DIM, HEAD_DIM)
    w1bd = jnp.einsum('gh,kpde->kpgdhe', eye, w1r).reshape(2, CMP_BLOCK, LANES, LANES).astype(BF16)
    w2bd = jnp.einsum('gh,kde->kgdhe', eye, w2).reshape(2, LANES, LANES).astype(BF16)
    return pe2, w1bd, w2bd


def _page_copies(pt_ref, pages_ref, buf, sem, layer, bi, part, slot, n_pages):
    out = []
    for j in range(n_pages):
        src = pages_ref.at[layer, pt_ref[bi, j]]
        out.append(pltpu.make_async_copy(src if part is None else src.at[part], buf.at[slot, j], sem.at[slot]))
    return out


def _paged_compress_kernel(pt_ref, pages_ref, pe_ref, w1_ref, w2_ref, o_ref, buf, x_s, sem,
                           *, layer, n_pages, n_steps, page):
    bi, s = pl.program_id(0), pl.program_id(1)
    step = bi * 2 + s
    slot = step % 2

    @pl.when(step == 0)
    def _():
        for c in _page_copies(pt_ref, pages_ref, buf, sem, layer, bi, s, slot, n_pages):
            c.start()

    @pl.when(step + 1 < n_steps)
    def _():
        nxt = step + 1
        for c in _page_copies(pt_ref, pages_ref, buf, sem, layer, nxt // 2, nxt % 2, 1 - slot, n_pages):
            c.start()
    for c in _page_copies(pt_ref, pages_ref, buf, sem, layer, bi, s, slot, n_pages):
        c.wait()

    def to_rows(j, carry):
        x_s[pl.ds(pl.multiple_of(j * page, page), page), :] = jnp.transpose(buf[slot, j].reshape(LANES, page))
        return carry
    lax.fori_loop(0, n_pages, to_rows, 0, unroll=8)
    _compress_kernel(x_s, pe_ref, w1_ref, w2_ref, o_ref, n_ch=n_pages * page // CMP_STRIDE)


def _paged_compress(pages_t, layer, page_table, pe, w1, w2):
    b, n_pages = page_table.shape
    page = pages_t.shape[-1]
    t = n_pages * page
    n_ch = t // CMP_STRIDE
    pe2, w1bd, w2bd = _compress_weights(pe, w1, w2)
    return pl.pallas_call(
        functools.partial(_paged_compress_kernel, layer=layer, n_pages=n_pages, n_steps=2 * b, page=page),
        out_shape=jax.ShapeDtypeStruct((2, b, n_ch, LANES), BF16),
        grid_spec=pltpu.PrefetchScalarGridSpec(
            num_scalar_prefetch=1,
            grid=(b, 2),
            in_specs=[pl.BlockSpec(memory_space=pl.ANY),
                      pl.BlockSpec((1, CMP_BLOCK, LANES), lambda bi, s, pt: (s, 0, 0)),
                      pl.BlockSpec((1, CMP_BLOCK, LANES, LANES), lambda bi, s, pt: (s, 0, 0, 0)),
                      pl.BlockSpec((1, LANES, LANES), lambda bi, s, pt: (s, 0, 0))],
            out_specs=pl.BlockSpec((1, 1, n_ch, LANES), lambda bi, s, pt: (s, bi, 0, 0)),
            scratch_shapes=[pltpu.VMEM((2, n_pages, NSA_KV_HEADS, HEAD_DIM, page), F32),
                            pltpu.VMEM((t, LANES), F32), pltpu.SemaphoreType.DMA((2,))]),
        compiler_params=_params("arbitrary", "arbitrary"),
        name="nsa_paged_compress",
    )(page_table, pages_t, pe2, w1bd, w2bd)


def _compress(kv, b, t, pe, w1, w2):
    n_ch = t // CMP_STRIDE
    pe2, w1bd, w2bd = _compress_weights(pe, w1, w2)
    return pl.pallas_call(
        functools.partial(_compress_kernel, n_ch=n_ch),
        out_shape=jax.ShapeDtypeStruct((2, b, n_ch, LANES), BF16),
        grid=(b, 2),
        in_specs=[pl.BlockSpec((t, LANES), lambda bi, s: (bi, s)),
                  pl.BlockSpec((1, CMP_BLOCK, LANES), lambda bi, s: (s, 0, 0)),
                  pl.BlockSpec((1, CMP_BLOCK, LANES, LANES), lambda bi, s: (s, 0, 0, 0)),
                  pl.BlockSpec((1, LANES, LANES), lambda bi, s: (s, 0, 0))],
        out_specs=pl.BlockSpec((1, 1, n_ch, LANES), lambda bi, s: (s, bi, 0, 0)),
        compiler_params=_params("parallel", "parallel"),
        name="nsa_compress",
    )(kv, pe2, w1bd, w2bd)


def _split3(x):
    hi = x.astype(BF16)
    r = x - hi.astype(F32)
    mid = r.astype(BF16)
    lo = (r - mid.astype(F32)).astype(BF16)
    return hi, mid, lo


def _cmp_sel_kernel(qx_ref, kc_ref, vc_ref, gt_ref, m2_ref, o_ref, sel_ref, sc_s, *, tq, nc):
    q0 = pl.program_id(1) * tq
    qpos = q0 + lax.broadcasted_iota(jnp.int32, (tq, 1), 0)
    cend = lax.broadcasted_iota(jnp.int32, (1, nc), 1) * CMP_STRIDE + (CMP_BLOCK - 1)
    dist = (qpos - cend).astype(F32)
    ok = dist >= 0.0
    kc = kc_ref[0, 0]
    vc = vc_ref[0, 0]
    gts = jax.nn.sigmoid(gt_ref[...])
    imps = []
    for g in range(NSA_KV_HEADS):
        imp = jnp.zeros((tq, nc), F32)
        for j in range(NSA_HPG):
            h = g * NSA_HPG + j
            s = lax.dot_general(qx_ref[:, h * LANES:(h + 1) * LANES], kc, (((1,), (1,)), ((), ())),
                                preferred_element_type=F32)
            s = jnp.where(ok, s - ALIBI[h] * dist, MASKED)
            e = jnp.where(ok, jnp.exp(s - jnp.max(s, axis=-1, keepdims=True)), 0.0)
            p = e / jnp.maximum(jnp.sum(e, axis=-1, keepdims=True), 1e-30)
            imp = imp + p
            o = jnp.dot(p.astype(BF16), vc, preferred_element_type=F32)
            o_ref[:, h * HEAD_DIM:(h + 1) * HEAD_DIM] = (o[:, g * HEAD_DIM:(g + 1) * HEAD_DIM]
                                                         * gts[:, 3 * h:3 * h + 1])
        imps.append(imp)
    m2 = m2_ref[...]
    score = sum(jnp.dot(piece, m2, preferred_element_type=F32) for piece in _split3(jnp.concatenate(imps, axis=1)))
    lane = lax.broadcasted_iota(jnp.int32, (1, LANES), 1)
    blk = lane & (SLC_BLOCK - 1)
    cur = qpos >> 6
    forced = (blk == 0) | (blk == cur) | (blk == cur - 1)
    elig = (blk * SLC_BLOCK) <= qpos
    sc_s[...] = jnp.transpose(jnp.where(elig, jnp.where(forced, NEG_BIG, score), -NEG_BIG))
    sub = lax.broadcasted_iota(jnp.int32, (SUBLANES, 1), 0)
    kept = []
    for g in range(NSA_KV_HEADS):
        for r in range(SLC_BLOCK // SUBLANES):
            mine = sc_s[g * SLC_BLOCK + r * SUBLANES:g * SLC_BLOCK + (r + 1) * SUBLANES, :]
            cnt = jnp.zeros(mine.shape, jnp.int32)
            for i in range(SLC_BLOCK):
                cand = sc_s[g * SLC_BLOCK + i:g * SLC_BLOCK + i + 1, :]
                if i < r * SUBLANES:
                    inc = jnp.where(cand >= mine, 1, 0)
                elif i >= (r + 1) * SUBLANES:
                    inc = jnp.where(cand > mine, 1, 0)
                else:
                    tie = jnp.where(sub > i - r * SUBLANES, 1, 0)
                    inc = jnp.where(cand > mine, 1, jnp.where(cand == mine, tie, 0))
                cnt = cnt + inc
            kept.append(jnp.where((cnt < N_SELECT) & (mine > -0.5 * NEG_BIG), 1.0, 0.0))
    sel_ref[...] = jnp.transpose(jnp.concatenate(kept, axis=0)).astype(BF16)


def _slc_map_matrix(nc, n_cmp):
    n = jnp.arange(nc)[:, None]
    j = jnp.arange(SLC_BLOCK)[None, :]
    ratio = SLC_BLOCK // CMP_STRIDE
    m = ((n >= ratio * j) & (n <= ratio * j + ratio - 1)).astype(F32) + \
        ((n >= ratio * j - 1) & (n <= ratio * j + ratio - 2)).astype(F32)
    m = jnp.where(n < n_cmp, m, 0.0)
    z = jnp.zeros_like(m)
    return jnp.concatenate([jnp.concatenate([m, z], 1), jnp.concatenate([z, m], 1)], 0).astype(BF16)


def _cmp_select(qx, kvc, gt, b, l):
    nc = kvc.shape[2]
    tq = min(256, l)
    nt = l // tq
    row = lambda bi, ti: (bi * nt + ti, 0)
    return pl.pallas_call(
        functools.partial(_cmp_sel_kernel, tq=tq, nc=nc),
        out_shape=[jax.ShapeDtypeStruct((b * l, NSA_WIDTH), F32), jax.ShapeDtypeStruct((b * l, LANES), BF16)],
        grid=(b, nt),
        in_specs=[pl.BlockSpec((tq, QX_W), row),
                  pl.BlockSpec((1, 1, nc, LANES), lambda bi, ti: (0, bi, 0, 0)),
                  pl.BlockSpec((1, 1, nc, LANES), lambda bi, ti: (1, bi, 0, 0)),
                  pl.BlockSpec((tq, GT_W), row),
                  pl.BlockSpec((2 * nc, LANES), lambda bi, ti: (0, 0))],
        out_specs=[pl.BlockSpec((tq, NSA_WIDTH), row), pl.BlockSpec((tq, LANES), row)],
        scratch_shapes=[pltpu.VMEM((LANES, tq), F32)],
        compiler_params=_params("parallel", "parallel"),
        name="nsa_cmp_select",
    )(qx, kvc, kvc, gt, _slc_map_matrix(nc, nc - 1))


FLASH_PARTS = 4


def _flash_kernel(qx_ref, k_ref, kf_ref, v_ref, sel_ref, gt_ref, o_ref, qs_s, acc_s, m_s,
                  *, tq, tk, mode, gate_col):
    i = pl.program_id(1)
    q0 = i * tq
    rows = NSA_HPG * tq
    d0 = (lax.broadcasted_iota(jnp.int32, (tq, tk), 0) - lax.broadcasted_iota(jnp.int32, (tq, tk), 1)).astype(F32)
    gts = jax.nn.sigmoid(gt_ref[...])
    lane = lax.broadcasted_iota(jnp.int32, (1, LANES), 1)

    def tile(g, k0, causal, winlow):
        kt = jnp.concatenate([k_ref[pl.ds(k0, tk), :], kf_ref[pl.ds(k0, tk), :]], axis=1)
        vt = v_ref[pl.ds(k0, tk), :]
        c = (q0 - k0).astype(F32)
        bias = None
        if causal:
            bias = jnp.where(d0 + c >= 0.0, 0.0, MASKED)
        if winlow:
            bias = jnp.where(d0 + c < float(WINDOW), bias, MASKED)
        for part in range(FLASH_PARTS):
            rpp = rows // FLASH_PARTS
            rs = slice(part * rpp, (part + 1) * rpp)
            s = lax.dot_general(qs_s[g, rs, :], kt, (((1,), (1,)), ((), ())), preferred_element_type=F32)
            if bias is not None and rpp >= tq:
                s = (s.reshape(-1, tq, tk) + bias[None]).reshape(-1, tk)
            elif bias is not None:
                s = s + bias[(part * rpp) % tq:(part * rpp) % tq + rpp]
            m_prev = m_s[g, rs, :]
            m_new = jnp.maximum(m_prev, jnp.max(s, axis=1, keepdims=True))
            p = jnp.exp(s - _lane_tile(m_new, tk // LANES))
            alpha = jnp.exp(m_prev - m_new)
            acc_s[g, rs, :] = _lane_tile(alpha, 2) * acc_s[g, rs, :] + jnp.dot(p.astype(BF16), vt,
                                                                                  preferred_element_type=F32)
            m_s[g, rs, :] = m_new

    for g in range(NSA_KV_HEADS):
        if mode == "slc":
            blockbias = (sel_ref[...].astype(F32) - 1.0) * (-MASKED)
            if g == 1:
                blockbias = pltpu.roll(blockbias, SLC_BLOCK, 1)
        else:
            blockbias = jnp.zeros((tq, LANES), F32)
        for jh in range(NSA_HPG):
            h = g * NSA_HPG + jh
            feat = jnp.where(lane < SLC_BLOCK, blockbias,
                             jnp.where(lane == SLC_BLOCK, ALIBI[h] * SLC_BLOCK,
                                       jnp.where(lane == SLC_BLOCK + 1, ALIBI[h], 0.0)))
            qs_s[g, jh * tq:(jh + 1) * tq, 0:LANES] = qx_ref[:, h * LANES:(h + 1) * LANES]
            qs_s[g, jh * tq:(jh + 1) * tq, LANES:2 * LANES] = feat.astype(BF16)
        m_s[g] = jnp.full(m_s.shape[1:], MASKED, F32)
        acc_s[g] = jnp.zeros(acc_s.shape[1:], F32)
        if mode == "slc":
            last = (q0 + tq - 1) // tk

            def body(j, carry):
                tile(g, pl.multiple_of(j * tk, tk), False, False)
                return carry
            lax.fori_loop(0, last, body, 0)
            tile(g, pl.multiple_of(last * tk, tk), True, False)
        else:
            tile(g, pl.multiple_of(jnp.maximum(i - WINDOW // tq, 0) * tq, tq), True, True)
        for jh in range(NSA_HPG):
            h = g * NSA_HPG + jh
            c = 3 * h + gate_col
            a = acc_s[g, jh * tq:(jh + 1) * tq, :]
            o = a[:, g * HEAD_DIM:(g + 1) * HEAD_DIM] / a[:, LANES + g * HEAD_DIM:LANES + (g + 1) * HEAD_DIM]
            o_ref[:, h * HEAD_DIM:(h + 1) * HEAD_DIM] = o * gts[:, c:c + 1]


def _key_position_features(l):
    assert l <= SLC_BLOCK * SLC_BLOCK
    pos = jnp.arange(l)[:, None]
    lane = jnp.arange(LANES)[None, :]
    f = jnp.where(lane < SLC_BLOCK, (lane == pos // SLC_BLOCK).astype(F32),
                  jnp.where(lane == SLC_BLOCK, (pos // SLC_BLOCK).astype(F32),
                            jnp.where(lane == SLC_BLOCK + 1, (pos % SLC_BLOCK).astype(F32), 0.0)))
    return f.astype(BF16)


def _flash(qx, kb, vb, sel, gt, b, l, mode):
    tq = min(256, l)
    tk = 2 * tq if mode == "slc" else WINDOW + tq
    assert l % tk == 0 if mode == "slc" else (l >= tk and WINDOW % tq == 0)
    nt = l // tq
    rows = NSA_HPG * tq
    row = lambda bi, ti: (bi * nt + ti, 0)
    branch = 1 if mode == "slc" else 2
    return pl.pallas_call(
        functools.partial(_flash_kernel, tq=tq, tk=tk, mode=mode, gate_col=branch),
        out_shape=jax.ShapeDtypeStruct((b * l, NSA_WIDTH), F32),
        grid=(b, nt),
        in_specs=[pl.BlockSpec((tq, QX_W), row),
                  pl.BlockSpec((l, LANES), lambda bi, ti: (bi, branch)),
                  pl.BlockSpec((l, LANES), lambda bi, ti: (0, 0)),
                  pl.BlockSpec((l, 2 * LANES), lambda bi, ti: (bi, branch)),
                  pl.BlockSpec((tq, LANES), row),
                  pl.BlockSpec((tq, GT_W), row)],
        out_specs=pl.BlockSpec((tq, NSA_WIDTH), row),
        scratch_shapes=[pltpu.VMEM((NSA_KV_HEADS, rows, 2 * LANES), BF16),
                        pltpu.VMEM((NSA_KV_HEADS, rows, 2 * LANES), F32),
                        pltpu.VMEM((NSA_KV_HEADS, rows, LANES), F32)],
        compiler_params=_params("parallel", "parallel"),
        name="nsa_" + mode,
    )(qx, kb, _key_position_features(l), vb, sel, gt)


def _decode_nsa_kernel(pt_ref, slc_ref, win_ref, kc_ref, vc_ref, q_ref, kvn_ref, gt_ref, m2_ref,
                       oc_ref, os_ref, ow_ref, buf, sem, *, layer, n_pages, page, nb, l, n_slc, nsp):
    bi = pl.program_id(0)
    slot = bi % 2
    pos0 = n_pages * page
    nc = kc_ref.shape[2]
    rows = NSA_HPG * l

    @pl.when(bi == 0)
    def _():
        for c in _page_copies(pt_ref, slc_ref, buf, sem, layer, bi, None, slot, n_pages):
            c.start()

    @pl.when(bi + 1 < nb)
    def _():
        for c in _page_copies(pt_ref, slc_ref, buf, sem, layer, bi + 1, None, 1 - slot, n_pages):
            c.start()

    qf = q_ref[...]
    kvn = kvn_ref[...]
    gts = jax.nn.sigmoid(gt_ref[...])
    t_col = lax.broadcasted_iota(jnp.int32, (rows, 1), 0) % l
    t_f = t_col.astype(F32)
    lane = lax.broadcasted_iota(jnp.int32, (1, LANES), 1)
    lane_f = lane.astype(F32)

    def pad_rows(x):
        return jnp.concatenate([x, jnp.zeros((LANES - l, x.shape[1]), x.dtype)], axis=0)

    def group_q(g):
        q = jnp.concatenate([qf[:, (g * NSA_HPG + j) * LANES:(g * NSA_HPG + j + 1) * LANES]
                             for j in range(NSA_HPG)], axis=0).astype(BF16)
        sl = jnp.concatenate([jnp.full((l, 1), ALIBI[g * NSA_HPG + j], F32) for j in range(NSA_HPG)], axis=0)
        return q, sl

    def new_keys(q, sl, branch):
        kn = pad_rows(kvn[:, 2 * branch * LANES:(2 * branch + 1) * LANES]).astype(BF16)
        vn = pad_rows(kvn[:, (2 * branch + 1) * LANES:(2 * branch + 2) * LANES]).astype(BF16)
        s = lax.dot_general(q, kn, (((1,), (1,)), ((), ())), preferred_element_type=F32)
        s = jnp.where(lane <= t_col, s - sl * (t_f - lane_f), MASKED)
        return s, vn

    def write(o_ref, o, den, g, col):
        for j in range(NSA_HPG):
            h = g * NSA_HPG + j
            c = 3 * h + col
            oh = o[j * l:(j + 1) * l, g * HEAD_DIM:(g + 1) * HEAD_DIM] / den[j * l:(j + 1) * l]
            o_ref[:, h * HEAD_DIM:(h + 1) * HEAD_DIM] = oh * gts[:, c:c + 1]

    cend = (lax.broadcasted_iota(jnp.int32, (1, nc), 1) * CMP_STRIDE + (CMP_BLOCK - 1)).astype(F32)
    imps = []
    qs = []
    for g in range(NSA_KV_HEADS):
        q, sl = group_q(g)
        qs.append((q, sl))
        dist = (t_f + float(pos0)) - cend
        ok = dist >= 0.0
        s = lax.dot_general(q, kc_ref[0, 0], (((1,), (1,)), ((), ())), preferred_element_type=F32)
        s = jnp.where(ok, s - sl * dist, MASKED)
        e = jnp.where(ok, jnp.exp(s - jnp.max(s, axis=-1, keepdims=True)), 0.0)
        p = e / jnp.maximum(jnp.sum(e, axis=-1, keepdims=True), 1e-30)
        write(oc_ref, jnp.dot(p.astype(BF16), vc_ref[0, 0], preferred_element_type=F32),
              jnp.ones((rows, 1), F32), g, 0)
        imps.append(sum(p[j * l:(j + 1) * l] for j in range(NSA_HPG)))
    m2 = m2_ref[...]
    score = sum(jnp.dot(piece, m2, preferred_element_type=F32) for piece in _split3(jnp.concatenate(imps, axis=1)))
    lane2 = lax.broadcasted_iota(jnp.int32, (1, NSA_KV_HEADS * nsp), 1)
    blk = lane2 & (nsp - 1)
    qpos = pos0 + lax.broadcasted_iota(jnp.int32, (l, 1), 0)
    cur = qpos // SLC_BLOCK
    forced = (blk == 0) | (blk == cur) | (blk == cur - 1)
    elig = ((blk * SLC_BLOCK) <= qpos) & (blk < n_slc)
    sc = jnp.where(elig, jnp.where(forced, NEG_BIG, score), -NEG_BIG)
    cnt = jnp.zeros(sc.shape, jnp.int32)
    for i in range(n_slc):
        col = jnp.where(lane2 < nsp, sc[:, i:i + 1], sc[:, nsp + i:nsp + i + 1])
        cnt = cnt + jnp.where(col > sc, 1, jnp.where(col == sc, jnp.where(blk > i, 1, 0), 0))
    sel = jnp.where((cnt < N_SELECT) & (sc > -0.5 * NEG_BIG), 1.0, 0.0)

    w = win_ref.shape[-1]
    wpos = lax.broadcasted_iota(jnp.int32, (1, w), 1)
    for g in range(NSA_KV_HEADS):
        q, sl = qs[g]
        kt = win_ref[0, 0, 0].reshape(LANES, w).astype(BF16)
        vt = win_ref[0, 0, 1].reshape(LANES, w).astype(BF16)
        dist = (t_f + float(w)) - wpos.astype(F32)
        s = jnp.dot(q, kt, preferred_element_type=F32)
        s = jnp.where(dist < float(WINDOW), s - sl * dist, MASKED)
        sn, vn = new_keys(q, sl, 2)
        m = jnp.maximum(jnp.max(s, axis=-1, keepdims=True), jnp.max(sn, axis=-1, keepdims=True))
        p, pn = jnp.exp(s - m), jnp.exp(sn - m)
        o = (lax.dot_general(p.astype(BF16), vt, (((1,), (1,)), ((), ())), preferred_element_type=F32)
             + jnp.dot(pn.astype(BF16), vn, preferred_element_type=F32))
        write(ow_ref, o, jnp.sum(p, axis=-1, keepdims=True) + jnp.sum(pn, axis=-1, keepdims=True), g, 2)

    for c in _page_copies(pt_ref, slc_ref, buf, sem, layer, bi, None, slot, n_pages):
        c.wait()
    half = page // SLC_BLOCK
    for g in range(NSA_KV_HEADS):
        q, sl = qs[g]
        ss = []
        for j in range(n_pages):
            kt = buf[slot, j, 0].reshape(LANES, page).astype(BF16)
            s = jnp.dot(q, kt, preferred_element_type=F32)
            keep = sel[:, g * nsp + half * j:g * nsp + half * j + 1]
            for hb in range(1, half):
                keep = jnp.where(lane < hb * SLC_BLOCK, keep, sel[:, g * nsp + half * j + hb:g * nsp + half * j + hb + 1])
            keep = jnp.concatenate([keep] * NSA_HPG, axis=0) > 0.5
            dist = (t_f + float(pos0 - j * page)) - lane_f
            ss.append(jnp.where(keep, s - sl * dist, MASKED))
        sn, vn = new_keys(q, sl, 1)
        m = jnp.max(sn, axis=-1, keepdims=True)
        for s in ss:
            m = jnp.maximum(m, jnp.max(s, axis=-1, keepdims=True))
        pn = jnp.exp(sn - m)
        den = jnp.sum(pn, axis=-1, keepdims=True)
        o = jnp.dot(pn.astype(BF16), vn, preferred_element_type=F32)
        for j in range(n_pages):
            p = jnp.exp(ss[j] - m)
            den = den + jnp.sum(p, axis=-1, keepdims=True)
            vt = buf[slot, j, 1].reshape(LANES, page).astype(BF16)
            o = o + lax.dot_general(p.astype(BF16), vt, (((1,), (1,)), ((), ())), preferred_element_type=F32)
        write(os_ref, o, den, g, 1)


def _decode_nsa(slc_t, win_t, layer, page_table, kvc, qf, kvn, gt, b, l):
    n_pages, page = page_table.shape[1], slc_t.shape[-1]
    pos0 = n_pages * page
    nc = kvc.shape[2]
    n_slc = -(-(pos0 + l) // SLC_BLOCK)
    nsp = 1 << (n_slc - 1).bit_length()
    assert pos0 % SLC_BLOCK == 0 and l <= SLC_BLOCK and page % SLC_BLOCK == 0 and l % 8 == 0
    n = jnp.arange(nc)[:, None]
    jj = jnp.arange(nsp)[None, :]
    ratio = SLC_BLOCK // CMP_STRIDE
    m = ((n >= ratio * jj) & (n <= ratio * jj + ratio - 1)).astype(F32) + \
        ((n >= ratio * jj - 1) & (n <= ratio * jj + ratio - 2)).astype(F32)
    m = jnp.where((n < nc - 1) & (jj < n_slc), m, 0.0)
    z = jnp.zeros_like(m)
    m2 = jnp.concatenate([jnp.concatenate([m, z], 1), jnp.concatenate([z, m], 1)], 0).astype(BF16)
    row = lambda bi, pt: (bi, 0)
    w = win_t.shape[-1]
    out = jax.ShapeDtypeStruct((b * l, NSA_WIDTH), F32)
    return pl.pallas_call(
        functools.partial(_decode_nsa_kernel, layer=layer, n_pages=n_pages, page=page, nb=b, l=l, n_slc=n_slc,
                          nsp=nsp),
        out_shape=[out, out, out],
        grid_spec=pltpu.PrefetchScalarGridSpec(
            num_scalar_prefetch=1,
            grid=(b,),
            in_specs=[pl.BlockSpec(memory_space=pl.ANY),
                      pl.BlockSpec((1, 1, 2, NSA_KV_HEADS, HEAD_DIM, w), lambda bi, pt: (layer, bi, 0, 0, 0, 0)),
                      pl.BlockSpec((1, 1, nc, LANES), lambda bi, pt: (0, bi, 0, 0)),
                      pl.BlockSpec((1, 1, nc, LANES), lambda bi, pt: (1, bi, 0, 0)),
                      pl.BlockSpec((l, QX_W), row), pl.BlockSpec((l, KV_WIDTH), row), pl.BlockSpec((l, GT_W), row),
                      pl.BlockSpec((NSA_KV_HEADS * nc, NSA_KV_HEADS * nsp), lambda bi, pt: (0, 0))],
            out_specs=[pl.BlockSpec((l, NSA_WIDTH), row)] * 3,
            scratch_shapes=[pltpu.VMEM((2, n_pages, 2, NSA_KV_HEADS, HEAD_DIM, page), F32),
                            pltpu.SemaphoreType.DMA((2,))]),
        compiler_params=_params("arbitrary"),
        name="nsa_decode",
    )(page_table, slc_t, win_t, kvc, kvc, qf, kvn, gt, m2)


def _s5_kernel(u_ref, z_ref, bre_ref, bim_ref, cre_ref, cim_ref, ab_ref, d_ref, wglu_ref, h0_ref,
               o_ref, hout_ref, xr_s, xi_s, h_s, *, nb, tc):
    @pl.when(pl.program_id(0) == 0)
    def _():
        h_s[...] = h0_ref[...]
    bre, bim = bre_ref[...], bim_ref[...]
    for b in range(nb):
        ub = u_ref[:, b * S5_WIDTH:(b + 1) * S5_WIDTH].astype(BF16)
        xr_s[:, b * S5_CH:(b + 1) * S5_CH] = jnp.dot(ub, bre, preferred_element_type=F32)
        xi_s[:, b * S5_CH:(b + 1) * S5_CH] = jnp.dot(ub, bim, preferred_element_type=F32)
    abr = ab_ref[0:1, :]
    abi = ab_ref[1:2, :]

    def step(t, carry):
        hr, hi = carry
        nr = abr * hr - abi * hi + xr_s[pl.ds(t, 1), :]
        ni = abr * hi + abi * hr + xi_s[pl.ds(t, 1), :]
        xr_s[pl.ds(t, 1), :] = nr
        xi_s[pl.ds(t, 1), :] = ni
        return nr, ni
    hr, hi = lax.fori_loop(0, tc, step, (h_s[0:1, :], h_s[1:2, :]), unroll=8)
    h_s[0:1, :] = hr
    h_s[1:2, :] = hi
    hout_ref[...] = h_s[...]
    cre, cim, wglu = cre_ref[...], cim_ref[...], wglu_ref[...]
    for b in range(nb):
        u = u_ref[:, b * S5_WIDTH:(b + 1) * S5_WIDTH]
        y = (jnp.dot(xr_s[:, b * S5_CH:(b + 1) * S5_CH].astype(BF16), cre, preferred_element_type=F32)
             - jnp.dot(xi_s[:, b * S5_CH:(b + 1) * S5_CH].astype(BF16), cim, preferred_element_type=F32)
             + d_ref[...] * u)
        gl = jnp.dot(jax.nn.gelu(y).astype(BF16), wglu, preferred_element_type=F32)
        z = z_ref[:, b * S5_WIDTH:(b + 1) * S5_WIDTH]
        o_ref[:, b * S5_WIDTH:(b + 1) * S5_WIDTH] = (gl[:, :S5_WIDTH] * jax.nn.sigmoid(gl[:, S5_WIDTH:])
                                                       * (z * jax.nn.sigmoid(z)))


def _s5_discretise(lw):
    a_re, a_im = lw['s5_a_re'].astype(F32), lw['s5_a_im'].astype(F32)
    b_re, b_im = lw['s5_b_re'].astype(F32), lw['s5_b_im'].astype(F32)
    dt = jnp.exp(lw['s5_log_dt'].astype(F32))[:, None]
    mag = jnp.exp(a_re * dt)
    abr, abi = mag * jnp.cos(a_im * dt), mag * jnp.sin(a_im * dt)
    den = a_re * a_re + a_im * a_im
    nr = abr - 1.0
    cr = (nr * a_re + abi * a_im) / den
    ci = (abi * a_re - nr * a_im) / den
    bbr = cr[..., None] * b_re - ci[..., None] * b_im
    bbi = cr[..., None] * b_im + ci[..., None] * b_re
    eye = jnp.eye(S5_GROUPS, dtype=F32)
    bd_in = lambda m: jnp.einsum('gh,gpc->gchp', eye, m).reshape(S5_WIDTH, S5_CH).astype(BF16)
    bd_out = lambda m: jnp.einsum('gh,gcp->gphc', eye, m.astype(F32)).reshape(S5_CH, S5_WIDTH).astype(BF16)
    ab = jnp.stack([abr.reshape(S5_CH), abi.reshape(S5_CH)])
    return bd_in(bbr), bd_in(bbi), bd_out(lw['s5_c_re']), bd_out(lw['s5_c_im']), ab


def _s5(su, sz, lw, h0, nb, l):
    bre, bim, cre, cim, ab = _s5_discretise(lw)
    tc = min(256, l)
    const = lambda t: (0, 0)
    return pl.pallas_call(
        functools.partial(_s5_kernel, nb=nb, tc=tc),
        out_shape=[jax.ShapeDtypeStruct((l, nb * S5_WIDTH), F32), jax.ShapeDtypeStruct((2, nb * S5_CH), F32)],
        grid=(l // tc,),
        in_specs=[pl.BlockSpec((tc, nb * S5_WIDTH), lambda t: (t, 0)),
                  pl.BlockSpec((tc, nb * S5_WIDTH), lambda t: (t, 0)),
                  pl.BlockSpec((S5_WIDTH, S5_CH), const), pl.BlockSpec((S5_WIDTH, S5_CH), const),
                  pl.BlockSpec((S5_CH, S5_WIDTH), const), pl.BlockSpec((S5_CH, S5_WIDTH), const),
                  pl.BlockSpec((2, nb * S5_CH), const), pl.BlockSpec((1, S5_WIDTH), const),
                  pl.BlockSpec((S5_WIDTH, 2 * S5_WIDTH), const), pl.BlockSpec((2, nb * S5_CH), const)],
        out_specs=[pl.BlockSpec((tc, nb * S5_WIDTH), lambda t: (t, 0)), pl.BlockSpec((2, nb * S5_CH), const)],
        scratch_shapes=[pltpu.VMEM((tc, nb * S5_CH), F32), pltpu.VMEM((tc, nb * S5_CH), F32),
                        pltpu.VMEM((2, nb * S5_CH), F32)],
        compiler_params=_params("arbitrary"),
        name="s5",
    )(su, sz, bre, bim, cre, cim, jnp.tile(ab, (1, nb)), lw['s5_d'].astype(F32).reshape(1, S5_WIDTH),
      lw['s5_w_glu'].astype(BF16), h0)


GLA_QK = GLA_HEADS * GLA_DK


def _gla_kernel(gl_ref, gt_ref, wg_ref, bg_ref, gn_ref, pall_ref, lm_ref, seg_ref, s0_ref, o_ref, sout_ref, st_s,
                *, r, t, nl):
    nseq = r // t

    @pl.when(pl.program_id(1) == 0)
    def _():
        st_s[...] = s0_ref[...]
    q = gl_ref[:, 0:GLA_QK] * (GLA_DK ** -0.5)
    k = gl_ref[:, GLA_QK:2 * GLA_QK]
    v = gl_ref[:, 2 * GLA_QK:2 * GLA_QK + GLA_WIDTH].astype(BF16)
    gz = gl_ref[:, 2 * GLA_QK + GLA_WIDTH:]
    lg = jnp.dot(gt_ref[...].astype(BF16), wg_ref[...], preferred_element_type=F32) + bg_ref[...]
    la = (jnp.minimum(lg, 0.0) - jnp.log(1.0 + jnp.exp(-jnp.abs(lg)))) * (1.0 / GLA_GATE_TAU)
    pall = pall_ref[...]
    cums = sum(jnp.dot(pall, piece, preferred_element_type=F32) for piece in _split3(la))
    cum = cums[0:r]
    cl = cums[(nl + 1) * r:(nl + 2) * r]
    lane_qk = lax.broadcasted_iota(jnp.int32, (1, GLA_QK), 1) // GLA_DK
    lane_v = lax.broadcasted_iota(jnp.int32, (1, GLA_WIDTH), 1) // GLA_DV

    def stack(x):
        return jnp.concatenate([jnp.where(lane_qk == h, x, 0.0) for h in range(GLA_HEADS)], axis=0).astype(BF16)

    def level(l, qf, kf):
        s = lax.dot_general(stack(qf), kf.astype(BF16), (((1,), (1,)), ((), ())), preferred_element_type=F32)
        return (s.reshape(GLA_HEADS, r, r) * lm_ref[l][None]).reshape(GLA_HEADS * r, r)
    att = level(0, q, k)
    for l in range(1, nl + 1):
        cm = cums[l * r:(l + 1) * r]
        att = att + level(l, q * jnp.exp(jnp.minimum(cum - cm, 0.0)), k * jnp.exp(jnp.minimum(cm - cum, 0.0)))
    res = jnp.dot(att.astype(BF16), v, preferred_element_type=F32)
    o = sum(jnp.where(lane_v == h, res[h * r:(h + 1) * r], 0.0) for h in range(GLA_HEADS))
    qd = (q * jnp.exp(cum)).astype(BF16)
    kd = k * jnp.exp(cl - cum)
    bd = (lax.broadcasted_iota(jnp.int32, (GLA_QK, GLA_WIDTH), 0) // GLA_DK
          == lax.broadcasted_iota(jnp.int32, (GLA_QK, GLA_WIDTH), 1) // GLA_DV)
    row_seq = lax.broadcasted_iota(jnp.int32, (r, 1), 0) // t
    inter = []
    for i in range(nseq):
        st = st_s[i]
        inter.append(jnp.dot(qd[i * t:(i + 1) * t], st.astype(BF16), preferred_element_type=F32))
        kdi = kd if nseq == 1 else jnp.where(row_seq == i, kd, 0.0)
        upd = jnp.dot(jnp.transpose(kdi).astype(BF16), v, preferred_element_type=F32)
        a_col = jnp.transpose(jnp.broadcast_to(jnp.exp(cl[i * t:i * t + 1, :]), (GLA_QK, GLA_QK)))
        st_s[i] = jnp.where(bd, _lane_tile(a_col, GLA_WIDTH // GLA_QK) * st + upd, 0.0)
    o = o + (inter[0] if nseq == 1 else jnp.concatenate(inter, axis=0))
    sout_ref[...] = st_s[...]
    hi = (o * o).astype(BF16)
    lo = (o * o - hi.astype(F32)).astype(BF16)
    ms = (jnp.dot(hi, seg_ref[...], preferred_element_type=F32) + jnp.dot(lo, seg_ref[...], preferred_element_type=F32))
    o_ref[...] = o * lax.rsqrt(ms + RMS_EPS) * gn_ref[...] * (gz * jax.nn.sigmoid(gz))


def _gla_tables(r, t, nl):
    idx = jnp.arange(r)
    seq = idx // t
    rows = [(seq[:, None] == seq[None, :]) & (idx[None, :] <= idx[:, None])]
    masks = [idx[:, None] == idx[None, :]]
    for l in range(1, nl + 1):
        mid = (idx >> l << l) + (1 << (l - 1)) - 1
        rows.append((seq[:, None] == seq[None, :]) & (idx[None, :] <= mid[:, None]))
        masks.append(((idx[:, None] >> l) == (idx[None, :] >> l)) & (((idx[:, None] >> (l - 1)) & 1) == 1)
                     & (((idx[None, :] >> (l - 1)) & 1) == 0))
    rows.append(seq[:, None] == seq[None, :])
    return jnp.concatenate(rows, 0).astype(BF16), jnp.stack(masks).astype(F32)


def _gla(gl, gt, lw, s0, nb, nchunk, r, t):
    n = gl.shape[0]
    nseq = r // t
    nl = t.bit_length() - 1
    pall, lm = _gla_tables(r, t, nl)
    wg = jnp.zeros((GT_W, GLA_QK), F32).at[3 * NSA_HEADS:3 * NSA_HEADS + GLA_GATE_RANK].set(lw['gla_wg']).astype(BF16)
    eye = jnp.eye(GLA_HEADS, dtype=F32)
    seg = (jnp.einsum('gh,de->gdhe', eye, jnp.ones((GLA_DV, GLA_DV), F32)) / GLA_DV).reshape(GLA_WIDTH, GLA_WIDTH)
    s0bd = jnp.einsum('gh,bgde->bgdhe', eye, s0.astype(F32)).reshape(-1, GLA_QK, GLA_WIDTH)
    row = lambda bi, ci: (bi * nchunk + ci, 0)
    const2 = lambda bi, ci: (0, 0)
    o, sfin = pl.pallas_call(
        functools.partial(_gla_kernel, r=r, t=t, nl=nl),
        out_shape=[jax.ShapeDtypeStruct((n, GLA_WIDTH), F32),
                   jax.ShapeDtypeStruct((nb * nseq, GLA_QK, GLA_WIDTH), F32)],
        grid=(nb, nchunk),
        in_specs=[pl.BlockSpec((r, GL_W), row), pl.BlockSpec((r, GT_W), row),
                  pl.BlockSpec((GT_W, GLA_QK), const2), pl.BlockSpec((1, GLA_QK), const2),
                  pl.BlockSpec((1, GLA_WIDTH), const2), pl.BlockSpec(((nl + 2) * r, r), const2),
                  pl.BlockSpec((nl + 1, r, r), lambda bi, ci: (0, 0, 0)),
                  pl.BlockSpec((GLA_WIDTH, GLA_WIDTH), const2),
                  pl.BlockSpec((nseq, GLA_QK, GLA_WIDTH), lambda bi, ci: (bi, 0, 0))],
        out_specs=[pl.BlockSpec((r, GLA_WIDTH), row),
                   pl.BlockSpec((nseq, GLA_QK, GLA_WIDTH), lambda bi, ci: (bi, 0, 0))],
        scratch_shapes=[pltpu.VMEM((nseq, GLA_QK, GLA_WIDTH), F32)],
        compiler_params=_params("parallel", "arbitrary"),
        name="gla",
    )(gl, gt, wg, lw['gla_bg'].astype(F32).reshape(1, GLA_QK),
      jnp.tile(lw['gla_norm'].astype(F32), GLA_HEADS).reshape(1, GLA_WIDTH), pall, lm, seg.astype(BF16), s0bd)
    sf = sfin.reshape(-1, GLA_HEADS, GLA_DK, GLA_HEADS, GLA_DV)
    state = jnp.stack([sf[:, h, :, h, :] for h in range(GLA_HEADS)], axis=1)
    return o, state


def _merge_kernel(h_ref, oc_ref, os_ref, ow_ref, nz_ref, og_ref, o5_ref, ple_ref, wo_ref, pn_ref, wgate_ref, wproj_ref,
                  fn_ref, out_ref, *, final):
    nz = nz_ref[...]
    o_nsa = (oc_ref[...] + os_ref[...] + ow_ref[...]) * (nz * jax.nn.sigmoid(nz))
    mix = jnp.concatenate([o_nsa, og_ref[...], o5_ref[...]], axis=1).astype(BF16)
    h = h_ref[...] + jnp.dot(mix, wo_ref[...], preferred_element_type=F32)

    def norm(x, g_ref):
        return x * lax.rsqrt(jnp.mean(x * x, axis=-1, keepdims=True) + RMS_EPS) * g_ref[...]
    gate = jax.nn.sigmoid(jnp.dot(norm(h, pn_ref).astype(BF16), wgate_ref[...], preferred_element_type=F32))
    h = h + gate * jnp.dot(ple_ref[...].astype(BF16), wproj_ref[...], preferred_element_type=F32)
    out_ref[...] = norm(h, fn_ref) if final else h


def _merge_call(h2d, o_cmp, o_slc, o_win, nz, o_gla, o_s5, s5_time_major, ple2d, lw, final_norm, nb):
    n, d = h2d.shape
    l = n // nb
    tm = min(512, l)
    nt = l // tm
    row = lambda bi, ti: (bi * nt + ti, 0)
    const = lambda bi, ti: (0, 0)
    o5_spec = pl.BlockSpec((tm, S5_WIDTH), (lambda bi, ti: (ti, bi)) if s5_time_major else row)
    fn = jnp.ones((d,), F32) if final_norm is None else final_norm
    return pl.pallas_call(
        functools.partial(_merge_kernel, final=final_norm is not None),
        out_shape=jax.ShapeDtypeStruct((n, d), F32),
        grid=(nb, nt),
        in_specs=[pl.BlockSpec((tm, d), row)] + [pl.BlockSpec((tm, NSA_WIDTH), row)] * 4 +
                 [pl.BlockSpec((tm, GLA_WIDTH), row), o5_spec, pl.BlockSpec((tm, PLE_DIM), row),
                  pl.BlockSpec((MIX_WIDTH, d), const), pl.BlockSpec((1, d), const), pl.BlockSpec((d, d), const),
                  pl.BlockSpec((PLE_DIM, d), const), pl.BlockSpec((1, d), const)],
        out_specs=pl.BlockSpec((tm, d), row),
        compiler_params=_params("parallel", "parallel"),
        name="merge",
    )(h2d, o_cmp, o_slc, o_win, nz, o_gla, o_s5, ple2d, lw['w_out'].astype(BF16),
      lw['ple_norm'].astype(F32).reshape(1, d), lw['ple_w_gate'].astype(BF16), lw['ple_w_proj'].astype(BF16),
      fn.astype(F32).reshape(1, d))


def _rmsnorm(x, g):
    xf = x.astype(jnp.float32)
    y = xf * lax.rsqrt(jnp.mean(xf * xf, axis=-1, keepdims=True) + RMS_EPS)
    return (y * g.astype(jnp.float32)).astype(x.dtype)


def _masked_softmax(s, mask):
    s = jnp.where(mask, s, -1e30)
    e = jnp.where(mask, jnp.exp(s - jnp.max(s, axis=-1, keepdims=True)), 0.0)
    return e / jnp.maximum(jnp.sum(e, axis=-1, keepdims=True), 1e-30)


def _alibi_slopes():
    return jnp.exp2(-8.0 * jnp.arange(1, NSA_HEADS + 1, dtype=jnp.float32) / NSA_HEADS)


def _compress_kv(x, pe, w1, w2):
    b, t, g, d = x.shape
    ratio = CMP_BLOCK // CMP_STRIDE
    n_ch = t // CMP_STRIDE
    n_cmp = n_ch - ratio + 1
    ch = x[:, :n_ch * CMP_STRIDE].reshape(b, n_ch, CMP_STRIDE, g, d)
    blocks = jnp.concatenate([ch[:, s:s + n_cmp] for s in range(ratio)], axis=2)
    blocks = blocks + pe[None, None, :, None, :]
    flat = jnp.moveaxis(blocks, 2, 3).reshape(b, n_cmp, g, CMP_BLOCK * d)
    return jax.nn.gelu(flat @ w1) @ w2


def _nsa_sparse_block(q, qpos, kc, vc, cend, ksb, vsb, slopes):
    f32 = jnp.float32
    b, qb = q.shape[:2]
    sl = slopes.reshape(NSA_KV_HEADS, NSA_HPG)
    scale = HEAD_DIM ** -0.5
    s = jnp.einsum('bqghd,bngd->bqghn', q, kc).astype(f32) * scale
    dist = (qpos[:, None] - cend[None, :]).astype(f32)
    s = s - sl[:, :, None] * dist[:, None, None, :]
    p = _masked_softmax(s, (cend[None, :] <= qpos[:, None])[:, None, None, :])
    o_cmp = jnp.einsum('bqghn,bngd->bqghd', p.astype(vc.dtype), vc)
    imp = p.sum(axis=3)
    n_cmp = imp.shape[-1]
    n_slc = ksb.shape[2]
    run = imp
    for sft in range(1, CMP_BLOCK // CMP_STRIDE):
        run = run + jnp.pad(imp, ((0, 0), (0, 0), (0, 0), (sft, 0)))[..., :n_cmp]
    ratio = SLC_BLOCK // CMP_STRIDE
    run = jnp.pad(run, ((0, 0), (0, 0), (0, 0), (0, ratio * n_slc - n_cmp)))
    score = run.reshape(b, qb, NSA_KV_HEADS, n_slc, ratio).sum(-1)
    blk = jnp.arange(n_slc)
    cur = qpos // SLC_BLOCK
    eligible = blk[None, :] * SLC_BLOCK <= qpos[:, None]
    forced = (blk[None, :] == 0) | (blk[None, :] == cur[:, None]) | (blk[None, :] == cur[:, None] - 1)
    score = jnp.where(forced[:, None, :], NEG_BIG, score)
    score = jnp.where(eligible[:, None, :], score, -NEG_BIG)
    top_val, top_idx = lax.top_k(score, min(N_SELECT, n_slc))
    sel_ok = top_val > -0.5 * NEG_BIG
    b_ix = jnp.arange(b)[:, None, None, None]
    g_ix = jnp.arange(NSA_KV_HEADS)[None, None, :, None]
    ksel = ksb[b_ix, g_ix, top_idx]
    vsel = vsb[b_ix, g_ix, top_idx]
    kpos = top_idx[..., None] * SLC_BLOCK + jnp.arange(SLC_BLOCK)
    s2 = jnp.einsum('bqghd,bqgkjd->bqghkj', q, ksel).astype(f32) * scale
    dist2 = (qpos[None, :, None, None, None] - kpos).astype(f32)
    s2 = s2 - sl[None, None, :, :, None, None] * dist2[:, :, :, None]
    ok2 = (sel_ok[..., None] & (dist2 >= 0))[:, :, :, None]
    shp = s2.shape
    nk = shp[4] * shp[5]
    p2 = _masked_softmax(s2.reshape(shp[:4] + (nk,)), ok2.reshape(ok2.shape[:4] + (nk,))).reshape(shp)
    o_slc = jnp.einsum('bqghkj,bqgkjd->bqghd', p2.astype(vsel.dtype), vsel)
    return o_cmp, o_slc


def _window_attn(q, qpos, k, v, kpos, slopes):
    sl = slopes.reshape(NSA_KV_HEADS, NSA_HPG)
    s = jnp.einsum('bqghd,bkgd->bqghk', q, k).astype(jnp.float32) * HEAD_DIM ** -0.5
    dist = (qpos[:, None] - kpos[None, :]).astype(jnp.float32)
    s = s - sl[:, :, None] * dist[:, None, None, :]
    ok = ((dist >= 0) & (dist < WINDOW) & (kpos[None, :] >= 0))[:, None, None, :]
    p = _masked_softmax(s, ok)
    return jnp.einsum('bqghk,bkgd->bqghd', p.astype(v.dtype), v)


def _gla_chunked(q, k, v, log_a, s0):
    f32 = jnp.float32
    b, l = q.shape[:2]
    c = min(GLA_CHUNK, l)
    pad = (-l) % c
    n = (l + pad) // c

    def prep(t):
        t = jnp.pad(t.astype(f32), ((0, 0), (0, pad), (0, 0), (0, 0)))
        return t.reshape(b, n, c, GLA_HEADS, t.shape[-1])
    q = prep(q) * GLA_DK ** -0.5
    k, v, log_a = prep(k), prep(v), prep(log_a)
    cum = jnp.cumsum(log_a, axis=2)
    causal = jnp.tril(jnp.ones((c, c), dtype=bool))[:, :, None, None]
    diff = cum[:, :, :, None] - cum[:, :, None, :]
    decay = jnp.where(causal, jnp.exp(jnp.where(causal, diff, 0.0)), 0.0)
    att = jnp.einsum('bnthd,bnshd,bntshd->bnhts', q, k, decay)
    o_intra = jnp.einsum('bnhts,bnshe->bnthe', att, v)
    q_dec = q * jnp.exp(cum)
    k_dec = k * jnp.exp(cum[:, :, -1:] - cum)
    a_tot = jnp.exp(cum[:, :, -1])

    def step(state, inp):
        qd, kd, vv, at = inp
        o = jnp.einsum('bchd,bhde->bche', qd, state)
        state = at[..., None] * state + jnp.einsum('bchd,bche->bhde', kd, vv)
        return state, o
    xs = (jnp.moveaxis(q_dec, 1, 0), jnp.moveaxis(k_dec, 1, 0), jnp.moveaxis(v, 1, 0), jnp.moveaxis(a_tot, 1, 0))
    s_fin, o_inter = lax.scan(step, s0.astype(f32), xs)
    o = o_intra + jnp.moveaxis(o_inter, 0, 1)
    return o.reshape(b, n * c, GLA_HEADS, GLA_DV)[:, :l], s_fin


def _s5_combine(e1, e2):
    a1r, a1i, b1r, b1i = e1
    a2r, a2i, b2r, b2i = e2
    return (a2r * a1r - a2i * a1i, a2r * a1i + a2i * a1r,
            a2r * b1r - a2i * b1i + b2r, a2r * b1i + a2i * b1r + b2i)


def _s5_scan(u, h0, a_re, a_im, b_re, b_im, c_re, c_im, d, log_dt):
    f32 = jnp.float32
    b, l, _ = u.shape
    dt = jnp.exp(log_dt.astype(f32))[:, None]
    mag = jnp.exp(a_re * dt)
    abr, abi = mag * jnp.cos(a_im * dt), mag * jnp.sin(a_im * dt)
    den = a_re * a_re + a_im * a_im
    nr = abr - 1.0
    cr = (nr * a_re + abi * a_im) / den
    ci = (abi * a_re - nr * a_im) / den
    bbr = cr[..., None] * b_re - ci[..., None] * b_im
    bbi = cr[..., None] * b_im + ci[..., None] * b_re
    ug = u.reshape(b, l, S5_GROUPS, S5_GROUP_CH)
    xr = jnp.einsum('gpc,blgc->blgp', bbr, ug)
    xi = jnp.einsum('gpc,blgc->blgp', bbi, ug)
    h0r, h0i = h0[..., 0].astype(f32), h0[..., 1].astype(f32)
    xr = xr.at[:, 0].add(abr * h0r - abi * h0i)
    xi = xi.at[:, 0].add(abr * h0i + abi * h0r)
    ar = jnp.broadcast_to(abr, xr.shape)
    ai = jnp.broadcast_to(abi, xr.shape)
    _, _, hr, hi = lax.associative_scan(_s5_combine, (ar, ai, xr, xi), axis=1)
    y = jnp.einsum('gcp,blgp->blgc', c_re, hr) - jnp.einsum('gcp,blgp->blgc', c_im, hi)
    y = y.reshape(b, l, S5_WIDTH) + d.astype(f32) * u
    return y, jnp.stack([hr[:, -1], hi[:, -1]], axis=-1)


GLA_CHUNK_ROWS = 256


def _split_kv(kv, b, l):
    kv5 = kv.reshape(b, l, 3, 2, NSA_KV_HEADS, HEAD_DIM)
    return kv5[:, :, 0], kv5[:, :, 1], kv5[:, :, 2]


def _prompt_layer(h2d, ple2d, lw, b, l, final_norm):
    qx, kv, kb, vb, nz, gl, gt, su, sz = _in_proj_b(h2d, b, l, lw['norm'], lw['w_in_b'])
    kvc = _compress(kv, b, l, lw['cmp_pe'], lw['cmp_w1'], lw['cmp_w2'])
    o_cmp, sel = _cmp_select(qx, kvc, gt, b, l)
    o_slc = _flash(qx, kb, vb, sel, gt, b, l, "slc")
    o_win = _flash(qx, kb, vb, sel, gt, b, l, "win")
    new_cmp, new_slc, new_win = _split_kv(kv, b, l)
    win_state = new_win[:, l - min(WINDOW, l):]

    t = min(GLA_CHUNK_ROWS, l)
    o_gla, gla_state = _gla(gl, gt, lw, jnp.zeros((b, GLA_HEADS, GLA_DK, GLA_DV), F32), b, l // t, t, t)

    o_s5, hfin = _s5(su, sz, lw, jnp.zeros((2, b * S5_CH), F32), b, l)
    s5_state = jnp.transpose(hfin.reshape(2, b, S5_GROUPS, S5_STATE), (1, 2, 3, 0))

    h2d = _merge_call(h2d, o_cmp, o_slc, o_win, nz, o_gla, o_s5, True, ple2d, lw, final_norm, b)
    return h2d, (new_cmp, new_slc, win_state, gla_state, s5_state)


def _sample_layer(h2d, ple2d, lw, past, b, l, final_norm):
    f32 = jnp.float32
    n = b * l
    qx, kv, kb, vb, nz, gl, gt, su, sz = _in_proj_b(h2d, 1, n, lw['norm'], lw['w_in_b'])
    su, sz = [jnp.transpose(a.reshape(b, l, S5_WIDTH), (1, 0, 2)).reshape(l, b * S5_WIDTH) for a in (su, sz)]
    new_cmp, new_slc, new_win = _split_kv(kv, b, l)
    pt = past['page_table']
    pos0 = pt.shape[1] * past['cmp_t'].shape[-1]
    assert (pos0 + l) // CMP_STRIDE == pos0 // CMP_STRIDE and pos0 % CMP_STRIDE == 0
    assert past['win'].shape[1] == WINDOW <= pos0
    kvc = _paged_compress(past['cmp_t'], past['layer'], pt, lw['cmp_pe'], lw['cmp_w1'], lw['cmp_w2'])
    gated = _decode_nsa(past['slc_t'], past['win_t'], past['layer'], pt, kvc, qx.astype(f32), kv, gt, b, l)
    win_state = jnp.concatenate([past['win'], new_win], axis=1)[:, l:]

    seqs = GLA_SAMPLE_ROWS // l
    o_gla, gla_state = _gla(gl, gt, lw, past['gla'], b // seqs, 1, GLA_SAMPLE_ROWS, l)

    h0 = jnp.transpose(past['s5'].astype(f32), (3, 0, 1, 2)).reshape(2, b * S5_CH)
    o_s5, hfin = _s5(su, sz, lw, h0, b, l)
    s5_state = jnp.transpose(hfin.reshape(2, b, S5_GROUPS, S5_STATE), (1, 2, 3, 0))

    o_s5 = jnp.transpose(o_s5.reshape(l, b, S5_WIDTH), (1, 0, 2)).reshape(n, S5_WIDTH)
    h2d = _merge_call(h2d, gated[0], gated[1], gated[2], nz, o_gla, o_s5, False, ple2d, lw, final_norm, 1)
    return h2d, (new_cmp, new_slc, win_state, gla_state, s5_state)


GLA_SAMPLE_ROWS = 128


def kernel(x_prompt, x_sample, cache_cmp, cache_slc, cache_win, state_gla, state_s5, page_table,
           p_prompt, p_sample, norm_mix, w_in, w_out, cmp_pe, cmp_w1, cmp_w2, gla_wg, gla_bg, gla_norm,
           s5_a_re, s5_a_im, s5_b_re, s5_b_im, s5_c_re, s5_c_im, s5_d, s5_log_dt, s5_w_glu,
           ple_norm, ple_w_gate, ple_w_proj, final_norm):
    depth = w_in.shape[0]
    w_in_b = _relayout_w_in_b(w_in)
    cmp_t = jnp.transpose(cache_cmp, (0, 1, 3, 4, 5, 2))
    slc_t = jnp.transpose(cache_slc, (0, 1, 3, 4, 5, 2))
    win_t = jnp.transpose(cache_win, (0, 1, 3, 4, 5, 2))
    bp, lp, d = x_prompt.shape
    bs, ls, _ = x_sample.shape
    hp, hs = x_prompt.reshape(bp * lp, d), x_sample.reshape(bs * ls, d)
    st_p, st_s = [], []
    for i in range(depth):
        fin = final_norm if i == depth - 1 else None
        lw = {'norm': norm_mix[i], 'w_in_b': w_in_b[i], 'w_out': w_out[i],
              'cmp_pe': cmp_pe[i], 'cmp_w1': cmp_w1[i], 'cmp_w2': cmp_w2[i],
              'gla_wg': gla_wg[i], 'gla_bg': gla_bg[i], 'gla_norm': gla_norm[i],
              's5_a_re': s5_a_re[i], 's5_a_im': s5_a_im[i], 's5_b_re': s5_b_re[i], 's5_b_im': s5_b_im[i],
              's5_c_re': s5_c_re[i], 's5_c_im': s5_c_im[i], 's5_d': s5_d[i], 's5_log_dt': s5_log_dt[i],
              's5_w_glu': s5_w_glu[i], 'ple_norm': ple_norm[i], 'ple_w_gate': ple_w_gate[i],
              'ple_w_proj': ple_w_proj[i]}
        hp, sp = _prompt_layer(hp, p_prompt[i].reshape(bp * lp, PLE_DIM), lw, bp, lp, fin)
        past = {'page_table': page_table, 'win': cache_win[i], 'gla': state_gla[i], 's5': state_s5[i],
                'cmp_t': cmp_t, 'slc_t': slc_t, 'win_t': win_t, 'layer': i}
        hs, ss = _sample_layer(hs, p_sample[i].reshape(bs * ls, PLE_DIM), lw, past, bs, ls, fin)
        st_p.append(sp)
        st_s.append(ss)
    outs = [hp.reshape(bp, lp, d), hs.reshape(bs, ls, d)]
    for j in range(5):
        outs.append(jnp.stack([s[j] for s in st_p]))
        outs.append(jnp.stack([s[j] for s in st_s]))
    return tuple(outs)
```

```python
import functools
import math

import jax
import jax.numpy as jnp
from jax import lax
from jax.experimental import pallas as pl
from jax.experimental.pallas import tpu as pltpu

D_MODEL = 1024
PLE_DIM = 256
HEAD_DIM = 64
NSA_HEADS = 8
NSA_KV_HEADS = 2
NSA_HPG = NSA_HEADS // NSA_KV_HEADS
NSA_WIDTH = NSA_HEADS * HEAD_DIM
CMP_BLOCK = 32
CMP_STRIDE = 16
SLC_BLOCK = 64
N_SELECT = 16
WINDOW = 512
SPARSE_Q_BLOCK = 64
WIN_Q_BLOCK = 128
GLA_HEADS = 4
GLA_DK = 32
GLA_DV = 64
GLA_WIDTH = GLA_HEADS * GLA_DV
GLA_GATE_RANK = 16
GLA_GATE_TAU = 16.0
GLA_CHUNK = 32
S5_GROUPS = 16
S5_GROUP_CH = 16
S5_STATE = 64
S5_WIDTH = S5_GROUPS * S5_GROUP_CH
S5_CH = S5_GROUPS * S5_STATE
MIX_WIDTH = NSA_WIDTH + GLA_WIDTH + S5_WIDTH
KV_WIDTH = 6 * NSA_KV_HEADS * HEAD_DIM
RMS_EPS = 1e-6
NEG_BIG = 1e9
MASKED = -1e30

LANES = 128
SUBLANES = 8
VMEM_LIMIT = 48 * 1024 * 1024

F32 = jnp.float32
BF16 = jnp.bfloat16
ALIBI = tuple(2.0 ** (-8.0 * (h + 1) / NSA_HEADS) for h in range(NSA_HEADS))

_SRC = {}
_off = 0
for _name, _w in (("nq", NSA_WIDTH), ("nkv", KV_WIDTH), ("ngate", 3 * NSA_HEADS), ("nz", NSA_WIDTH),
                  ("gq", GLA_HEADS * GLA_DK), ("gk", GLA_HEADS * GLA_DK), ("gv", GLA_WIDTH),
                  ("glr", GLA_GATE_RANK), ("gz", GLA_WIDTH), ("su", S5_WIDTH), ("sz", S5_WIDTH)):
    _SRC[_name] = (_off, _w)
    _off += _w
IN_WIDTH = _off


def _cols(w, name):
    o, n = _SRC[name]
    return w[..., o:o + n]


_ORDER_A = ("nq", "nkv", "nz", "gq", "gk", "gv", "gz", "su", "sz", "ngate", "glr")
_DST_A = {}
_off = 0
for _name in _ORDER_A:
    _DST_A[_name] = (_off, _SRC[_name][1])
    _off += _SRC[_name][1]
IN_PAD_A = -(-_off // LANES) * LANES


def _relayout_w_in_a(w_in):
    w = jnp.concatenate([_cols(w_in, n) for n in _ORDER_A], axis=-1)
    w = jnp.pad(w, ((0, 0),) * (w.ndim - 1) + ((0, IN_PAD_A - w.shape[-1]),))
    return w.astype(BF16)


def _seg_a(u, name):
    o, w = _DST_A[name]
    return u[..., o:o + w]


QX_W = NSA_HEADS * LANES
GL_W = 2 * GLA_HEADS * GLA_DK + 2 * GLA_WIDTH
GT_W = LANES
_B_SEGS = (("qx", QX_W), ("kv", KV_WIDTH), ("nz", NSA_WIDTH), ("gl", GL_W), ("gt", GT_W),
           ("su", S5_WIDTH), ("sz", S5_WIDTH))
_B_OFF = {}
_off = 0
for _name, _w in _B_SEGS:
    _B_OFF[_name] = (_off, _w)
    _off += _w
IN_PAD_B = _off


def _relayout_w_in_b(w_in):
    d = w_in.shape[:-1]
    wq = _cols(w_in, "nq") * (HEAD_DIM ** -0.5)
    z = jnp.zeros(d + (HEAD_DIM,), w_in.dtype)
    qparts = []
    for h in range(NSA_HEADS):
        wh = wq[..., h * HEAD_DIM:(h + 1) * HEAD_DIM]
        qparts += [wh, z] if h < NSA_HPG else [z, wh]
    gt = jnp.concatenate([_cols(w_in, "ngate"), _cols(w_in, "glr")], axis=-1)
    gt = jnp.pad(gt, ((0, 0),) * len(d) + ((0, GT_W - gt.shape[-1]),))
    w = jnp.concatenate(qparts + [_cols(w_in, "nkv"), _cols(w_in, "nz"), _cols(w_in, "gq"), _cols(w_in, "gk"),
                                  _cols(w_in, "gv"), _cols(w_in, "gz"), gt, _cols(w_in, "su"), _cols(w_in, "sz")],
                        axis=-1)
    return w.astype(BF16)


def _lane_tile(x, k):
    return jnp.concatenate([x] * k, axis=1)


def _params(*sem):
    return pltpu.CompilerParams(dimension_semantics=sem, vmem_limit_bytes=VMEM_LIMIT)


def _norm_rows(x_ref, g_ref):
    x = x_ref[...]
    ms = jnp.mean(x * x, axis=-1, keepdims=True)
    return (x * lax.rsqrt(ms + RMS_EPS) * g_ref[...]).astype(BF16)


def _in_proj_a_kernel(x_ref, g_ref, w_ref, o_ref):
    o_ref[...] = jnp.dot(_norm_rows(x_ref, g_ref), w_ref[...], preferred_element_type=F32)


def _in_proj_a(x2d, g, w_bf16):
    n, d = x2d.shape
    tm = min(512, n)
    return pl.pallas_call(
        _in_proj_a_kernel,
        out_shape=jax.ShapeDtypeStruct((n, IN_PAD_A), F32),
        grid=(n // tm,),
        in_specs=[pl.BlockSpec((tm, d), lambda i: (i, 0)),
                  pl.BlockSpec((1, d), lambda i: (0, 0)),
                  pl.BlockSpec((d, IN_PAD_A), lambda i: (0, 0))],
        out_specs=pl.BlockSpec((tm, IN_PAD_A), lambda i: (i, 0)),
        compiler_params=_params("parallel"),
        name="in_proj_a",
    )(x2d, g.reshape(1, d), w_bf16)


def _in_proj_b_kernel(x_ref, g_ref, w_ref, qx_ref, kv_ref, kb_ref, vb_ref, nz_ref, gl_ref, gt_ref, su_ref, sz_ref):
    xn = _norm_rows(x_ref, g_ref)

    def seg(name):
        o, w = _B_OFF[name]
        return jnp.dot(xn, w_ref[:, o:o + w], preferred_element_type=F32)
    qx_ref[...] = seg("qx").astype(BF16)
    kv = seg("kv")
    kv_ref[...] = kv
    kvb = kv.astype(BF16)
    ones = jnp.ones((kv.shape[0], LANES), BF16)
    for br in range(3):
        kb_ref[:, br * LANES:(br + 1) * LANES] = kvb[:, 2 * br * LANES:(2 * br + 1) * LANES]
        vb_ref[:, 2 * br * LANES:(2 * br + 1) * LANES] = kvb[:, (2 * br + 1) * LANES:(2 * br + 2) * LANES]
        vb_ref[:, (2 * br + 1) * LANES:(2 * br + 2) * LANES] = ones
    nz_ref[...] = seg("nz")
    gl_ref[...] = seg("gl")
    gt_ref[...] = seg("gt")
    su_ref[...] = seg("su")
    sz_ref[...] = seg("sz")


def _in_proj_b(x2d, b, l, g, w_bf16):
    n, d = x2d.shape
    tm = min(512, l)
    nt = l // tm
    row = lambda bi, ti: (bi * nt + ti, 0)
    tcol = lambda bi, ti: (ti, bi)
    shapes = [((n, QX_W), BF16), ((n, KV_WIDTH), F32), ((n, 3 * LANES), BF16), ((n, 6 * LANES), BF16),
              ((n, NSA_WIDTH), F32), ((n, GL_W), F32), ((n, GT_W), F32),
              ((l, b * S5_WIDTH), F32), ((l, b * S5_WIDTH), F32)]
    out_specs = [pl.BlockSpec((tm, s[0][1]), row) for s in shapes[:7]] + \
                [pl.BlockSpec((tm, S5_WIDTH), tcol)] * 2
    return pl.pallas_call(
        _in_proj_b_kernel,
        out_shape=[jax.ShapeDtypeStruct(*s) for s in shapes],
        grid=(b, nt),
        in_specs=[pl.BlockSpec((tm, d), row),
                  pl.BlockSpec((1, d), lambda bi, ti: (0, 0)),
                  pl.BlockSpec((d, IN_PAD_B), lambda bi, ti: (0, 0))],
        out_specs=out_specs,
        compiler_params=_params("parallel", "parallel"),
        name="in_proj_b",
    )(x2d, g.reshape(1, d), w_bf16)


def _compress_kernel(x_ref, pe_ref, w1_ref, w2_ref, o_ref, *, n_ch):
    halves = [jnp.zeros((n_ch, LANES), F32)] * 2
    for p in range(CMP_STRIDE):
        xp = x_ref[pl.ds(p, n_ch, stride=CMP_STRIDE), :]
        for s in range(2):
            q = s * CMP_STRIDE + p
            halves[s] = halves[s] + jnp.dot((xp + pe_ref[0, q:q + 1, :]).astype(BF16), w1_ref[0, q],
                                            preferred_element_type=F32)
    second = pltpu.roll(halves[1], n_ch - 1, 0)
    hid = jax.nn.gelu(halves[0] + second).astype(BF16)
    o_ref[0, 0] = jnp.dot(hid, w2_ref[0], preferred_element_type=F32).astype(BF16)


def _compress_weights(pe, w1, w2):
    eye = jnp.eye(NSA_KV_HEADS, dtype=F32)
    pe2 = jnp.tile(pe, (1, 1, NSA_KV_HEADS))
    w1r = w1.reshape(2, CMP_BLOCK, HEAD_DIM, HEAD_DIM)
    w1bd = jnp.einsum('gh,kpde->kpgdhe', eye, w1r).reshape(2, CMP_BLOCK, LANES, LANES).astype(BF16)
    w2bd = jnp.einsum('gh,kde->kgdhe', eye, w2).reshape(2, LANES, LANES).astype(BF16)
    return pe2, w1bd, w2bd


def _page_copies(pt_ref, pages_ref, buf, sem, layer, bi, part, slot, n_pages):
    out = []
    for j in range(n_pages):
        src = pages_ref.at[layer, pt_ref[bi, j]]
        out.append(pltpu.make_async_copy(src if part is None else src.at[part], buf.at[slot, j], sem.at[slot]))
    return out


def _paged_compress_kernel(pt_ref, pages_ref, pe_ref, w1_ref, w2_ref, o_ref, buf, x_s, sem,
                           *, layer, n_pages, n_steps, page):
    bi, s = pl.program_id(0), pl.program_id(1)
    step = bi * 2 + s
    slot = step % 2

    @pl.when(step == 0)
    def _():
        for c in _page_copies(pt_ref, pages_ref, buf, sem, layer, bi, s, slot, n_pages):
            c.start()

    @pl.when(step + 1 < n_steps)
    def _():
        nxt = step + 1
        for c in _page_copies(pt_ref, pages_ref, buf, sem, layer, nxt // 2, nxt % 2, 1 - slot, n_pages):
            c.start()
    for c in _page_copies(pt_ref, pages_ref, buf, sem, layer, bi, s, slot, n_pages):
        c.wait()

    def to_rows(j, carry):
        x_s[pl.ds(pl.multiple_of(j * page, page), page), :] = jnp.transpose(buf[slot, j].reshape(LANES, page))
        return carry
    lax.fori_loop(0, n_pages, to_rows, 0, unroll=8)
    _compress_kernel(x_s, pe_ref, w1_ref, w2_ref, o_ref, n_ch=n_pages * page // CMP_STRIDE)


def _paged_compress(pages_t, layer, page_table, pe, w1, w2):
    b, n_pages = page_table.shape
    page = pages_t.shape[-1]
    t = n_pages * page
    n_ch = t // CMP_STRIDE
    pe2, w1bd, w2bd = _compress_weights(pe, w1, w2)
    return pl.pallas_call(
        functools.partial(_paged_compress_kernel, layer=layer, n_pages=n_pages, n_steps=2 * b, page=page),
        out_shape=jax.ShapeDtypeStruct((2, b, n_ch, LANES), BF16),
        grid_spec=pltpu.PrefetchScalarGridSpec(
            num_scalar_prefetch=1,
            grid=(b, 2),
            in_specs=[pl.BlockSpec(memory_space=pl.ANY),
                      pl.BlockSpec((1, CMP_BLOCK, LANES), lambda bi, s, pt: (s, 0, 0)),
                      pl.BlockSpec((1, CMP_BLOCK, LANES, LANES), lambda bi, s, pt: (s, 0, 0, 0)),
                      pl.BlockSpec((1, LANES, LANES), lambda bi, s, pt: (s, 0, 0))],
            out_specs=pl.BlockSpec((1, 1, n_ch, LANES), lambda bi, s, pt: (s, bi, 0, 0)),
            scratch_shapes=[pltpu.VMEM((2, n_pages, NSA_KV_HEADS, HEAD_DIM, page), F32),
                            pltpu.VMEM((t, LANES), F32), pltpu.SemaphoreType.DMA((2,))]),
        compiler_params=_params("arbitrary", "arbitrary"),
        name="nsa_paged_compress",
    )(page_table, pages_t, pe2, w1bd, w2bd)


def _compress(kv, b, t, pe, w1, w2):
    n_ch = t // CMP_STRIDE
    pe2, w1bd, w2bd = _compress_weights(pe, w1, w2)
    return pl.pallas_call(
        functools.partial(_compress_kernel, n_ch=n_ch),
        out_shape=jax.ShapeDtypeStruct((2, b, n_ch, LANES), BF16),
        grid=(b, 2),
        in_specs=[pl.BlockSpec((t, LANES), lambda bi, s: (bi, s)),
                  pl.BlockSpec((1, CMP_BLOCK, LANES), lambda bi, s: (s, 0, 0)),
                  pl.BlockSpec((1, CMP_BLOCK, LANES, LANES), lambda bi, s: (s, 0, 0, 0)),
                  pl.BlockSpec((1, LANES, LANES), lambda bi, s: (s, 0, 0))],
        out_specs=pl.BlockSpec((1, 1, n_ch, LANES), lambda bi, s: (s, bi, 0, 0)),
        compiler_params=_params("parallel", "parallel"),
        name="nsa_compress",
    )(kv, pe2, w1bd, w2bd)


def _split3(x):
    hi = x.astype(BF16)
    r = x - hi.astype(F32)
    mid = r.astype(BF16)
    lo = (r - mid.astype(F32)).astype(BF16)
    return hi, mid, lo


def _cmp_sel_kernel(qx_ref, kc_ref, vc_ref, gt_ref, m2_ref, o_ref, sel_ref, sc_s, *, tq, nc):
    q0 = pl.program_id(1) * tq
    qpos = q0 + lax.broadcasted_iota(jnp.int32, (tq, 1), 0)
    cend = lax.broadcasted_iota(jnp.int32, (1, nc), 1) * CMP_STRIDE + (CMP_BLOCK - 1)
    dist = (qpos - cend).astype(F32)
    ok = dist >= 0.0
    kc = kc_ref[0, 0]
    vc = vc_ref[0, 0]
    gts = jax.nn.sigmoid(gt_ref[...])
    imps = []
    for g in range(NSA_KV_HEADS):
        imp = jnp.zeros((tq, nc), F32)
        for j in range(NSA_HPG):
            h = g * NSA_HPG + j
            s = lax.dot_general(qx_ref[:, h * LANES:(h + 1) * LANES], kc, (((1,), (1,)), ((), ())),
                                preferred_element_type=F32)
            s = jnp.where(ok, s - ALIBI[h] * dist, MASKED)
            e = jnp.where(ok, jnp.exp(s - jnp.max(s, axis=-1, keepdims=True)), 0.0)
            p = e / jnp.maximum(jnp.sum(e, axis=-1, keepdims=True), 1e-30)
            imp = imp + p
            o = jnp.dot(p.astype(BF16), vc, preferred_element_type=F32)
            o_ref[:, h * HEAD_DIM:(h + 1) * HEAD_DIM] = (o[:, g * HEAD_DIM:(g + 1) * HEAD_DIM]
                                                         * gts[:, 3 * h:3 * h + 1])
        imps.append(imp)
    m2 = m2_ref[...]
    score = sum(jnp.dot(piece, m2, preferred_element_type=F32) for piece in _split3(jnp.concatenate(imps, axis=1)))
    lane = lax.broadcasted_iota(jnp.int32, (1, LANES), 1)
    blk = lane & (SLC_BLOCK - 1)
    cur = qpos >> 6
    forced = (blk == 0) | (blk == cur) | (blk == cur - 1)
    elig = (blk * SLC_BLOCK) <= qpos
    sc_s[...] = jnp.transpose(jnp.where(elig, jnp.where(forced, NEG_BIG, score), -NEG_BIG))
    sub = lax.broadcasted_iota(jnp.int32, (SUBLANES, 1), 0)
    kept = []
    for g in range(NSA_KV_HEADS):
        for r in range(SLC_BLOCK // SUBLANES):
            mine = sc_s[g * SLC_BLOCK + r * SUBLANES:g * SLC_BLOCK + (r + 1) * SUBLANES, :]
            cnt = jnp.zeros(mine.shape, jnp.int32)
            for i in range(SLC_BLOCK):
                cand = sc_s[g * SLC_BLOCK + i:g * SLC_BLOCK + i + 1, :]
                if i < r * SUBLANES:
                    inc = jnp.where(cand >= mine, 1, 0)
                elif i >= (r + 1) * SUBLANES:
                    inc = jnp.where(cand > mine, 1, 0)
                else:
                    tie = jnp.where(sub > i - r * SUBLANES, 1, 0)
                    inc = jnp.where(cand > mine, 1, jnp.where(cand == mine, tie, 0))
                cnt = cnt + inc
            kept.append(jnp.where((cnt < N_SELECT) & (mine > -0.5 * NEG_BIG), 1.0, 0.0))
    sel_ref[...] = jnp.transpose(jnp.concatenate(kept, axis=0)).astype(BF16)


def _slc_map_matrix(nc, n_cmp):
    n = jnp.arange(nc)[:, None]
    j = jnp.arange(SLC_BLOCK)[None, :]
    ratio = SLC_BLOCK // CMP_STRIDE
    m = ((n >= ratio * j) & (n <= ratio * j + ratio - 1)).astype(F32) + \
        ((n >= ratio * j - 1) & (n <= ratio * j + ratio - 2)).astype(F32)
    m = jnp.where(n < n_cmp, m, 0.0)
    z = jnp.zeros_like(m)
    return jnp.concatenate([jnp.concatenate([m, z], 1), jnp.concatenate([z, m], 1)], 0).astype(BF16)


def _cmp_select(qx, kvc, gt, b, l):
    nc = kvc.shape[2]
    tq = min(256, l)
    nt = l // tq
    row = lambda bi, ti: (bi * nt + ti, 0)
    return pl.pallas_call(
        functools.partial(_cmp_sel_kernel, tq=tq, nc=nc),
        out_shape=[jax.ShapeDtypeStruct((b * l, NSA_WIDTH), F32), jax.ShapeDtypeStruct((b * l, LANES), BF16)],
        grid=(b, nt),
        in_specs=[pl.BlockSpec((tq, QX_W), row),
                  pl.BlockSpec((1, 1, nc, LANES), lambda bi, ti: (0, bi, 0, 0)),
                  pl.BlockSpec((1, 1, nc, LANES), lambda bi, ti: (1, bi, 0, 0)),
                  pl.BlockSpec((tq, GT_W), row),
                  pl.BlockSpec((2 * nc, LANES), lambda bi, ti: (0, 0))],
        out_specs=[pl.BlockSpec((tq, NSA_WIDTH), row), pl.BlockSpec((tq, LANES), row)],
        scratch_shapes=[pltpu.VMEM((LANES, tq), F32)],
        compiler_params=_params("parallel", "parallel"),
        name="nsa_cmp_select",
    )(qx, kvc, kvc, gt, _slc_map_matrix(nc, nc - 1))


FLASH_PARTS = 4


def _flash_kernel(qx_ref, k_ref, kf_ref, v_ref, sel_ref, gt_ref, o_ref, qs_s, acc_s, m_s,
                  *, tq, tk, mode, gate_col):
    i = pl.program_id(1)
    q0 = i * tq
    rows = NSA_HPG * tq
    d0 = (lax.broadcasted_iota(jnp.int32, (tq, tk), 0) - lax.broadcasted_iota(jnp.int32, (tq, tk), 1)).astype(F32)
    gts = jax.nn.sigmoid(gt_ref[...])
    lane = lax.broadcasted_iota(jnp.int32, (1, LANES), 1)

    def tile(g, k0, causal, winlow):
        kt = jnp.concatenate([k_ref[pl.ds(k0, tk), :], kf_ref[pl.ds(k0, tk), :]], axis=1)
        vt = v_ref[pl.ds(k0, tk), :]
        c = (q0 - k0).astype(F32)
        bias = None
        if causal:
            bias = jnp.where(d0 + c >= 0.0, 0.0, MASKED)
        if winlow:
            bias = jnp.where(d0 + c < float(WINDOW), bias, MASKED)
        for part in range(FLASH_PARTS):
            rpp = rows // FLASH_PARTS
            rs = slice(part * rpp, (part + 1) * rpp)
            s = lax.dot_general(qs_s[g, rs, :], kt, (((1,), (1,)), ((), ())), preferred_element_type=F32)
            if bias is not None and rpp >= tq:
                s = (s.reshape(-1, tq, tk) + bias[None]).reshape(-1, tk)
            elif bias is not None:
                s = s + bias[(part * rpp) % tq:(part * rpp) % tq + rpp]
            m_prev = m_s[g, rs, :]
            m_new = jnp.maximum(m_prev, jnp.max(s, axis=1, keepdims=True))
            p = jnp.exp(s - _lane_tile(m_new, tk // LANES))
            alpha = jnp.exp(m_prev - m_new)
            acc_s[g, rs, :] = _lane_tile(alpha, 2) * acc_s[g, rs, :] + jnp.dot(p.astype(BF16), vt,
                                                                                  preferred_element_type=F32)
            m_s[g, rs, :] = m_new

    for g in range(NSA_KV_HEADS):
        if mode == "slc":
            blockbias = (sel_ref[...].astype(F32) - 1.0) * (-MASKED)
            if g == 1:
                blockbias = pltpu.roll(blockbias, SLC_BLOCK, 1)
        else:
            blockbias = jnp.zeros((tq, LANES), F32)
        for jh in range(NSA_HPG):
            h = g * NSA_HPG + jh
            feat = jnp.where(lane < SLC_BLOCK, blockbias,
                             jnp.where(lane == SLC_BLOCK, ALIBI[h] * SLC_BLOCK,
                                       jnp.where(lane == SLC_BLOCK + 1, ALIBI[h], 0.0)))
            qs_s[g, jh * tq:(jh + 1) * tq, 0:LANES] = qx_ref[:, h * LANES:(h + 1) * LANES]
            qs_s[g, jh * tq:(jh + 1) * tq, LANES:2 * LANES] = feat.astype(BF16)
        m_s[g] = jnp.full(m_s.shape[1:], MASKED, F32)
        acc_s[g] = jnp.zeros(acc_s.shape[1:], F32)
        if mode == "slc":
            last = (q0 + tq - 1) // tk

            def body(j, carry):
                tile(g, pl.multiple_of(j * tk, tk), False, False)
                return carry
            lax.fori_loop(0, last, body, 0)
            tile(g, pl.multiple_of(last * tk, tk), True, False)
        else:
            tile(g, pl.multiple_of(jnp.maximum(i - WINDOW // tq, 0) * tq, tq), True, True)
        for jh in range(NSA_HPG):
            h = g * NSA_HPG + jh
            c = 3 * h + gate_col
            a = acc_s[g, jh * tq:(jh + 1) * tq, :]
            o = a[:, g * HEAD_DIM:(g + 1) * HEAD_DIM] / a[:, LANES + g * HEAD_DIM:LANES + (g + 1) * HEAD_DIM]
            o_ref[:, h * HEAD_DIM:(h + 1) * HEAD_DIM] = o * gts[:, c:c + 1]


def _key_position_features(l):
    assert l <= SLC_BLOCK * SLC_BLOCK
    pos = jnp.arange(l)[:, None]
    lane = jnp.arange(LANES)[None, :]
    f = jnp.where(lane < SLC_BLOCK, (lane == pos // SLC_BLOCK).astype(F32),
                  jnp.where(lane == SLC_BLOCK, (pos // SLC_BLOCK).astype(F32),
                            jnp.where(lane == SLC_BLOCK + 1, (pos % SLC_BLOCK).astype(F32), 0.0)))
    return f.astype(BF16)


def _flash(qx, kb, vb, sel, gt, b, l, mode):
    tq = min(256, l)
    tk = 2 * tq if mode == "slc" else WINDOW + tq
    assert l % tk == 0 if mode == "slc" else (l >= tk and WINDOW % tq == 0)
    nt = l // tq
    rows = NSA_HPG * tq
    row = lambda bi, ti: (bi * nt + ti, 0)
    branch = 1 if mode == "slc" else 2
    return pl.pallas_call(
        functools.partial(_flash_kernel, tq=tq, tk=tk, mode=mode, gate_col=branch),
        out_shape=jax.ShapeDtypeStruct((b * l, NSA_WIDTH), F32),
        grid=(b, nt),
        in_specs=[pl.BlockSpec((tq, QX_W), row),
                  pl.BlockSpec((l, LANES), lambda bi, ti: (bi, branch)),
                  pl.BlockSpec((l, LANES), lambda bi, ti: (0, 0)),
                  pl.BlockSpec((l, 2 * LANES), lambda bi, ti: (bi, branch)),
                  pl.BlockSpec((tq, LANES), row),
                  pl.BlockSpec((tq, GT_W), row)],
        out_specs=pl.BlockSpec((tq, NSA_WIDTH), row),
        scratch_shapes=[pltpu.VMEM((NSA_KV_HEADS, rows, 2 * LANES), BF16),
                        pltpu.VMEM((NSA_KV_HEADS, rows, 2 * LANES), F32),
                        pltpu.VMEM((NSA_KV_HEADS, rows, LANES), F32)],
        compiler_params=_params("parallel", "parallel"),
        name="nsa_" + mode,
    )(qx, kb, _key_position_features(l), vb, sel, gt)


def _decode_nsa_kernel(pt_ref, slc_ref, win_ref, kc_ref, vc_ref, q_ref, kvn_ref, gt_ref, m2_ref,
                       oc_ref, os_ref, ow_ref, buf, sem, *, layer, n_pages, page, nb, l, n_slc, nsp):
    bi = pl.program_id(0)
    slot = bi % 2
    pos0 = n_pages * page
    nc = kc_ref.shape[2]
    rows = NSA_HPG * l

    @pl.when(bi == 0)
    def _():
        for c in _page_copies(pt_ref, slc_ref, buf, sem, layer, bi, None, slot, n_pages):
            c.start()

    @pl.when(bi + 1 < nb)
    def _():
        for c in _page_copies(pt_ref, slc_ref, buf, sem, layer, bi + 1, None, 1 - slot, n_pages):
            c.start()

    qf = q_ref[...]
    kvn = kvn_ref[...]
    gts = jax.nn.sigmoid(gt_ref[...])
    t_col = lax.broadcasted_iota(jnp.int32, (rows, 1), 0) % l
    t_f = t_col.astype(F32)
    lane = lax.broadcasted_iota(jnp.int32, (1, LANES), 1)
    lane_f = lane.astype(F32)

    def pad_rows(x):
        return jnp.concatenate([x, jnp.zeros((LANES - l, x.shape[1]), x.dtype)], axis=0)

    def group_q(g):
        q = jnp.concatenate([qf[:, (g * NSA_HPG + j) * LANES:(g * NSA_HPG + j + 1) * LANES]
                             for j in range(NSA_HPG)], axis=0).astype(BF16)
        sl = jnp.concatenate([jnp.full((l, 1), ALIBI[g * NSA_HPG + j], F32) for j in range(NSA_HPG)], axis=0)
        return q, sl

    def new_keys(q, sl, branch, tc=t_col, tf=t_f):
        kn = pad_rows(kvn[:, 2 * branch * LANES:(2 * branch + 1) * LANES]).astype(BF16)
        vn = pad_rows(kvn[:, (2 * branch + 1) * LANES:(2 * branch + 2) * LANES]).astype(BF16)
        s = lax.dot_general(q, kn, (((1,), (1,)), ((), ())), preferred_element_type=F32)
        s = jnp.where(lane <= tc, s - sl * (tf - lane_f), MASKED)
        return s, vn

    def write(o_ref, o, den, g, col):
        for j in range(NSA_HPG):
            h = g * NSA_HPG + j
            c = 3 * h + col
            oh = o[j * l:(j + 1) * l, g * HEAD_DIM:(g + 1) * HEAD_DIM] / den[j * l:(j + 1) * l]
            o_ref[:, h * HEAD_DIM:(h + 1) * HEAD_DIM] = oh * gts[:, c:c + 1]

    cend = (lax.broadcasted_iota(jnp.int32, (1, nc), 1) * CMP_STRIDE + (CMP_BLOCK - 1)).astype(F32)
    imps = []
    qs = []
    for g in range(NSA_KV_HEADS):
        q, sl = group_q(g)
        qs.append((q, sl))
        dist = (t_f + float(pos0)) - cend
        ok = dist >= 0.0
        s = lax.dot_general(q, kc_ref[0, 0], (((1,), (1,)), ((), ())), preferred_element_type=F32)
        s = jnp.where(ok, s - sl * dist, MASKED)
        e = jnp.where(ok, jnp.exp(s - jnp.max(s, axis=-1, keepdims=True)), 0.0)
        p = e / jnp.maximum(jnp.sum(e, axis=-1, keepdims=True), 1e-30)
        write(oc_ref, jnp.dot(p.astype(BF16), vc_ref[0, 0], preferred_element_type=F32),
              jnp.ones((rows, 1), F32), g, 0)
        imps.append(sum(p[j * l:(j + 1) * l] for j in range(NSA_HPG)))
    m2 = m2_ref[...]
    score = sum(jnp.dot(piece, m2, preferred_element_type=F32) for piece in _split3(jnp.concatenate(imps, axis=1)))
    lane2 = lax.broadcasted_iota(jnp.int32, (1, NSA_KV_HEADS * nsp), 1)
    blk = lane2 & (nsp - 1)
    qpos = pos0 + lax.broadcasted_iota(jnp.int32, (l, 1), 0)
    cur = qpos // SLC_BLOCK
    forced = (blk == 0) | (blk == cur) | (blk == cur - 1)
    elig = ((blk * SLC_BLOCK) <= qpos) & (blk < n_slc)
    sc = jnp.where(elig, jnp.where(forced, NEG_BIG, score), -NEG_BIG)
    cnt = jnp.zeros(sc.shape, jnp.int32)
    for i in range(n_slc):
        col = jnp.where(lane2 < nsp, sc[:, i:i + 1], sc[:, nsp + i:nsp + i + 1])
        cnt = cnt + jnp.where(col > sc, 1, jnp.where(col == sc, jnp.where(blk > i, 1, 0), 0))
    sel = jnp.where((cnt < N_SELECT) & (sc > -0.5 * NEG_BIG), 1.0, 0.0)

    w = win_ref.shape[-1]
    wpos = lax.broadcasted_iota(jnp.int32, (1, w), 1)
    for g in range(NSA_KV_HEADS):
        q, sl = qs[g]
        kt = win_ref[0, 0, 0].reshape(LANES, w).astype(BF16)
        vt = win_ref[0, 0, 1].reshape(LANES, w).astype(BF16)
        dist = (t_f + float(w)) - wpos.astype(F32)
        s = jnp.dot(q, kt, preferred_element_type=F32)
        s = jnp.where(dist < float(WINDOW), s - sl * dist, MASKED)
        sn, vn = new_keys(q, sl, 2)
        m = jnp.maximum(jnp.max(s, axis=-1, keepdims=True), jnp.max(sn, axis=-1, keepdims=True))
        p, pn = jnp.exp(s - m), jnp.exp(sn - m)
        o = (lax.dot_general(p.astype(BF16), vt, (((1,), (1,)), ((), ())), preferred_element_type=F32)
             + jnp.dot(pn.astype(BF16), vn, preferred_element_type=F32))
        write(ow_ref, o, jnp.sum(p, axis=-1, keepdims=True) + jnp.sum(pn, axis=-1, keepdims=True), g, 2)

    for c in _page_copies(pt_ref, slc_ref, buf, sem, layer, bi, None, slot, n_pages):
        c.wait()
    half = page // SLC_BLOCK
    q = jnp.concatenate([qs[g][0] for g in range(NSA_KV_HEADS)], axis=0)
    sl = jnp.concatenate([qs[g][1] for g in range(NSA_KV_HEADS)], axis=0)
    t2c = jnp.concatenate([t_col] * NSA_KV_HEADS, axis=0)
    t2f = jnp.concatenate([t_f] * NSA_KV_HEADS, axis=0)
    ss = []
    for j in range(n_pages):
        kt = buf[slot, j, 0].reshape(LANES, page).astype(BF16)
        s = jnp.dot(q, kt, preferred_element_type=F32)
        keeps = []
        for g in range(NSA_KV_HEADS):
            keep = sel[:, g * nsp + half * j:g * nsp + half * j + 1]
            for hb in range(1, half):
                keep = jnp.where(lane < hb * SLC_BLOCK, keep, sel[:, g * nsp + half * j + hb:g * nsp + half * j + hb + 1])
            keeps += [keep] * NSA_HPG
        dist = (t2f + float(pos0 - j * page)) - lane_f
        ss.append(jnp.where(jnp.concatenate(keeps, axis=0) > 0.5, s - sl * dist, MASKED))
    sn, vn = new_keys(q, sl, 1, t2c, t2f)
    mx = sn
    for s in ss:
        mx = jnp.maximum(mx, s)
    m = jnp.max(mx, axis=-1, keepdims=True)
    pn = jnp.exp(sn - m)
    psum = pn
    o = jnp.dot(pn.astype(BF16), vn, preferred_element_type=F32)
    for j in range(n_pages):
        p = jnp.exp(ss[j] - m)
        psum = psum + p
        vt = buf[slot, j, 1].reshape(LANES, page).astype(BF16)
        o = o + lax.dot_general(p.astype(BF16), vt, (((1,), (1,)), ((), ())), preferred_element_type=F32)
    den = jnp.sum(psum, axis=-1, keepdims=True)
    for g in range(NSA_KV_HEADS):
        write(os_ref, o[g * rows:(g + 1) * rows], den[g * rows:(g + 1) * rows], g, 1)


def _decode_nsa(slc_t, win_t, layer, page_table, kvc, qf, kvn, gt, b, l):
    n_pages, page = page_table.shape[1], slc_t.shape[-1]
    pos0 = n_pages * page
    nc = kvc.shape[2]
    n_slc = -(-(pos0 + l) // SLC_BLOCK)
    nsp = 1 << (n_slc - 1).bit_length()
    assert pos0 % SLC_BLOCK == 0 and l <= SLC_BLOCK and page % SLC_BLOCK == 0 and l % 8 == 0
    n = jnp.arange(nc)[:, None]
    jj = jnp.arange(nsp)[None, :]
    ratio = SLC_BLOCK // CMP_STRIDE
    m = ((n >= ratio * jj) & (n <= ratio * jj + ratio - 1)).astype(F32) + \
        ((n >= ratio * jj - 1) & (n <= ratio * jj + ratio - 2)).astype(F32)
    m = jnp.where((n < nc - 1) & (jj < n_slc), m, 0.0)
    z = jnp.zeros_like(m)
    m2 = jnp.concatenate([jnp.concatenate([m, z], 1), jnp.concatenate([z, m], 1)], 0).astype(BF16)
    row = lambda bi, pt: (bi, 0)
    w = win_t.shape[-1]
    out = jax.ShapeDtypeStruct((b * l, NSA_WIDTH), F32)
    return pl.pallas_call(
        functools.partial(_decode_nsa_kernel, layer=layer, n_pages=n_pages, page=page, nb=b, l=l, n_slc=n_slc,
                          nsp=nsp),
        out_shape=[out, out, out],
        grid_spec=pltpu.PrefetchScalarGridSpec(
            num_scalar_prefetch=1,
            grid=(b,),
            in_specs=[pl.BlockSpec(memory_space=pl.ANY),
                      pl.BlockSpec((1, 1, 2, NSA_KV_HEADS, HEAD_DIM, w), lambda bi, pt: (layer, bi, 0, 0, 0, 0)),
                      pl.BlockSpec((1, 1, nc, LANES), lambda bi, pt: (0, bi, 0, 0)),
                      pl.BlockSpec((1, 1, nc, LANES), lambda bi, pt: (1, bi, 0, 0)),
                      pl.BlockSpec((l, QX_W), row), pl.BlockSpec((l, KV_WIDTH), row), pl.BlockSpec((l, GT_W), row),
                      pl.BlockSpec((NSA_KV_HEADS * nc, NSA_KV_HEADS * nsp), lambda bi, pt: (0, 0))],
            out_specs=[pl.BlockSpec((l, NSA_WIDTH), row)] * 3,
            scratch_shapes=[pltpu.VMEM((2, n_pages, 2, NSA_KV_HEADS, HEAD_DIM, page), F32),
                            pltpu.SemaphoreType.DMA((2,))]),
        compiler_params=_params("arbitrary"),
        name="nsa_decode",
    )(page_table, slc_t, win_t, kvc, kvc, qf, kvn, gt, m2)


def _s5_kernel(u_ref, z_ref, bre_ref, bim_ref, cre_ref, cim_ref, ab_ref, d_ref, wglu_ref, h0_ref,
               o_ref, hout_ref, xr_s, xi_s, h_s, *, nb, tc):
    @pl.when(pl.program_id(0) == 0)
    def _():
        h_s[...] = h0_ref[...]
    bre, bim = bre_ref[...], bim_ref[...]
    for b in range(nb):
        ub = u_ref[:, b * S5_WIDTH:(b + 1) * S5_WIDTH].astype(BF16)
        xr_s[:, b * S5_CH:(b + 1) * S5_CH] = jnp.dot(ub, bre, preferred_element_type=F32)
        xi_s[:, b * S5_CH:(b + 1) * S5_CH] = jnp.dot(ub, bim, preferred_element_type=F32)
    abr = ab_ref[0:1, :]
    abi = ab_ref[1:2, :]

    def step(t, carry):
        hr, hi = carry
        nr = abr * hr - abi * hi + xr_s[pl.ds(t, 1), :]
        ni = abr * hi + abi * hr + xi_s[pl.ds(t, 1), :]
        xr_s[pl.ds(t, 1), :] = nr
        xi_s[pl.ds(t, 1), :] = ni
        return nr, ni
    hr, hi = lax.fori_loop(0, tc, step, (h_s[0:1, :], h_s[1:2, :]), unroll=8)
    h_s[0:1, :] = hr
    h_s[1:2, :] = hi
    hout_ref[...] = h_s[...]
    cre, cim, wglu = cre_ref[...], cim_ref[...], wglu_ref[...]
    for b in range(nb):
        u = u_ref[:, b * S5_WIDTH:(b + 1) * S5_WIDTH]
        y = (jnp.dot(xr_s[:, b * S5_CH:(b + 1) * S5_CH].astype(BF16), cre, preferred_element_type=F32)
             - jnp.dot(xi_s[:, b * S5_CH:(b + 1) * S5_CH].astype(BF16), cim, preferred_element_type=F32)
             + d_ref[...] * u)
        gl = jnp.dot(jax.nn.gelu(y).astype(BF16), wglu, preferred_element_type=F32)
        z = z_ref[:, b * S5_WIDTH:(b + 1) * S5_WIDTH]
        o_ref[:, b * S5_WIDTH:(b + 1) * S5_WIDTH] = (gl[:, :S5_WIDTH] * jax.nn.sigmoid(gl[:, S5_WIDTH:])
                                                       * (z * jax.nn.sigmoid(z)))


def _s5_discretise(lw):
    a_re, a_im = lw['s5_a_re'].astype(F32), lw['s5_a_im'].astype(F32)
    b_re, b_im = lw['s5_b_re'].astype(F32), lw['s5_b_im'].astype(F32)
    dt = jnp.exp(lw['s5_log_dt'].astype(F32))[:, None]
    mag = jnp.exp(a_re * dt)
    abr, abi = mag * jnp.cos(a_im * dt), mag * jnp.sin(a_im * dt)
    den = a_re * a_re + a_im * a_im
    nr = abr - 1.0
    cr = (nr * a_re + abi * a_im) / den
    ci = (abi * a_re - nr * a_im) / den
    bbr = cr[..., None] * b_re - ci[..., None] * b_im
    bbi = cr[..., None] * b_im + ci[..., None] * b_re
    eye = jnp.eye(S5_GROUPS, dtype=F32)
    bd_in = lambda m: jnp.einsum('gh,gpc->gchp', eye, m).reshape(S5_WIDTH, S5_CH).astype(BF16)
    bd_out = lambda m: jnp.einsum('gh,gcp->gphc', eye, m.astype(F32)).reshape(S5_CH, S5_WIDTH).astype(BF16)
    ab = jnp.stack([abr.reshape(S5_CH), abi.reshape(S5_CH)])
    return bd_in(bbr), bd_in(bbi), bd_out(lw['s5_c_re']), bd_out(lw['s5_c_im']), ab


def _s5(su, sz, lw, h0, nb, l):
    bre, bim, cre, cim, ab = _s5_discretise(lw)
    tc = min(256, l)
    const = lambda t: (0, 0)
    return pl.pallas_call(
        functools.partial(_s5_kernel, nb=nb, tc=tc),
        out_shape=[jax.ShapeDtypeStruct((l, nb * S5_WIDTH), F32), jax.ShapeDtypeStruct((2, nb * S5_CH), F32)],
        grid=(l // tc,),
        in_specs=[pl.BlockSpec((tc, nb * S5_WIDTH), lambda t: (t, 0)),
                  pl.BlockSpec((tc, nb * S5_WIDTH), lambda t: (t, 0)),
                  pl.BlockSpec((S5_WIDTH, S5_CH), const), pl.BlockSpec((S5_WIDTH, S5_CH), const),
                  pl.BlockSpec((S5_CH, S5_WIDTH), const), pl.BlockSpec((S5_CH, S5_WIDTH), const),
                  pl.BlockSpec((2, nb * S5_CH), const), pl.BlockSpec((1, S5_WIDTH), const),
                  pl.BlockSpec((S5_WIDTH, 2 * S5_WIDTH), const), pl.BlockSpec((2, nb * S5_CH), const)],
        out_specs=[pl.BlockSpec((tc, nb * S5_WIDTH), lambda t: (t, 0)), pl.BlockSpec((2, nb * S5_CH), const)],
        scratch_shapes=[pltpu.VMEM((tc, nb * S5_CH), F32), pltpu.VMEM((tc, nb * S5_CH), F32),
                        pltpu.VMEM((2, nb * S5_CH), F32)],
        compiler_params=_params("arbitrary"),
        name="s5",
    )(su, sz, bre, bim, cre, cim, jnp.tile(ab, (1, nb)), lw['s5_d'].astype(F32).reshape(1, S5_WIDTH),
      lw['s5_w_glu'].astype(BF16), h0)


GLA_QK = GLA_HEADS * GLA_DK


def _gla_kernel(gl_ref, gt_ref, wg_ref, bg_ref, gn_ref, pall_ref, lm_ref, seg_ref, s0_ref, o_ref, sout_ref, st_s,
                *, r, t, nl):
    nseq = r // t

    @pl.when(pl.program_id(1) == 0)
    def _():
        st_s[...] = s0_ref[...]
    q = gl_ref[:, 0:GLA_QK] * (GLA_DK ** -0.5)
    k = gl_ref[:, GLA_QK:2 * GLA_QK]
    v = gl_ref[:, 2 * GLA_QK:2 * GLA_QK + GLA_WIDTH].astype(BF16)
    gz = gl_ref[:, 2 * GLA_QK + GLA_WIDTH:]
    lg = jnp.dot(gt_ref[...].astype(BF16), wg_ref[...], preferred_element_type=F32) + bg_ref[...]
    la = (jnp.minimum(lg, 0.0) - jnp.log(1.0 + jnp.exp(-jnp.abs(lg)))) * (1.0 / GLA_GATE_TAU)
    pall = pall_ref[...]
    cums = sum(jnp.dot(pall, piece, preferred_element_type=F32) for piece in _split3(la))
    cum = cums[0:r]
    cl = cums[(nl + 1) * r:(nl + 2) * r]
    lane_qk = lax.broadcasted_iota(jnp.int32, (1, GLA_QK), 1) // GLA_DK
    lane_v = lax.broadcasted_iota(jnp.int32, (1, GLA_WIDTH), 1) // GLA_DV

    def stack(x):
        return jnp.concatenate([jnp.where(lane_qk == h, x, 0.0) for h in range(GLA_HEADS)], axis=0).astype(BF16)

    def level(l, qf, kf):
        s = lax.dot_general(stack(qf), kf.astype(BF16), (((1,), (1,)), ((), ())), preferred_element_type=F32)
        return (s.reshape(GLA_HEADS, r, r) * lm_ref[l][None]).reshape(GLA_HEADS * r, r)
    att = level(0, q, k)
    for l in range(1, nl + 1):
        cm = cums[l * r:(l + 1) * r]
        att = att + level(l, q * jnp.exp(jnp.minimum(cum - cm, 0.0)), k * jnp.exp(jnp.minimum(cm - cum, 0.0)))
    res = jnp.dot(att.astype(BF16), v, preferred_element_type=F32)
    o = sum(jnp.where(lane_v == h, res[h * r:(h + 1) * r], 0.0) for h in range(GLA_HEADS))
    qd = (q * jnp.exp(cum)).astype(BF16)
    kd = k * jnp.exp(cl - cum)
    bd = (lax.broadcasted_iota(jnp.int32, (GLA_QK, GLA_WIDTH), 0) // GLA_DK
          == lax.broadcasted_iota(jnp.int32, (GLA_QK, GLA_WIDTH), 1) // GLA_DV)
    row_seq = lax.broadcasted_iota(jnp.int32, (r, 1), 0) // t
    inter = []
    for i in range(nseq):
        st = st_s[i]
        inter.append(jnp.dot(qd[i * t:(i + 1) * t], st.astype(BF16), preferred_element_type=F32))
        kdi = kd if nseq == 1 else jnp.where(row_seq == i, kd, 0.0)
        upd = jnp.dot(jnp.transpose(kdi).astype(BF16), v, preferred_element_type=F32)
        a_col = jnp.transpose(jnp.broadcast_to(jnp.exp(cl[i * t:i * t + 1, :]), (GLA_QK, GLA_QK)))
        st_s[i] = jnp.where(bd, _lane_tile(a_col, GLA_WIDTH // GLA_QK) * st + upd, 0.0)
    o = o + (inter[0] if nseq == 1 else jnp.concatenate(inter, axis=0))
    sout_ref[...] = st_s[...]
    hi = (o * o).astype(BF16)
    lo = (o * o - hi.astype(F32)).astype(BF16)
    ms = (jnp.dot(hi, seg_ref[...], preferred_element_type=F32) + jnp.dot(lo, seg_ref[...], preferred_element_type=F32))
    o_ref[...] = o * lax.rsqrt(ms + RMS_EPS) * gn_ref[...] * (gz * jax.nn.sigmoid(gz))


def _gla_tables(r, t, nl):
    idx = jnp.arange(r)
    seq = idx // t
    rows = [(seq[:, None] == seq[None, :]) & (idx[None, :] <= idx[:, None])]
    masks = [idx[:, None] == idx[None, :]]
    for l in range(1, nl + 1):
        mid = (idx >> l << l) + (1 << (l - 1)) - 1
        rows.append((seq[:, None] == seq[None, :]) & (idx[None, :] <= mid[:, None]))
        masks.append(((idx[:, None] >> l) == (idx[None, :] >> l)) & (((idx[:, None] >> (l - 1)) & 1) == 1)
                     & (((idx[None, :] >> (l - 1)) & 1) == 0))
    rows.append(seq[:, None] == seq[None, :])
    return jnp.concatenate(rows, 0).astype(BF16), jnp.stack(masks).astype(F32)


def _gla(gl, gt, lw, s0, nb, nchunk, r, t):
    n = gl.shape[0]
    nseq = r // t
    nl = t.bit_length() - 1
    pall, lm = _gla_tables(r, t, nl)
    wg = jnp.zeros((GT_W, GLA_QK), F32).at[3 * NSA_HEADS:3 * NSA_HEADS + GLA_GATE_RANK].set(lw['gla_wg']).astype(BF16)
    eye = jnp.eye(GLA_HEADS, dtype=F32)
    seg = (jnp.einsum('gh,de->gdhe', eye, jnp.ones((GLA_DV, GLA_DV), F32)) / GLA_DV).reshape(GLA_WIDTH, GLA_WIDTH)
    s0bd = jnp.einsum('gh,bgde->bgdhe', eye, s0.astype(F32)).reshape(-1, GLA_QK, GLA_WIDTH)
    row = lambda bi, ci: (bi * nchunk + ci, 0)
    const2 = lambda bi, ci: (0, 0)
    o, sfin = pl.pallas_call(
        functools.partial(_gla_kernel, r=r, t=t, nl=nl),
        out_shape=[jax.ShapeDtypeStruct((n, GLA_WIDTH), F32),
                   jax.ShapeDtypeStruct((nb * nseq, GLA_QK, GLA_WIDTH), F32)],
        grid=(nb, nchunk),
        in_specs=[pl.BlockSpec((r, GL_W), row), pl.BlockSpec((r, GT_W), row),
                  pl.BlockSpec((GT_W, GLA_QK), const2), pl.BlockSpec((1, GLA_QK), const2),
                  pl.BlockSpec((1, GLA_WIDTH), const2), pl.BlockSpec(((nl + 2) * r, r), const2),
                  pl.BlockSpec((nl + 1, r, r), lambda bi, ci: (0, 0, 0)),
                  pl.BlockSpec((GLA_WIDTH, GLA_WIDTH), const2),
                  pl.BlockSpec((nseq, GLA_QK, GLA_WIDTH), lambda bi, ci: (bi, 0, 0))],
        out_specs=[pl.BlockSpec((r, GLA_WIDTH), row),
                   pl.BlockSpec((nseq, GLA_QK, GLA_WIDTH), lambda bi, ci: (bi, 0, 0))],
        scratch_shapes=[pltpu.VMEM((nseq, GLA_QK, GLA_WIDTH), F32)],
        compiler_params=_params("parallel", "arbitrary"),
        name="gla",
    )(gl, gt, wg, lw['gla_bg'].astype(F32).reshape(1, GLA_QK),
      jnp.tile(lw['gla_norm'].astype(F32), GLA_HEADS).reshape(1, GLA_WIDTH), pall, lm, seg.astype(BF16), s0bd)
    sf = sfin.reshape(-1, GLA_HEADS, GLA_DK, GLA_HEADS, GLA_DV)
    state = jnp.stack([sf[:, h, :, h, :] for h in range(GLA_HEADS)], axis=1)
    return o, state


def _merge_kernel(h_ref, oc_ref, os_ref, ow_ref, nz_ref, og_ref, o5_ref, ple_ref, wo_ref, pn_ref, wgate_ref, wproj_ref,
                  fn_ref, out_ref, *, final):
    nz = nz_ref[...]
    o_nsa = (oc_ref[...] + os_ref[...] + ow_ref[...]) * (nz * jax.nn.sigmoid(nz))
    mix = jnp.concatenate([o_nsa, og_ref[...], o5_ref[...]], axis=1).astype(BF16)
    h = h_ref[...] + jnp.dot(mix, wo_ref[...], preferred_element_type=F32)

    def norm(x, g_ref):
        return x * lax.rsqrt(jnp.mean(x * x, axis=-1, keepdims=True) + RMS_EPS) * g_ref[...]
    gate = jax.nn.sigmoid(jnp.dot(norm(h, pn_ref).astype(BF16), wgate_ref[...], preferred_element_type=F32))
    h = h + gate * jnp.dot(ple_ref[...].astype(BF16), wproj_ref[...], preferred_element_type=F32)
    out_ref[...] = norm(h, fn_ref) if final else h


def _merge_call(h2d, o_cmp, o_slc, o_win, nz, o_gla, o_s5, s5_time_major, ple2d, lw, final_norm, nb):
    n, d = h2d.shape
    l = n // nb
    tm = min(512, l)
    nt = l // tm
    row = lambda bi, ti: (bi * nt + ti, 0)
    const = lambda bi, ti: (0, 0)
    o5_spec = pl.BlockSpec((tm, S5_WIDTH), (lambda bi, ti: (ti, bi)) if s5_time_major else row)
    fn = jnp.ones((d,), F32) if final_norm is None else final_norm
    return pl.pallas_call(
        functools.partial(_merge_kernel, final=final_norm is not None),
        out_shape=jax.ShapeDtypeStruct((n, d), F32),
        grid=(nb, nt),
        in_specs=[pl.BlockSpec((tm, d), row)] + [pl.BlockSpec((tm, NSA_WIDTH), row)] * 4 +
                 [pl.BlockSpec((tm, GLA_WIDTH), row), o5_spec, pl.BlockSpec((tm, PLE_DIM), row),
                  pl.BlockSpec((MIX_WIDTH, d), const), pl.BlockSpec((1, d), const), pl.BlockSpec((d, d), const),
                  pl.BlockSpec((PLE_DIM, d), const), pl.BlockSpec((1, d), const)],
        out_specs=pl.BlockSpec((tm, d), row),
        compiler_params=_params("parallel", "parallel"),
        name="merge",
    )(h2d, o_cmp, o_slc, o_win, nz, o_gla, o_s5, ple2d, lw['w_out'].astype(BF16),
      lw['ple_norm'].astype(F32).reshape(1, d), lw['ple_w_gate'].astype(BF16), lw['ple_w_proj'].astype(BF16),
      fn.astype(F32).reshape(1, d))


def _rmsnorm(x, g):
    xf = x.astype(jnp.float32)
    y = xf * lax.rsqrt(jnp.mean(xf * xf, axis=-1, keepdims=True) + RMS_EPS)
    return (y * g.astype(jnp.float32)).astype(x.dtype)


def _masked_softmax(s, mask):
    s = jnp.where(mask, s, -1e30)
    e = jnp.where(mask, jnp.exp(s - jnp.max(s, axis=-1, keepdims=True)), 0.0)
    return e / jnp.maximum(jnp.sum(e, axis=-1, keepdims=True), 1e-30)


def _alibi_slopes():
    return jnp.exp2(-8.0 * jnp.arange(1, NSA_HEADS + 1, dtype=jnp.float32) / NSA_HEADS)


def _compress_kv(x, pe, w1, w2):
    b, t, g, d = x.shape
    ratio = CMP_BLOCK // CMP_STRIDE
    n_ch = t // CMP_STRIDE
    n_cmp = n_ch - ratio + 1
    ch = x[:, :n_ch * CMP_STRIDE].reshape(b, n_ch, CMP_STRIDE, g, d)
    blocks = jnp.concatenate([ch[:, s:s + n_cmp] for s in range(ratio)], axis=2)
    blocks = blocks + pe[None, None, :, None, :]
    flat = jnp.moveaxis(blocks, 2, 3).reshape(b, n_cmp, g, CMP_BLOCK * d)
    return jax.nn.gelu(flat @ w1) @ w2


def _nsa_sparse_block(q, qpos, kc, vc, cend, ksb, vsb, slopes):
    f32 = jnp.float32
    b, qb = q.shape[:2]
    sl = slopes.reshape(NSA_KV_HEADS, NSA_HPG)
    scale = HEAD_DIM ** -0.5
    s = jnp.einsum('bqghd,bngd->bqghn', q, kc).astype(f32) * scale
    dist = (qpos[:, None] - cend[None, :]).astype(f32)
    s = s - sl[:, :, None] * dist[:, None, None, :]
    p = _masked_softmax(s, (cend[None, :] <= qpos[:, None])[:, None, None, :])
    o_cmp = jnp.einsum('bqghn,bngd->bqghd', p.astype(vc.dtype), vc)
    imp = p.sum(axis=3)
    n_cmp = imp.shape[-1]
    n_slc = ksb.shape[2]
    run = imp
    for sft in range(1, CMP_BLOCK // CMP_STRIDE):
        run = run + jnp.pad(imp, ((0, 0), (0, 0), (0, 0), (sft, 0)))[..., :n_cmp]
    ratio = SLC_BLOCK // CMP_STRIDE
    run = jnp.pad(run, ((0, 0), (0, 0), (0, 0), (0, ratio * n_slc - n_cmp)))
    score = run.reshape(b, qb, NSA_KV_HEADS, n_slc, ratio).sum(-1)
    blk = jnp.arange(n_slc)
    cur = qpos // SLC_BLOCK
    eligible = blk[None, :] * SLC_BLOCK <= qpos[:, None]
    forced = (blk[None, :] == 0) | (blk[None, :] == cur[:, None]) | (blk[None, :] == cur[:, None] - 1)
    score = jnp.where(forced[:, None, :], NEG_BIG, score)
    score = jnp.where(eligible[:, None, :], score, -NEG_BIG)
    top_val, top_idx = lax.top_k(score, min(N_SELECT, n_slc))
    sel_ok = top_val > -0.5 * NEG_BIG
    b_ix = jnp.arange(b)[:, None, None, None]
    g_ix = jnp.arange(NSA_KV_HEADS)[None, None, :, None]
    ksel = ksb[b_ix, g_ix, top_idx]
    vsel = vsb[b_ix, g_ix, top_idx]
    kpos = top_idx[..., None] * SLC_BLOCK + jnp.arange(SLC_BLOCK)
    s2 = jnp.einsum('bqghd,bqgkjd->bqghkj', q, ksel).astype(f32) * scale
    dist2 = (qpos[None, :, None, None, None] - kpos).astype(f32)
    s2 = s2 - sl[None, None, :, :, None, None] * dist2[:, :, :, None]
    ok2 = (sel_ok[..., None] & (dist2 >= 0))[:, :, :, None]
    shp = s2.shape
    nk = shp[4] * shp[5]
    p2 = _masked_softmax(s2.reshape(shp[:4] + (nk,)), ok2.reshape(ok2.shape[:4] + (nk,))).reshape(shp)
    o_slc = jnp.einsum('bqghkj,bqgkjd->bqghd', p2.astype(vsel.dtype), vsel)
    return o_cmp, o_slc


def _window_attn(q, qpos, k, v, kpos, slopes):
    sl = slopes.reshape(NSA_KV_HEADS, NSA_HPG)
    s = jnp.einsum('bqghd,bkgd->bqghk', q, k).astype(jnp.float32) * HEAD_DIM ** -0.5
    dist = (qpos[:, None] - kpos[None, :]).astype(jnp.float32)
    s = s - sl[:, :, None] * dist[:, None, None, :]
    ok = ((dist >= 0) & (dist < WINDOW) & (kpos[None, :] >= 0))[:, None, None, :]
    p = _masked_softmax(s, ok)
    return jnp.einsum('bqghk,bkgd->bqghd', p.astype(v.dtype), v)


def _gla_chunked(q, k, v, log_a, s0):
    f32 = jnp.float32
    b, l = q.shape[:2]
    c = min(GLA_CHUNK, l)
    pad = (-l) % c
    n = (l + pad) // c

    def prep(t):
        t = jnp.pad(t.astype(f32), ((0, 0), (0, pad), (0, 0), (0, 0)))
        return t.reshape(b, n, c, GLA_HEADS, t.shape[-1])
    q = prep(q) * GLA_DK ** -0.5
    k, v, log_a = prep(k), prep(v), prep(log_a)
    cum = jnp.cumsum(log_a, axis=2)
    causal = jnp.tril(jnp.ones((c, c), dtype=bool))[:, :, None, None]
    diff = cum[:, :, :, None] - cum[:, :, None, :]
    decay = jnp.where(causal, jnp.exp(jnp.where(causal, diff, 0.0)), 0.0)
    att = jnp.einsum('bnthd,bnshd,bntshd->bnhts', q, k, decay)
    o_intra = jnp.einsum('bnhts,bnshe->bnthe', att, v)
    q_dec = q * jnp.exp(cum)
    k_dec = k * jnp.exp(cum[:, :, -1:] - cum)
    a_tot = jnp.exp(cum[:, :, -1])

    def step(state, inp):
        qd, kd, vv, at = inp
        o = jnp.einsum('bchd,bhde->bche', qd, state)
        state = at[..., None] * state + jnp.einsum('bchd,bche->bhde', kd, vv)
        return state, o
    xs = (jnp.moveaxis(q_dec, 1, 0), jnp.moveaxis(k_dec, 1, 0), jnp.moveaxis(v, 1, 0), jnp.moveaxis(a_tot, 1, 0))
    s_fin, o_inter = lax.scan(step, s0.astype(f32), xs)
    o = o_intra + jnp.moveaxis(o_inter, 0, 1)
    return o.reshape(b, n * c, GLA_HEADS, GLA_DV)[:, :l], s_fin


def _s5_combine(e1, e2):
    a1r, a1i, b1r, b1i = e1
    a2r, a2i, b2r, b2i = e2
    return (a2r * a1r - a2i * a1i, a2r * a1i + a2i * a1r,
            a2r * b1r - a2i * b1i + b2r, a2r * b1i + a2i * b1r + b2i)


def _s5_scan(u, h0, a_re, a_im, b_re, b_im, c_re, c_im, d, log_dt):
    f32 = jnp.float32
    b, l, _ = u.shape
    dt = jnp.exp(log_dt.astype(f32))[:, None]
    mag = jnp.exp(a_re * dt)
    abr, abi = mag * jnp.cos(a_im * dt), mag * jnp.sin(a_im * dt)
    den = a_re * a_re + a_im * a_im
    nr = abr - 1.0
    cr = (nr * a_re + abi * a_im) / den
    ci = (abi * a_re - nr * a_im) / den
    bbr = cr[..., None] * b_re - ci[..., None] * b_im
    bbi = cr[..., None] * b_im + ci[..., None] * b_re
    ug = u.reshape(b, l, S5_GROUPS, S5_GROUP_CH)
    xr = jnp.einsum('gpc,blgc->blgp', bbr, ug)
    xi = jnp.einsum('gpc,blgc->blgp', bbi, ug)
    h0r, h0i = h0[..., 0].astype(f32), h0[..., 1].astype(f32)
    xr = xr.at[:, 0].add(abr * h0r - abi * h0i)
    xi = xi.at[:, 0].add(abr * h0i + abi * h0r)
    ar = jnp.broadcast_to(abr, xr.shape)
    ai = jnp.broadcast_to(abi, xr.shape)
    _, _, hr, hi = lax.associative_scan(_s5_combine, (ar, ai, xr, xi), axis=1)
    y = jnp.einsum('gcp,blgp->blgc', c_re, hr) - jnp.einsum('gcp,blgp->blgc', c_im, hi)
    y = y.reshape(b, l, S5_WIDTH) + d.astype(f32) * u
    return y, jnp.stack([hr[:, -1], hi[:, -1]], axis=-1)


GLA_CHUNK_ROWS = 256


def _split_kv(kv, b, l):
    kv5 = kv.reshape(b, l, 3, 2, NSA_KV_HEADS, HEAD_DIM)
    return kv5[:, :, 0], kv5[:, :, 1], kv5[:, :, 2]


def _prompt_layer(h2d, ple2d, lw, b, l, final_norm):
    qx, kv, kb, vb, nz, gl, gt, su, sz = _in_proj_b(h2d, b, l, lw['norm'], lw['w_in_b'])
    kvc = _compress(kv, b, l, lw['cmp_pe'], lw['cmp_w1'], lw['cmp_w2'])
    o_cmp, sel = _cmp_select(qx, kvc, gt, b, l)
    o_slc = _flash(qx, kb, vb, sel, gt, b, l, "slc")
    o_win = _flash(qx, kb, vb, sel, gt, b, l, "win")
    new_cmp, new_slc, new_win = _split_kv(kv, b, l)
    win_state = new_win[:, l - min(WINDOW, l):]

    t = min(GLA_CHUNK_ROWS, l)
    o_gla, gla_state = _gla(gl, gt, lw, jnp.zeros((b, GLA_HEADS, GLA_DK, GLA_DV), F32), b, l // t, t, t)

    o_s5, hfin = _s5(su, sz, lw, jnp.zeros((2, b * S5_CH), F32), b, l)
    s5_state = jnp.transpose(hfin.reshape(2, b, S5_GROUPS, S5_STATE), (1, 2, 3, 0))

    h2d = _merge_call(h2d, o_cmp, o_slc, o_win, nz, o_gla, o_s5, True, ple2d, lw, final_norm, b)
    return h2d, (new_cmp, new_slc, win_state, gla_state, s5_state)


def _sample_layer(h2d, ple2d, lw, past, b, l, final_norm):
    f32 = jnp.float32
    n = b * l
    qx, kv, kb, vb, nz, gl, gt, su, sz = _in_proj_b(h2d, 1, n, lw['norm'], lw['w_in_b'])
    su, sz = [jnp.transpose(a.reshape(b, l, S5_WIDTH), (1, 0, 2)).reshape(l, b * S5_WIDTH) for a in (su, sz)]
    new_cmp, new_slc, new_win = _split_kv(kv, b, l)
    pt = past['page_table']
    pos0 = pt.shape[1] * past['cmp_t'].shape[-1]
    assert (pos0 + l) // CMP_STRIDE == pos0 // CMP_STRIDE and pos0 % CMP_STRIDE == 0
    assert past['win'].shape[1] == WINDOW <= pos0
    kvc = _paged_compress(past['cmp_t'], past['layer'], pt, lw['cmp_pe'], lw['cmp_w1'], lw['cmp_w2'])
    gated = _decode_nsa(past['slc_t'], past['win_t'], past['layer'], pt, kvc, qx.astype(f32), kv, gt, b, l)
    win_state = jnp.concatenate([past['win'], new_win], axis=1)[:, l:]

    seqs = GLA_SAMPLE_ROWS // l
    o_gla, gla_state = _gla(gl, gt, lw, past['gla'], b // seqs, 1, GLA_SAMPLE_ROWS, l)

    h0 = jnp.transpose(past['s5'].astype(f32), (3, 0, 1, 2)).reshape(2, b * S5_CH)
    o_s5, hfin = _s5(su, sz, lw, h0, b, l)
    s5_state = jnp.transpose(hfin.reshape(2, b, S5_GROUPS, S5_STATE), (1, 2, 3, 0))

    o_s5 = jnp.transpose(o_s5.reshape(l, b, S5_WIDTH), (1, 0, 2)).reshape(n, S5_WIDTH)
    h2d = _merge_call(h2d, gated[0], gated[1], gated[2], nz, o_gla, o_s5, False, ple2d, lw, final_norm, 1)
    return h2d, (new_cmp, new_slc, win_state, gla_state, s5_state)


GLA_SAMPLE_ROWS = 128


def kernel(x_prompt, x_sample, cache_cmp, cache_slc, cache_win, state_gla, state_s5, page_table,
           p_prompt, p_sample, norm_mix, w_in, w_out, cmp_pe, cmp_w1, cmp_w2, gla_wg, gla_bg, gla_norm,
           s5_a_re, s5_a_im, s5_b_re, s5_b_im, s5_c_re, s5_c_im, s5_d, s5_log_dt, s5_w_glu,
           ple_norm, ple_w_gate, ple_w_proj, final_norm):
    depth = w_in.shape[0]
    w_in_b = _relayout_w_in_b(w_in)
    cmp_t = jnp.transpose(cache_cmp, (0, 1, 3, 4, 5, 2))
    slc_t = jnp.transpose(cache_slc, (0, 1, 3, 4, 5, 2))
    win_t = jnp.transpose(cache_win, (0, 1, 3, 4, 5, 2))
    bp, lp, d = x_prompt.shape
    bs, ls, _ = x_sample.shape
    hp, hs = x_prompt.reshape(bp * lp, d), x_sample.reshape(bs * ls, d)
    st_p, st_s = [], []
    for i in range(depth):
        fin = final_norm if i == depth - 1 else None
        lw = {'norm': norm_mix[i], 'w_in_b': w_in_b[i], 'w_out': w_out[i],
              'cmp_pe': cmp_pe[i], 'cmp_w1': cmp_w1[i], 'cmp_w2': cmp_w2[i],
              'gla_wg': gla_wg[i], 'gla_bg': gla_bg[i], 'gla_norm': gla_norm[i],
              's5_a_re': s5_a_re[i], 's5_a_im': s5_a_im[i], 's5_b_re': s5_b_re[i], 's5_b_im': s5_b_im[i],
              's5_c_re': s5_c_re[i], 's5_c_im': s5_c_im[i], 's5_d': s5_d[i], 's5_log_dt': s5_log_dt[i],
              's5_w_glu': s5_w_glu[i], 'ple_norm': ple_norm[i], 'ple_w_gate': ple_w_gate[i],
              'ple_w_proj': ple_w_proj[i]}
        hp, sp = _prompt_layer(hp, p_prompt[i].reshape(bp * lp, PLE_DIM), lw, bp, lp, fin)
        past = {'page_table': page_table, 'win': cache_win[i], 'gla': state_gla[i], 's5': state_s5[i],
                'cmp_t': cmp_t, 'slc_t': slc_t, 'win_t': win_t, 'layer': i}
        hs, ss = _sample_layer(hs, p_sample[i].reshape(bs * ls, PLE_DIM), lw, past, bs, ls, fin)
        st_p.append(sp)
        st_s.append(ss)
    outs = [hp.reshape(bp, lp, d), hs.reshape(bs, ls, d)]
    for j in range(5):
        outs.append(jnp.stack([s[j] for s in st_p]))
        outs.append(jnp.stack([s[j] for s in st_s]))
    return tuple(outs)
```

```python
import functools
import math

import jax
import jax.numpy as jnp
from jax import lax
from jax.experimental import pallas as pl
from jax.experimental.pallas import tpu as pltpu

D_MODEL = 1024
PLE_DIM = 256
HEAD_DIM = 64
NSA_HEADS = 8
NSA_KV_HEADS = 2
NSA_HPG = NSA_HEADS // NSA_KV_HEADS
NSA_WIDTH = NSA_HEADS * HEAD_DIM
CMP_BLOCK = 32
CMP_STRIDE = 16
SLC_BLOCK = 64
N_SELECT = 16
WINDOW = 512
SPARSE_Q_BLOCK = 64
WIN_Q_BLOCK = 128
GLA_HEADS = 4
GLA_DK = 32
GLA_DV = 64
GLA_WIDTH = GLA_HEADS * GLA_DV
GLA_GATE_RANK = 16
GLA_GATE_TAU = 16.0
GLA_CHUNK = 32
S5_GROUPS = 16
S5_GROUP_CH = 16
S5_STATE = 64
S5_WIDTH = S5_GROUPS * S5_GROUP_CH
S5_CH = S5_GROUPS * S5_STATE
MIX_WIDTH = NSA_WIDTH + GLA_WIDTH + S5_WIDTH
KV_WIDTH = 6 * NSA_KV_HEADS * HEAD_DIM
RMS_EPS = 1e-6
NEG_BIG = 1e9
MASKED = -1e30

LANES = 128
SUBLANES = 8
VMEM_LIMIT = 48 * 1024 * 1024

F32 = jnp.float32
BF16 = jnp.bfloat16
ALIBI = tuple(2.0 ** (-8.0 * (h + 1) / NSA_HEADS) for h in range(NSA_HEADS))

_SRC = {}
_off = 0
for _name, _w in (("nq", NSA_WIDTH), ("nkv", KV_WIDTH), ("ngate", 3 * NSA_HEADS), ("nz", NSA_WIDTH),
                  ("gq", GLA_HEADS * GLA_DK), ("gk", GLA_HEADS * GLA_DK), ("gv", GLA_WIDTH),
                  ("glr", GLA_GATE_RANK), ("gz", GLA_WIDTH), ("su", S5_WIDTH), ("sz", S5_WIDTH)):
    _SRC[_name] = (_off, _w)
    _off += _w
IN_WIDTH = _off


def _cols(w, name):
    o, n = _SRC[name]
    return w[..., o:o + n]


_ORDER_A = ("nq", "nkv", "nz", "gq", "gk", "gv", "gz", "su", "sz", "ngate", "glr")
_DST_A = {}
_off = 0
for _name in _ORDER_A:
    _DST_A[_name] = (_off, _SRC[_name][1])
    _off += _SRC[_name][1]
IN_PAD_A = -(-_off // LANES) * LANES


def _relayout_w_in_a(w_in):
    w = jnp.concatenate([_cols(w_in, n) for n in _ORDER_A], axis=-1)
    w = jnp.pad(w, ((0, 0),) * (w.ndim - 1) + ((0, IN_PAD_A - w.shape[-1]),))
    return w.astype(BF16)


def _seg_a(u, name):
    o, w = _DST_A[name]
    return u[..., o:o + w]


QX_W = NSA_HEADS * LANES
GL_W = 2 * GLA_HEADS * GLA_DK + 2 * GLA_WIDTH
GT_W = LANES
_B_SEGS = (("qx", QX_W), ("kv", KV_WIDTH), ("nz", NSA_WIDTH), ("gl", GL_W), ("gt", GT_W),
           ("su", S5_WIDTH), ("sz", S5_WIDTH))
_B_OFF = {}
_off = 0
for _name, _w in _B_SEGS:
    _B_OFF[_name] = (_off, _w)
    _off += _w
IN_PAD_B = _off


def _relayout_w_in_b(w_in):
    d = w_in.shape[:-1]
    wq = _cols(w_in, "nq") * (HEAD_DIM ** -0.5)
    z = jnp.zeros(d + (HEAD_DIM,), w_in.dtype)
    qparts = []
    for h in range(NSA_HEADS):
        wh = wq[..., h * HEAD_DIM:(h + 1) * HEAD_DIM]
        qparts += [wh, z] if h < NSA_HPG else [z, wh]
    gt = jnp.concatenate([_cols(w_in, "ngate"), _cols(w_in, "glr")], axis=-1)
    gt = jnp.pad(gt, ((0, 0),) * len(d) + ((0, GT_W - gt.shape[-1]),))
    w = jnp.concatenate(qparts + [_cols(w_in, "nkv"), _cols(w_in, "nz"), _cols(w_in, "gq"), _cols(w_in, "gk"),
                                  _cols(w_in, "gv"), _cols(w_in, "gz"), gt, _cols(w_in, "su"), _cols(w_in, "sz")],
                        axis=-1)
    return w.astype(BF16)


def _lane_tile(x, k):
    return jnp.concatenate([x] * k, axis=1)


def _params(*sem):
    return pltpu.CompilerParams(dimension_semantics=sem, vmem_limit_bytes=VMEM_LIMIT)


def _norm_rows(x_ref, g_ref):
    x = x_ref[...]
    ms = jnp.mean(x * x, axis=-1, keepdims=True)
    return (x * lax.rsqrt(ms + RMS_EPS) * g_ref[...]).astype(BF16)


def _in_proj_a_kernel(x_ref, g_ref, w_ref, o_ref):
    o_ref[...] = jnp.dot(_norm_rows(x_ref, g_ref), w_ref[...], preferred_element_type=F32)


def _in_proj_a(x2d, g, w_bf16):
    n, d = x2d.shape
    tm = min(512, n)
    return pl.pallas_call(
        _in_proj_a_kernel,
        out_shape=jax.ShapeDtypeStruct((n, IN_PAD_A), F32),
        grid=(n // tm,),
        in_specs=[pl.BlockSpec((tm, d), lambda i: (i, 0)),
                  pl.BlockSpec((1, d), lambda i: (0, 0)),
                  pl.BlockSpec((d, IN_PAD_A), lambda i: (0, 0))],
        out_specs=pl.BlockSpec((tm, IN_PAD_A), lambda i: (i, 0)),
        compiler_params=_params("parallel"),
        name="in_proj_a",
    )(x2d, g.reshape(1, d), w_bf16)


def _in_proj_b_kernel(x_ref, g_ref, w_ref, qx_ref, kv_ref, kb_ref, vb_ref, nz_ref, gl_ref, gt_ref, su_ref, sz_ref):
    xn = _norm_rows(x_ref, g_ref)

    def seg(name):
        o, w = _B_OFF[name]
        return jnp.dot(xn, w_ref[:, o:o + w], preferred_element_type=F32)
    qx_ref[...] = seg("qx").astype(BF16)
    kv = seg("kv")
    kv_ref[...] = kv
    kvb = kv.astype(BF16)
    ones = jnp.ones((kv.shape[0], LANES), BF16)
    for br in range(3):
        kb_ref[:, br * LANES:(br + 1) * LANES] = kvb[:, 2 * br * LANES:(2 * br + 1) * LANES]
        vb_ref[:, 2 * br * LANES:(2 * br + 1) * LANES] = kvb[:, (2 * br + 1) * LANES:(2 * br + 2) * LANES]
        vb_ref[:, (2 * br + 1) * LANES:(2 * br + 2) * LANES] = ones
    nz_ref[...] = seg("nz")
    gl_ref[...] = seg("gl")
    gt_ref[...] = seg("gt")
    su_ref[...] = seg("su")
    sz_ref[...] = seg("sz")


def _in_proj_b(x2d, b, l, g, w_bf16):
    n, d = x2d.shape
    tm = min(512, l)
    nt = l // tm
    row = lambda bi, ti: (bi * nt + ti, 0)
    tcol = lambda bi, ti: (ti, bi)
    shapes = [((n, QX_W), BF16), ((n, KV_WIDTH), F32), ((n, 3 * LANES), BF16), ((n, 6 * LANES), BF16),
              ((n, NSA_WIDTH), F32), ((n, GL_W), F32), ((n, GT_W), F32),
              ((l, b * S5_WIDTH), F32), ((l, b * S5_WIDTH), F32)]
    out_specs = [pl.BlockSpec((tm, s[0][1]), row) for s in shapes[:7]] + \
                [pl.BlockSpec((tm, S5_WIDTH), tcol)] * 2
    return pl.pallas_call(
        _in_proj_b_kernel,
        out_shape=[jax.ShapeDtypeStruct(*s) for s in shapes],
        grid=(b, nt),
        in_specs=[pl.BlockSpec((tm, d), row),
                  pl.BlockSpec((1, d), lambda bi, ti: (0, 0)),
                  pl.BlockSpec((d, IN_PAD_B), lambda bi, ti: (0, 0))],
        out_specs=out_specs,
        compiler_params=_params("parallel", "parallel"),
        name="in_proj_b",
    )(x2d, g.reshape(1, d), w_bf16)


def _compress_kernel(x_ref, pe_ref, w1_ref, w2_ref, o_ref, *, n_ch):
    halves = [jnp.zeros((n_ch, LANES), F32)] * 2
    for p in range(CMP_STRIDE):
        xp = x_ref[pl.ds(p, n_ch, stride=CMP_STRIDE), :]
        for s in range(2):
            q = s * CMP_STRIDE + p
            halves[s] = halves[s] + jnp.dot((xp + pe_ref[0, q:q + 1, :]).astype(BF16), w1_ref[0, q],
                                            preferred_element_type=F32)
    second = pltpu.roll(halves[1], n_ch - 1, 0)
    hid = jax.nn.gelu(halves[0] + second).astype(BF16)
    o_ref[0, 0] = jnp.dot(hid, w2_ref[0], preferred_element_type=F32).astype(BF16)


def _compress_weights(pe, w1, w2):
    eye = jnp.eye(NSA_KV_HEADS, dtype=F32)
    pe2 = jnp.tile(pe, (1, 1, NSA_KV_HEADS))
    w1r = w1.reshape(2, CMP_BLOCK, HEAD_DIM, HEAD_DIM)
    w1bd = jnp.einsum('gh,kpde->kpgdhe', eye, w1r).reshape(2, CMP_BLOCK, LANES, LANES).astype(BF16)
    w2bd = jnp.einsum('gh,kde->kgdhe', eye, w2).reshape(2, LANES, LANES).astype(BF16)
    return pe2, w1bd, w2bd


def _page_copies(pt_ref, pages_ref, buf, sem, layer, bi, part, slot, n_pages):
    out = []
    for j in range(n_pages):
        src = pages_ref.at[layer, pt_ref[bi, j]]
        out.append(pltpu.make_async_copy(src if part is None else src.at[part], buf.at[slot, j], sem.at[slot]))
    return out


def _paged_compress_kernel(pt_ref, pages_ref, pe_ref, w1_ref, w2_ref, o_ref, buf, x_s, sem,
                           *, layer, n_pages, n_steps, page):
    bi, s = pl.program_id(0), pl.program_id(1)
    step = bi * 2 + s
    slot = step % 2

    @pl.when(step == 0)
    def _():
        for c in _page_copies(pt_ref, pages_ref, buf, sem, layer, bi, s, slot, n_pages):
            c.start()

    @pl.when(step + 1 < n_steps)
    def _():
        nxt = step + 1
        for c in _page_copies(pt_ref, pages_ref, buf, sem, layer, nxt // 2, nxt % 2, 1 - slot, n_pages):
            c.start()
    for c in _page_copies(pt_ref, pages_ref, buf, sem, layer, bi, s, slot, n_pages):
        c.wait()

    def to_rows(j, carry):
        x_s[pl.ds(pl.multiple_of(j * page, page), page), :] = jnp.transpose(buf[slot, j].reshape(LANES, page))
        return carry
    lax.fori_loop(0, n_pages, to_rows, 0, unroll=8)
    _compress_kernel(x_s, pe_ref, w1_ref, w2_ref, o_ref, n_ch=n_pages * page // CMP_STRIDE)


def _paged_compress(pages_t, layer, page_table, pe, w1, w2):
    b, n_pages = page_table.shape
    page = pages_t.shape[-1]
    t = n_pages * page
    n_ch = t // CMP_STRIDE
    pe2, w1bd, w2bd = _compress_weights(pe, w1, w2)
    return pl.pallas_call(
        functools.partial(_paged_compress_kernel, layer=layer, n_pages=n_pages, n_steps=2 * b, page=page),
        out_shape=jax.ShapeDtypeStruct((2, b, n_ch, LANES), BF16),
        grid_spec=pltpu.PrefetchScalarGridSpec(
            num_scalar_prefetch=1,
            grid=(b, 2),
            in_specs=[pl.BlockSpec(memory_space=pl.ANY),
                      pl.BlockSpec((1, CMP_BLOCK, LANES), lambda bi, s, pt: (s, 0, 0)),
                      pl.BlockSpec((1, CMP_BLOCK, LANES, LANES), lambda bi, s, pt: (s, 0, 0, 0)),
                      pl.BlockSpec((1, LANES, LANES), lambda bi, s, pt: (s, 0, 0))],
            out_specs=pl.BlockSpec((1, 1, n_ch, LANES), lambda bi, s, pt: (s, bi, 0, 0)),
            scratch_shapes=[pltpu.VMEM((2, n_pages, NSA_KV_HEADS, HEAD_DIM, page), F32),
                            pltpu.VMEM((t, LANES), F32), pltpu.SemaphoreType.DMA((2,))]),
        compiler_params=_params("arbitrary", "arbitrary"),
        name="nsa_paged_compress",
    )(page_table, pages_t, pe2, w1bd, w2bd)


def _compress(kv, b, t, pe, w1, w2):
    n_ch = t // CMP_STRIDE
    pe2, w1bd, w2bd = _compress_weights(pe, w1, w2)
    return pl.pallas_call(
        functools.partial(_compress_kernel, n_ch=n_ch),
        out_shape=jax.ShapeDtypeStruct((2, b, n_ch, LANES), BF16),
        grid=(b, 2),
        in_specs=[pl.BlockSpec((t, LANES), lambda bi, s: (bi, s)),
                  pl.BlockSpec((1, CMP_BLOCK, LANES), lambda bi, s: (s, 0, 0)),
                  pl.BlockSpec((1, CMP_BLOCK, LANES, LANES), lambda bi, s: (s, 0, 0, 0)),
                  pl.BlockSpec((1, LANES, LANES), lambda bi, s: (s, 0, 0))],
        out_specs=pl.BlockSpec((1, 1, n_ch, LANES), lambda bi, s: (s, bi, 0, 0)),
        compiler_params=_params("parallel", "parallel"),
        name="nsa_compress",
    )(kv, pe2, w1bd, w2bd)


def _split3(x):
    hi = x.astype(BF16)
    r = x - hi.astype(F32)
    mid = r.astype(BF16)
    lo = (r - mid.astype(F32)).astype(BF16)
    return hi, mid, lo


def _cmp_sel_kernel(qx_ref, kc_ref, vc_ref, gt_ref, m2_ref, o_ref, sel_ref, sc_s, *, tq, nc):
    q0 = pl.program_id(1) * tq
    qpos = q0 + lax.broadcasted_iota(jnp.int32, (tq, 1), 0)
    cend = lax.broadcasted_iota(jnp.int32, (1, nc), 1) * CMP_STRIDE + (CMP_BLOCK - 1)
    dist = (qpos - cend).astype(F32)
    ok = dist >= 0.0
    kc = kc_ref[0, 0]
    vc = vc_ref[0, 0]
    gts = jax.nn.sigmoid(gt_ref[...])
    imps = []
    for g in range(NSA_KV_HEADS):
        imp = jnp.zeros((tq, nc), F32)
        for j in range(NSA_HPG):
            h = g * NSA_HPG + j
            s = lax.dot_general(qx_ref[:, h * LANES:(h + 1) * LANES], kc, (((1,), (1,)), ((), ())),
                                preferred_element_type=F32)
            s = jnp.where(ok, s - ALIBI[h] * dist, MASKED)
            e = jnp.where(ok, jnp.exp(s - jnp.max(s, axis=-1, keepdims=True)), 0.0)
            p = e / jnp.maximum(jnp.sum(e, axis=-1, keepdims=True), 1e-30)
            imp = imp + p
            o = jnp.dot(p.astype(BF16), vc, preferred_element_type=F32)
            o_ref[:, h * HEAD_DIM:(h + 1) * HEAD_DIM] = (o[:, g * HEAD_DIM:(g + 1) * HEAD_DIM]
                                                         * gts[:, 3 * h:3 * h + 1])
        imps.append(imp)
    m2 = m2_ref[...]
    score = sum(jnp.dot(piece, m2, preferred_element_type=F32) for piece in _split3(jnp.concatenate(imps, axis=1)))
    lane = lax.broadcasted_iota(jnp.int32, (1, LANES), 1)
    blk = lane & (SLC_BLOCK - 1)
    cur = qpos >> 6
    forced = (blk == 0) | (blk == cur) | (blk == cur - 1)
    elig = (blk * SLC_BLOCK) <= qpos
    sc_s[...] = jnp.transpose(jnp.where(elig, jnp.where(forced, NEG_BIG, score), -NEG_BIG))
    sub = lax.broadcasted_iota(jnp.int32, (SUBLANES, 1), 0)
    kept = []
    for g in range(NSA_KV_HEADS):
        for r in range(SLC_BLOCK // SUBLANES):
            mine = sc_s[g * SLC_BLOCK + r * SUBLANES:g * SLC_BLOCK + (r + 1) * SUBLANES, :]
            cnt = jnp.zeros(mine.shape, jnp.int32)
            for i in range(SLC_BLOCK):
                cand = sc_s[g * SLC_BLOCK + i:g * SLC_BLOCK + i + 1, :]
                if i < r * SUBLANES:
                    inc = jnp.where(cand >= mine, 1, 0)
                elif i >= (r + 1) * SUBLANES:
                    inc = jnp.where(cand > mine, 1, 0)
                else:
                    tie = jnp.where(sub > i - r * SUBLANES, 1, 0)
                    inc = jnp.where(cand > mine, 1, jnp.where(cand == mine, tie, 0))
                cnt = cnt + inc
            kept.append(jnp.where((cnt < N_SELECT) & (mine > -0.5 * NEG_BIG), 1.0, 0.0))
    sel_ref[...] = jnp.transpose(jnp.concatenate(kept, axis=0)).astype(BF16)


def _slc_map_matrix(nc, n_cmp):
    n = jnp.arange(nc)[:, None]
    j = jnp.arange(SLC_BLOCK)[None, :]
    ratio = SLC_BLOCK // CMP_STRIDE
    m = ((n >= ratio * j) & (n <= ratio * j + ratio - 1)).astype(F32) + \
        ((n >= ratio * j - 1) & (n <= ratio * j + ratio - 2)).astype(F32)
    m = jnp.where(n < n_cmp, m, 0.0)
    z = jnp.zeros_like(m)
    return jnp.concatenate([jnp.concatenate([m, z], 1), jnp.concatenate([z, m], 1)], 0).astype(BF16)


def _cmp_select(qx, kvc, gt, b, l):
    nc = kvc.shape[2]
    tq = min(256, l)
    nt = l // tq
    row = lambda bi, ti: (bi * nt + ti, 0)
    return pl.pallas_call(
        functools.partial(_cmp_sel_kernel, tq=tq, nc=nc),
        out_shape=[jax.ShapeDtypeStruct((b * l, NSA_WIDTH), F32), jax.ShapeDtypeStruct((b * l, LANES), BF16)],
        grid=(b, nt),
        in_specs=[pl.BlockSpec((tq, QX_W), row),
                  pl.BlockSpec((1, 1, nc, LANES), lambda bi, ti: (0, bi, 0, 0)),
                  pl.BlockSpec((1, 1, nc, LANES), lambda bi, ti: (1, bi, 0, 0)),
                  pl.BlockSpec((tq, GT_W), row),
                  pl.BlockSpec((2 * nc, LANES), lambda bi, ti: (0, 0))],
        out_specs=[pl.BlockSpec((tq, NSA_WIDTH), row), pl.BlockSpec((tq, LANES), row)],
        scratch_shapes=[pltpu.VMEM((LANES, tq), F32)],
        compiler_params=_params("parallel", "parallel"),
        name="nsa_cmp_select",
    )(qx, kvc, kvc, gt, _slc_map_matrix(nc, nc - 1))


FLASH_PARTS = 4


def _flash_kernel(fl_ref, qx_ref, k_ref, kf_ref, v_ref, sel_ref, gt_ref, o_ref, qs_s, acc_s, m_s,
                  *, tq, tk, mode, gate_col, nt, nkt):
    i = pl.program_id(1)
    q0 = i * tq
    rows = NSA_HPG * tq
    d0 = (lax.broadcasted_iota(jnp.int32, (tq, tk), 0) - lax.broadcasted_iota(jnp.int32, (tq, tk), 1)).astype(F32)
    gts = jax.nn.sigmoid(gt_ref[...])
    lane = lax.broadcasted_iota(jnp.int32, (1, LANES), 1)

    def tile(g, k0, causal, winlow):
        kt = jnp.concatenate([k_ref[pl.ds(k0, tk), :], kf_ref[pl.ds(k0, tk), :]], axis=1)
        vt = v_ref[pl.ds(k0, tk), :]
        c = (q0 - k0).astype(F32)
        bias = None
        if causal:
            bias = jnp.where(d0 + c >= 0.0, 0.0, MASKED)
        if winlow:
            bias = jnp.where(d0 + c < float(WINDOW), bias, MASKED)
        for part in range(FLASH_PARTS):
            rpp = rows // FLASH_PARTS
            rs = slice(part * rpp, (part + 1) * rpp)
            s = lax.dot_general(qs_s[g, rs, :], kt, (((1,), (1,)), ((), ())), preferred_element_type=F32)
            if bias is not None and rpp >= tq:
                s = (s.reshape(-1, tq, tk) + bias[None]).reshape(-1, tk)
            elif bias is not None:
                s = s + bias[(part * rpp) % tq:(part * rpp) % tq + rpp]
            m_prev = m_s[g, rs, :]
            m_new = jnp.maximum(m_prev, jnp.max(s, axis=1, keepdims=True))
            p = jnp.exp(s - _lane_tile(m_new, tk // LANES))
            alpha = jnp.exp(m_prev - m_new)
            acc_s[g, rs, :] = _lane_tile(alpha, 2) * acc_s[g, rs, :] + jnp.dot(p.astype(BF16), vt,
                                                                                  preferred_element_type=F32)
            m_s[g, rs, :] = m_new

    for g in range(NSA_KV_HEADS):
        if mode == "slc":
            blockbias = (sel_ref[...].astype(F32) - 1.0) * (-MASKED)
            if g == 1:
                blockbias = pltpu.roll(blockbias, SLC_BLOCK, 1)
        else:
            blockbias = jnp.zeros((tq, LANES), F32)
        for jh in range(NSA_HPG):
            h = g * NSA_HPG + jh
            feat = jnp.where(lane < SLC_BLOCK, blockbias,
                             jnp.where(lane == SLC_BLOCK, ALIBI[h] * SLC_BLOCK,
                                       jnp.where(lane == SLC_BLOCK + 1, ALIBI[h], 0.0)))
            qs_s[g, jh * tq:(jh + 1) * tq, 0:LANES] = qx_ref[:, h * LANES:(h + 1) * LANES]
            qs_s[g, jh * tq:(jh + 1) * tq, LANES:2 * LANES] = feat.astype(BF16)
        m_s[g] = jnp.full(m_s.shape[1:], MASKED, F32)
        acc_s[g] = jnp.zeros(acc_s.shape[1:], F32)
        if mode == "slc":
            last = (q0 + tq - 1) // tk

            def body(j, carry):
                @pl.when(fl_ref[((pl.program_id(0) * nt + i) * NSA_KV_HEADS + g) * nkt + j] > 0)
                def _():
                    tile(g, pl.multiple_of(j * tk, tk), False, False)
                return carry
            lax.fori_loop(0, last, body, 0)
            tile(g, pl.multiple_of(last * tk, tk), True, False)
        else:
            tile(g, pl.multiple_of(jnp.maximum(i - WINDOW // tq, 0) * tq, tq), True, True)
        for jh in range(NSA_HPG):
            h = g * NSA_HPG + jh
            c = 3 * h + gate_col
            a = acc_s[g, jh * tq:(jh + 1) * tq, :]
            o = a[:, g * HEAD_DIM:(g + 1) * HEAD_DIM] / a[:, LANES + g * HEAD_DIM:LANES + (g + 1) * HEAD_DIM]
            o_ref[:, h * HEAD_DIM:(h + 1) * HEAD_DIM] = o * gts[:, c:c + 1]


def _key_position_features(l):
    assert l <= SLC_BLOCK * SLC_BLOCK
    pos = jnp.arange(l)[:, None]
    lane = jnp.arange(LANES)[None, :]
    f = jnp.where(lane < SLC_BLOCK, (lane == pos // SLC_BLOCK).astype(F32),
                  jnp.where(lane == SLC_BLOCK, (pos // SLC_BLOCK).astype(F32),
                            jnp.where(lane == SLC_BLOCK + 1, (pos % SLC_BLOCK).astype(F32), 0.0)))
    return f.astype(BF16)


def _flash(qx, kb, vb, sel, gt, b, l, mode):
    tq = min(256, l)
    tk = 2 * tq if mode == "slc" else WINDOW + tq
    assert l % tk == 0 if mode == "slc" else (l >= tk and WINDOW % tq == 0)
    nt = l // tq
    rows = NSA_HPG * tq
    row = lambda bi, ti, fl: (bi * nt + ti, 0)
    branch = 1 if mode == "slc" else 2
    nkt = l // tk if mode == "slc" else 1
    if mode == "slc":
        per_tile = tk // SLC_BLOCK
        occ = sel.astype(F32).reshape(b * nt, tq, NSA_KV_HEADS, SLC_BLOCK)[..., :nkt * per_tile]
        occ = occ.reshape(b * nt, tq, NSA_KV_HEADS, nkt, per_tile).max(axis=(1, 4))
        flags = (occ > 0.5).astype(jnp.int32).reshape(b * nt * NSA_KV_HEADS * nkt)
    else:
        flags = jnp.ones((1,), jnp.int32)
    return pl.pallas_call(
        functools.partial(_flash_kernel, tq=tq, tk=tk, mode=mode, gate_col=branch, nt=nt, nkt=nkt),
        out_shape=jax.ShapeDtypeStruct((b * l, NSA_WIDTH), F32),
        grid_spec=pltpu.PrefetchScalarGridSpec(
            num_scalar_prefetch=1,
            grid=(b, nt),
            in_specs=[pl.BlockSpec((tq, QX_W), row),
                      pl.BlockSpec((l, LANES), lambda bi, ti, fl: (bi, branch)),
                      pl.BlockSpec((l, LANES), lambda bi, ti, fl: (0, 0)),
                      pl.BlockSpec((l, 2 * LANES), lambda bi, ti, fl: (bi, branch)),
                      pl.BlockSpec((tq, LANES), row),
                      pl.BlockSpec((tq, GT_W), row)],
            out_specs=pl.BlockSpec((tq, NSA_WIDTH), row),
            scratch_shapes=[pltpu.VMEM((NSA_KV_HEADS, rows, 2 * LANES), BF16),
                            pltpu.VMEM((NSA_KV_HEADS, rows, 2 * LANES), F32),
                            pltpu.VMEM((NSA_KV_HEADS, rows, LANES), F32)]),
        compiler_params=_params("parallel", "parallel"),
        name="nsa_" + mode,
    )(flags, qx, kb, _key_position_features(l), vb, sel, gt)


def _decode_nsa_kernel(pt_ref, slc_ref, win_ref, kc_ref, vc_ref, q_ref, kvn_ref, gt_ref, m2_ref,
                       oc_ref, os_ref, ow_ref, buf, sem, *, layer, n_pages, page, nb, l, n_slc, nsp):
    bi = pl.program_id(0)
    slot = bi % 2
    pos0 = n_pages * page
    nc = kc_ref.shape[2]
    rows = NSA_HPG * l

    @pl.when(bi == 0)
    def _():
        for c in _page_copies(pt_ref, slc_ref, buf, sem, layer, bi, None, slot, n_pages):
            c.start()

    @pl.when(bi + 1 < nb)
    def _():
        for c in _page_copies(pt_ref, slc_ref, buf, sem, layer, bi + 1, None, 1 - slot, n_pages):
            c.start()

    qf = q_ref[...]
    kvn = kvn_ref[...]
    gts = jax.nn.sigmoid(gt_ref[...])
    t_col = lax.broadcasted_iota(jnp.int32, (rows, 1), 0) % l
    t_f = t_col.astype(F32)
    lane = lax.broadcasted_iota(jnp.int32, (1, LANES), 1)
    lane_f = lane.astype(F32)

    def pad_rows(x):
        return jnp.concatenate([x, jnp.zeros((LANES - l, x.shape[1]), x.dtype)], axis=0)

    def group_q(g):
        q = jnp.concatenate([qf[:, (g * NSA_HPG + j) * LANES:(g * NSA_HPG + j + 1) * LANES]
                             for j in range(NSA_HPG)], axis=0).astype(BF16)
        sl = jnp.concatenate([jnp.full((l, 1), ALIBI[g * NSA_HPG + j], F32) for j in range(NSA_HPG)], axis=0)
        return q, sl

    def new_keys(q, sl, branch, tc=t_col, tf=t_f):
        kn = pad_rows(kvn[:, 2 * branch * LANES:(2 * branch + 1) * LANES]).astype(BF16)
        vn = pad_rows(kvn[:, (2 * branch + 1) * LANES:(2 * branch + 2) * LANES]).astype(BF16)
        s = lax.dot_general(q, kn, (((1,), (1,)), ((), ())), preferred_element_type=F32)
        s = jnp.where(lane <= tc, s - sl * (tf - lane_f), MASKED)
        return s, vn

    def write(o_ref, o, den, g, col):
        for j in range(NSA_HPG):
            h = g * NSA_HPG + j
            c = 3 * h + col
            oh = o[j * l:(j + 1) * l, g * HEAD_DIM:(g + 1) * HEAD_DIM] / den[j * l:(j + 1) * l]
            o_ref[:, h * HEAD_DIM:(h + 1) * HEAD_DIM] = oh * gts[:, c:c + 1]

    cend = (lax.broadcasted_iota(jnp.int32, (1, nc), 1) * CMP_STRIDE + (CMP_BLOCK - 1)).astype(F32)
    imps = []
    qs = []
    for g in range(NSA_KV_HEADS):
        q, sl = group_q(g)
        qs.append((q, sl))
        dist = (t_f + float(pos0)) - cend
        ok = dist >= 0.0
        s = lax.dot_general(q, kc_ref[0, 0], (((1,), (1,)), ((), ())), preferred_element_type=F32)
        s = jnp.where(ok, s - sl * dist, MASKED)
        e = jnp.where(ok, jnp.exp(s - jnp.max(s, axis=-1, keepdims=True)), 0.0)
        p = e / jnp.maximum(jnp.sum(e, axis=-1, keepdims=True), 1e-30)
        write(oc_ref, jnp.dot(p.astype(BF16), vc_ref[0, 0], preferred_element_type=F32),
              jnp.ones((rows, 1), F32), g, 0)
        imps.append(sum(p[j * l:(j + 1) * l] for j in range(NSA_HPG)))
    m2 = m2_ref[...]
    score = sum(jnp.dot(piece, m2, preferred_element_type=F32) for piece in _split3(jnp.concatenate(imps, axis=1)))
    lane2 = lax.broadcasted_iota(jnp.int32, (1, NSA_KV_HEADS * nsp), 1)
    blk = lane2 & (nsp - 1)
    qpos = pos0 + lax.broadcasted_iota(jnp.int32, (l, 1), 0)
    cur = qpos // SLC_BLOCK
    forced = (blk == 0) | (blk == cur) | (blk == cur - 1)
    elig = ((blk * SLC_BLOCK) <= qpos) & (blk < n_slc)
    sc = jnp.where(elig, jnp.where(forced, NEG_BIG, score), -NEG_BIG)
    cnt = jnp.zeros(sc.shape, jnp.int32)
    for i in range(n_slc):
        col = jnp.where(lane2 < nsp, sc[:, i:i + 1], sc[:, nsp + i:nsp + i + 1])
        cnt = cnt + jnp.where(col > sc, 1, jnp.where(col == sc, jnp.where(blk > i, 1, 0), 0))
    sel = jnp.where((cnt < N_SELECT) & (sc > -0.5 * NEG_BIG), 1.0, 0.0)

    w = win_ref.shape[-1]
    wpos = lax.broadcasted_iota(jnp.int32, (1, w), 1)
    for g in range(NSA_KV_HEADS):
        q, sl = qs[g]
        kt = win_ref[0, 0, 0].reshape(LANES, w).astype(BF16)
        vt = win_ref[0, 0, 1].reshape(LANES, w).astype(BF16)
        dist = (t_f + float(w)) - wpos.astype(F32)
        s = jnp.dot(q, kt, preferred_element_type=F32)
        s = jnp.where(dist < float(WINDOW), s - sl * dist, MASKED)
        sn, vn = new_keys(q, sl, 2)
        m = jnp.maximum(jnp.max(s, axis=-1, keepdims=True), jnp.max(sn, axis=-1, keepdims=True))
        p, pn = jnp.exp(s - m), jnp.exp(sn - m)
        o = (lax.dot_general(p.astype(BF16), vt, (((1,), (1,)), ((), ())), preferred_element_type=F32)
             + jnp.dot(pn.astype(BF16), vn, preferred_element_type=F32))
        write(ow_ref, o, jnp.sum(p, axis=-1, keepdims=True) + jnp.sum(pn, axis=-1, keepdims=True), g, 2)

    for c in _page_copies(pt_ref, slc_ref, buf, sem, layer, bi, None, slot, n_pages):
        c.wait()
    half = page // SLC_BLOCK
    q = jnp.concatenate([qs[g][0] for g in range(NSA_KV_HEADS)], axis=0)
    sl = jnp.concatenate([qs[g][1] for g in range(NSA_KV_HEADS)], axis=0)
    t2c = jnp.concatenate([t_col] * NSA_KV_HEADS, axis=0)
    t2f = jnp.concatenate([t_f] * NSA_KV_HEADS, axis=0)
    ss = []
    for j in range(n_pages):
        kt = buf[slot, j, 0].reshape(LANES, page).astype(BF16)
        s = jnp.dot(q, kt, preferred_element_type=F32)
        keeps = []
        for g in range(NSA_KV_HEADS):
            keep = sel[:, g * nsp + half * j:g * nsp + half * j + 1]
            for hb in range(1, half):
                keep = jnp.where(lane < hb * SLC_BLOCK, keep, sel[:, g * nsp + half * j + hb:g * nsp + half * j + hb + 1])
            keeps += [keep] * NSA_HPG
        dist = (t2f + float(pos0 - j * page)) - lane_f
        ss.append(jnp.where(jnp.concatenate(keeps, axis=0) > 0.5, s - sl * dist, MASKED))
    sn, vn = new_keys(q, sl, 1, t2c, t2f)
    mx = sn
    for s in ss:
        mx = jnp.maximum(mx, s)
    m = jnp.max(mx, axis=-1, keepdims=True)
    pn = jnp.exp(sn - m)
    psum = pn
    o = jnp.dot(pn.astype(BF16), vn, preferred_element_type=F32)
    for j in range(n_pages):
        p = jnp.exp(ss[j] - m)
        psum = psum + p
        vt = buf[slot, j, 1].reshape(LANES, page).astype(BF16)
        o = o + lax.dot_general(p.astype(BF16), vt, (((1,), (1,)), ((), ())), preferred_element_type=F32)
    den = jnp.sum(psum, axis=-1, keepdims=True)
    for g in range(NSA_KV_HEADS):
        write(os_ref, o[g * rows:(g + 1) * rows], den[g * rows:(g + 1) * rows], g, 1)


def _decode_nsa(slc_t, win_t, layer, page_table, kvc, qf, kvn, gt, b, l):
    n_pages, page = page_table.shape[1], slc_t.shape[-1]
    pos0 = n_pages * page
    nc = kvc.shape[2]
    n_slc = -(-(pos0 + l) // SLC_BLOCK)
    nsp = 1 << (n_slc - 1).bit_length()
    assert pos0 % SLC_BLOCK == 0 and l <= SLC_BLOCK and page % SLC_BLOCK == 0 and l % 8 == 0
    n = jnp.arange(nc)[:, None]
    jj = jnp.arange(nsp)[None, :]
    ratio = SLC_BLOCK // CMP_STRIDE
    m = ((n >= ratio * jj) & (n <= ratio * jj + ratio - 1)).astype(F32) + \
        ((n >= ratio * jj - 1) & (n <= ratio * jj + ratio - 2)).astype(F32)
    m = jnp.where((n < nc - 1) & (jj < n_slc), m, 0.0)
    z = jnp.zeros_like(m)
    m2 = jnp.concatenate([jnp.concatenate([m, z], 1), jnp.concatenate([z, m], 1)], 0).astype(BF16)
    row = lambda bi, pt: (bi, 0)
    w = win_t.shape[-1]
    out = jax.ShapeDtypeStruct((b * l, NSA_WIDTH), F32)
    return pl.pallas_call(
        functools.partial(_decode_nsa_kernel, layer=layer, n_pages=n_pages, page=page, nb=b, l=l, n_slc=n_slc,
                          nsp=nsp),
        out_shape=[out, out, out],
        grid_spec=pltpu.PrefetchScalarGridSpec(
            num_scalar_prefetch=1,
            grid=(b,),
            in_specs=[pl.BlockSpec(memory_space=pl.ANY),
                      pl.BlockSpec((1, 1, 2, NSA_KV_HEADS, HEAD_DIM, w), lambda bi, pt: (layer, bi, 0, 0, 0, 0)),
                      pl.BlockSpec((1, 1, nc, LANES), lambda bi, pt: (0, bi, 0, 0)),
                      pl.BlockSpec((1, 1, nc, LANES), lambda bi, pt: (1, bi, 0, 0)),
                      pl.BlockSpec((l, QX_W), row), pl.BlockSpec((l, KV_WIDTH), row), pl.BlockSpec((l, GT_W), row),
                      pl.BlockSpec((NSA_KV_HEADS * nc, NSA_KV_HEADS * nsp), lambda bi, pt: (0, 0))],
            out_specs=[pl.BlockSpec((l, NSA_WIDTH), row)] * 3,
            scratch_shapes=[pltpu.VMEM((2, n_pages, 2, NSA_KV_HEADS, HEAD_DIM, page), F32),
                            pltpu.SemaphoreType.DMA((2,))]),
        compiler_params=_params("arbitrary"),
        name="nsa_decode",
    )(page_table, slc_t, win_t, kvc, kvc, qf, kvn, gt, m2)


def _s5_kernel(u_ref, z_ref, bre_ref, bim_ref, cre_ref, cim_ref, ab_ref, d_ref, wglu_ref, h0_ref,
               o_ref, hout_ref, xr_s, xi_s, h_s, *, nb, tc):
    @pl.when(pl.program_id(0) == 0)
    def _():
        h_s[...] = h0_ref[...]
    bre, bim = bre_ref[...], bim_ref[...]
    for b in range(nb):
        ub = u_ref[:, b * S5_WIDTH:(b + 1) * S5_WIDTH].astype(BF16)
        xr_s[:, b * S5_CH:(b + 1) * S5_CH] = jnp.dot(ub, bre, preferred_element_type=F32)
        xi_s[:, b * S5_CH:(b + 1) * S5_CH] = jnp.dot(ub, bim, preferred_element_type=F32)
    abr = ab_ref[0:1, :]
    abi = ab_ref[1:2, :]

    def step(t, carry):
        hr, hi = carry
        nr = abr * hr - abi * hi + xr_s[pl.ds(t, 1), :]
        ni = abr * hi + abi * hr + xi_s[pl.ds(t, 1), :]
        xr_s[pl.ds(t, 1), :] = nr
        xi_s[pl.ds(t, 1), :] = ni
        return nr, ni
    hr, hi = lax.fori_loop(0, tc, step, (h_s[0:1, :], h_s[1:2, :]), unroll=8)
    h_s[0:1, :] = hr
    h_s[1:2, :] = hi
    hout_ref[...] = h_s[...]
    cre, cim, wglu = cre_ref[...], cim_ref[...], wglu_ref[...]
    for b in range(nb):
        u = u_ref[:, b * S5_WIDTH:(b + 1) * S5_WIDTH]
        y = (jnp.dot(xr_s[:, b * S5_CH:(b + 1) * S5_CH].astype(BF16), cre, preferred_element_type=F32)
             - jnp.dot(xi_s[:, b * S5_CH:(b + 1) * S5_CH].astype(BF16), cim, preferred_element_type=F32)
             + d_ref[...] * u)
        gl = jnp.dot(jax.nn.gelu(y).astype(BF16), wglu, preferred_element_type=F32)
        z = z_ref[:, b * S5_WIDTH:(b + 1) * S5_WIDTH]
        o_ref[:, b * S5_WIDTH:(b + 1) * S5_WIDTH] = (gl[:, :S5_WIDTH] * jax.nn.sigmoid(gl[:, S5_WIDTH:])
                                                       * (z * jax.nn.sigmoid(z)))


def _s5_discretise(lw):
    a_re, a_im = lw['s5_a_re'].astype(F32), lw['s5_a_im'].astype(F32)
    b_re, b_im = lw['s5_b_re'].astype(F32), lw['s5_b_im'].astype(F32)
    dt = jnp.exp(lw['s5_log_dt'].astype(F32))[:, None]
    mag = jnp.exp(a_re * dt)
    abr, abi = mag * jnp.cos(a_im * dt), mag * jnp.sin(a_im * dt)
    den = a_re * a_re + a_im * a_im
    nr = abr - 1.0
    cr = (nr * a_re + abi * a_im) / den
    ci = (abi * a_re - nr * a_im) / den
    bbr = cr[..., None] * b_re - ci[..., None] * b_im
    bbi = cr[..., None] * b_im + ci[..., None] * b_re
    eye = jnp.eye(S5_GROUPS, dtype=F32)
    bd_in = lambda m: jnp.einsum('gh,gpc->gchp', eye, m).reshape(S5_WIDTH, S5_CH).astype(BF16)
    bd_out = lambda m: jnp.einsum('gh,gcp->gphc', eye, m.astype(F32)).reshape(S5_CH, S5_WIDTH).astype(BF16)
    ab = jnp.stack([abr.reshape(S5_CH), abi.reshape(S5_CH)])
    return bd_in(bbr), bd_in(bbi), bd_out(lw['s5_c_re']), bd_out(lw['s5_c_im']), ab


def _s5(su, sz, lw, h0, nb, l):
    bre, bim, cre, cim, ab = _s5_discretise(lw)
    tc = min(256, l)
    const = lambda t: (0, 0)
    return pl.pallas_call(
        functools.partial(_s5_kernel, nb=nb, tc=tc),
        out_shape=[jax.ShapeDtypeStruct((l, nb * S5_WIDTH), F32), jax.ShapeDtypeStruct((2, nb * S5_CH), F32)],
        grid=(l // tc,),
        in_specs=[pl.BlockSpec((tc, nb * S5_WIDTH), lambda t: (t, 0)),
                  pl.BlockSpec((tc, nb * S5_WIDTH), lambda t: (t, 0)),
                  pl.BlockSpec((S5_WIDTH, S5_CH), const), pl.BlockSpec((S5_WIDTH, S5_CH), const),
                  pl.BlockSpec((S5_CH, S5_WIDTH), const), pl.BlockSpec((S5_CH, S5_WIDTH), const),
                  pl.BlockSpec((2, nb * S5_CH), const), pl.BlockSpec((1, S5_WIDTH), const),
                  pl.BlockSpec((S5_WIDTH, 2 * S5_WIDTH), const), pl.BlockSpec((2, nb * S5_CH), const)],
        out_specs=[pl.BlockSpec((tc, nb * S5_WIDTH), lambda t: (t, 0)), pl.BlockSpec((2, nb * S5_CH), const)],
        scratch_shapes=[pltpu.VMEM((tc, nb * S5_CH), F32), pltpu.VMEM((tc, nb * S5_CH), F32),
                        pltpu.VMEM((2, nb * S5_CH), F32)],
        compiler_params=_params("arbitrary"),
        name="s5",
    )(su, sz, bre, bim, cre, cim, jnp.tile(ab, (1, nb)), lw['s5_d'].astype(F32).reshape(1, S5_WIDTH),
      lw['s5_w_glu'].astype(BF16), h0)


GLA_QK = GLA_HEADS * GLA_DK


def _gla_kernel(gl_ref, gt_ref, wg_ref, bg_ref, gn_ref, pall_ref, lm_ref, seg_ref, s0_ref, o_ref, sout_ref, st_s,
                *, r, t, nl):
    nseq = r // t

    @pl.when(pl.program_id(1) == 0)
    def _():
        st_s[...] = s0_ref[...]
    q = gl_ref[:, 0:GLA_QK] * (GLA_DK ** -0.5)
    k = gl_ref[:, GLA_QK:2 * GLA_QK]
    v = gl_ref[:, 2 * GLA_QK:2 * GLA_QK + GLA_WIDTH].astype(BF16)
    gz = gl_ref[:, 2 * GLA_QK + GLA_WIDTH:]
    lg = jnp.dot(gt_ref[...].astype(BF16), wg_ref[...], preferred_element_type=F32) + bg_ref[...]
    la = (jnp.minimum(lg, 0.0) - jnp.log(1.0 + jnp.exp(-jnp.abs(lg)))) * (1.0 / GLA_GATE_TAU)
    pall = pall_ref[...]
    cums = sum(jnp.dot(pall, piece, preferred_element_type=F32) for piece in _split3(la))
    cum = cums[0:r]
    cl = cums[(nl + 1) * r:(nl + 2) * r]
    lane_qk = lax.broadcasted_iota(jnp.int32, (1, GLA_QK), 1) // GLA_DK
    lane_v = lax.broadcasted_iota(jnp.int32, (1, GLA_WIDTH), 1) // GLA_DV

    def stack(x):
        return jnp.concatenate([jnp.where(lane_qk == h, x, 0.0) for h in range(GLA_HEADS)], axis=0).astype(BF16)

    def level(l, qf, kf):
        s = lax.dot_general(stack(qf), kf.astype(BF16), (((1,), (1,)), ((), ())), preferred_element_type=F32)
        return (s.reshape(GLA_HEADS, r, r) * lm_ref[l][None]).reshape(GLA_HEADS * r, r)
    att = level(0, q, k)
    for l in range(1, nl + 1):
        cm = cums[l * r:(l + 1) * r]
        att = att + level(l, q * jnp.exp(jnp.minimum(cum - cm, 0.0)), k * jnp.exp(jnp.minimum(cm - cum, 0.0)))
    res = jnp.dot(att.astype(BF16), v, preferred_element_type=F32)
    o = sum(jnp.where(lane_v == h, res[h * r:(h + 1) * r], 0.0) for h in range(GLA_HEADS))
    qd = (q * jnp.exp(cum)).astype(BF16)
    kd = k * jnp.exp(cl - cum)
    bd = (lax.broadcasted_iota(jnp.int32, (GLA_QK, GLA_WIDTH), 0) // GLA_DK
          == lax.broadcasted_iota(jnp.int32, (GLA_QK, GLA_WIDTH), 1) // GLA_DV)
    row_seq = lax.broadcasted_iota(jnp.int32, (r, 1), 0) // t
    inter = []
    for i in range(nseq):
        st = st_s[i]
        inter.append(jnp.dot(qd[i * t:(i + 1) * t], st.astype(BF16), preferred_element_type=F32))
        kdi = kd if nseq == 1 else jnp.where(row_seq == i, kd, 0.0)
        upd = jnp.dot(jnp.transpose(kdi).astype(BF16), v, preferred_element_type=F32)
        a_col = jnp.transpose(jnp.broadcast_to(jnp.exp(cl[i * t:i * t + 1, :]), (GLA_QK, GLA_QK)))
        st_s[i] = jnp.where(bd, _lane_tile(a_col, GLA_WIDTH // GLA_QK) * st + upd, 0.0)
    o = o + (inter[0] if nseq == 1 else jnp.concatenate(inter, axis=0))
    sout_ref[...] = st_s[...]
    hi = (o * o).astype(BF16)
    lo = (o * o - hi.astype(F32)).astype(BF16)
    ms = (jnp.dot(hi, seg_ref[...], preferred_element_type=F32) + jnp.dot(lo, seg_ref[...], preferred_element_type=F32))
    o_ref[...] = o * lax.rsqrt(ms + RMS_EPS) * gn_ref[...] * (gz * jax.nn.sigmoid(gz))


def _gla_tables(r, t, nl):
    idx = jnp.arange(r)
    seq = idx // t
    rows = [(seq[:, None] == seq[None, :]) & (idx[None, :] <= idx[:, None])]
    masks = [idx[:, None] == idx[None, :]]
    for l in range(1, nl + 1):
        mid = (idx >> l << l) + (1 << (l - 1)) - 1
        rows.append((seq[:, None] == seq[None, :]) & (idx[None, :] <= mid[:, None]))
        masks.append(((idx[:, None] >> l) == (idx[None, :] >> l)) & (((idx[:, None] >> (l - 1)) & 1) == 1)
                     & (((idx[None, :] >> (l - 1)) & 1) == 0))
    rows.append(seq[:, None] == seq[None, :])
    return jnp.concatenate(rows, 0).astype(BF16), jnp.stack(masks).astype(F32)


def _gla(gl, gt, lw, s0, nb, nchunk, r, t):
    n = gl.shape[0]
    nseq = r // t
    nl = t.bit_length() - 1
    pall, lm = _gla_tables(r, t, nl)
    wg = jnp.zeros((GT_W, GLA_QK), F32).at[3 * NSA_HEADS:3 * NSA_HEADS + GLA_GATE_RANK].set(lw['gla_wg']).astype(BF16)
    eye = jnp.eye(GLA_HEADS, dtype=F32)
    seg = (jnp.einsum('gh,de->gdhe', eye, jnp.ones((GLA_DV, GLA_DV), F32)) / GLA_DV).reshape(GLA_WIDTH, GLA_WIDTH)
    s0bd = jnp.einsum('gh,bgde->bgdhe', eye, s0.astype(F32)).reshape(-1, GLA_QK, GLA_WIDTH)
    row = lambda bi, ci: (bi * nchunk + ci, 0)
    const2 = lambda bi, ci: (0, 0)
    o, sfin = pl.pallas_call(
        functools.partial(_gla_kernel, r=r, t=t, nl=nl),
        out_shape=[jax.ShapeDtypeStruct((n, GLA_WIDTH), F32),
                   jax.ShapeDtypeStruct((nb * nseq, GLA_QK, GLA_WIDTH), F32)],
        grid=(nb, nchunk),
        in_specs=[pl.BlockSpec((r, GL_W), row), pl.BlockSpec((r, GT_W), row),
                  pl.BlockSpec((GT_W, GLA_QK), const2), pl.BlockSpec((1, GLA_QK), const2),
                  pl.BlockSpec((1, GLA_WIDTH), const2), pl.BlockSpec(((nl + 2) * r, r), const2),
                  pl.BlockSpec((nl + 1, r, r), lambda bi, ci: (0, 0, 0)),
                  pl.BlockSpec((GLA_WIDTH, GLA_WIDTH), const2),
                  pl.BlockSpec((nseq, GLA_QK, GLA_WIDTH), lambda bi, ci: (bi, 0, 0))],
        out_specs=[pl.BlockSpec((r, GLA_WIDTH), row),
                   pl.BlockSpec((nseq, GLA_QK, GLA_WIDTH), lambda bi, ci: (bi, 0, 0))],
        scratch_shapes=[pltpu.VMEM((nseq, GLA_QK, GLA_WIDTH), F32)],
        compiler_params=_params("parallel", "arbitrary"),
        name="gla",
    )(gl, gt, wg, lw['gla_bg'].astype(F32).reshape(1, GLA_QK),
      jnp.tile(lw['gla_norm'].astype(F32), GLA_HEADS).reshape(1, GLA_WIDTH), pall, lm, seg.astype(BF16), s0bd)
    sf = sfin.reshape(-1, GLA_HEADS, GLA_DK, GLA_HEADS, GLA_DV)
    state = jnp.stack([sf[:, h, :, h, :] for h in range(GLA_HEADS)], axis=1)
    return o, state


def _merge_kernel(h_ref, oc_ref, os_ref, ow_ref, nz_ref, og_ref, o5_ref, ple_ref, wo_ref, pn_ref, wgate_ref, wproj_ref,
                  fn_ref, out_ref, *, final):
    nz = nz_ref[...]
    o_nsa = (oc_ref[...] + os_ref[...] + ow_ref[...]) * (nz * jax.nn.sigmoid(nz))
    mix = jnp.concatenate([o_nsa, og_ref[...], o5_ref[...]], axis=1).astype(BF16)
    h = h_ref[...] + jnp.dot(mix, wo_ref[...], preferred_element_type=F32)

    def norm(x, g_ref):
        return x * lax.rsqrt(jnp.mean(x * x, axis=-1, keepdims=True) + RMS_EPS) * g_ref[...]
    gate = jax.nn.sigmoid(jnp.dot(norm(h, pn_ref).astype(BF16), wgate_ref[...], preferred_element_type=F32))
    h = h + gate * jnp.dot(ple_ref[...].astype(BF16), wproj_ref[...], preferred_element_type=F32)
    out_ref[...] = norm(h, fn_ref) if final else h


def _merge_call(h2d, o_cmp, o_slc, o_win, nz, o_gla, o_s5, s5_time_major, ple2d, lw, final_norm, nb):
    n, d = h2d.shape
    l = n // nb
    tm = min(512, l)
    nt = l // tm
    row = lambda bi, ti: (bi * nt + ti, 0)
    const = lambda bi, ti: (0, 0)
    o5_spec = pl.BlockSpec((tm, S5_WIDTH), (lambda bi, ti: (ti, bi)) if s5_time_major else row)
    fn = jnp.ones((d,), F32) if final_norm is None else final_norm
    return pl.pallas_call(
        functools.partial(_merge_kernel, final=final_norm is not None),
        out_shape=jax.ShapeDtypeStruct((n, d), F32),
        grid=(nb, nt),
        in_specs=[pl.BlockSpec((tm, d), row)] + [pl.BlockSpec((tm, NSA_WIDTH), row)] * 4 +
                 [pl.BlockSpec((tm, GLA_WIDTH), row), o5_spec, pl.BlockSpec((tm, PLE_DIM), row),
                  pl.BlockSpec((MIX_WIDTH, d), const), pl.BlockSpec((1, d), const), pl.BlockSpec((d, d), const),
                  pl.BlockSpec((PLE_DIM, d), const), pl.BlockSpec((1, d), const)],
        out_specs=pl.BlockSpec((tm, d), row),
        compiler_params=_params("parallel", "parallel"),
        name="merge",
    )(h2d, o_cmp, o_slc, o_win, nz, o_gla, o_s5, ple2d, lw['w_out'].astype(BF16),
      lw['ple_norm'].astype(F32).reshape(1, d), lw['ple_w_gate'].astype(BF16), lw['ple_w_proj'].astype(BF16),
      fn.astype(F32).reshape(1, d))


def _rmsnorm(x, g):
    xf = x.astype(jnp.float32)
    y = xf * lax.rsqrt(jnp.mean(xf * xf, axis=-1, keepdims=True) + RMS_EPS)
    return (y * g.astype(jnp.float32)).astype(x.dtype)


def _masked_softmax(s, mask):
    s = jnp.where(mask, s, -1e30)
    e = jnp.where(mask, jnp.exp(s - jnp.max(s, axis=-1, keepdims=True)), 0.0)
    return e / jnp.maximum(jnp.sum(e, axis=-1, keepdims=True), 1e-30)


def _alibi_slopes():
    return jnp.exp2(-8.0 * jnp.arange(1, NSA_HEADS + 1, dtype=jnp.float32) / NSA_HEADS)


def _compress_kv(x, pe, w1, w2):
    b, t, g, d = x.shape
    ratio = CMP_BLOCK // CMP_STRIDE
    n_ch = t // CMP_STRIDE
    n_cmp = n_ch - ratio + 1
    ch = x[:, :n_ch * CMP_STRIDE].reshape(b, n_ch, CMP_STRIDE, g, d)
    blocks = jnp.concatenate([ch[:, s:s + n_cmp] for s in range(ratio)], axis=2)
    blocks = blocks + pe[None, None, :, None, :]
    flat = jnp.moveaxis(blocks, 2, 3).reshape(b, n_cmp, g, CMP_BLOCK * d)
    return jax.nn.gelu(flat @ w1) @ w2


def _nsa_sparse_block(q, qpos, kc, vc, cend, ksb, vsb, slopes):
    f32 = jnp.float32
    b, qb = q.shape[:2]
    sl = slopes.reshape(NSA_KV_HEADS, NSA_HPG)
    scale = HEAD_DIM ** -0.5
    s = jnp.einsum('bqghd,bngd->bqghn', q, kc).astype(f32) * scale
    dist = (qpos[:, None] - cend[None, :]).astype(f32)
    s = s - sl[:, :, None] * dist[:, None, None, :]
    p = _masked_softmax(s, (cend[None, :] <= qpos[:, None])[:, None, None, :])
    o_cmp = jnp.einsum('bqghn,bngd->bqghd', p.astype(vc.dtype), vc)
    imp = p.sum(axis=3)
    n_cmp = imp.shape[-1]
    n_slc = ksb.shape[2]
    run = imp
    for sft in range(1, CMP_BLOCK // CMP_STRIDE):
        run = run + jnp.pad(imp, ((0, 0), (0, 0), (0, 0), (sft, 0)))[..., :n_cmp]
    ratio = SLC_BLOCK // CMP_STRIDE
    run = jnp.pad(run, ((0, 0), (0, 0), (0, 0), (0, ratio * n_slc - n_cmp)))
    score = run.reshape(b, qb, NSA_KV_HEADS, n_slc, ratio).sum(-1)
    blk = jnp.arange(n_slc)
    cur = qpos // SLC_BLOCK
    eligible = blk[None, :] * SLC_BLOCK <= qpos[:, None]
    forced = (blk[None, :] == 0) | (blk[None, :] == cur[:, None]) | (blk[None, :] == cur[:, None] - 1)
    score = jnp.where(forced[:, None, :], NEG_BIG, score)
    score = jnp.where(eligible[:, None, :], score, -NEG_BIG)
    top_val, top_idx = lax.top_k(score, min(N_SELECT, n_slc))
    sel_ok = top_val > -0.5 * NEG_BIG
    b_ix = jnp.arange(b)[:, None, None, None]
    g_ix = jnp.arange(NSA_KV_HEADS)[None, None, :, None]
    ksel = ksb[b_ix, g_ix, top_idx]
    vsel = vsb[b_ix, g_ix, top_idx]
    kpos = top_idx[..., None] * SLC_BLOCK + jnp.arange(SLC_BLOCK)
    s2 = jnp.einsum('bqghd,bqgkjd->bqghkj', q, ksel).astype(f32) * scale
    dist2 = (qpos[None, :, None, None, None] - kpos).astype(f32)
    s2 = s2 - sl[None, None, :, :, None, None] * dist2[:, :, :, None]
    ok2 = (sel_ok[..., None] & (dist2 >= 0))[:, :, :, None]
    shp = s2.shape
    nk = shp[4] * shp[5]
    p2 = _masked_softmax(s2.reshape(shp[:4] + (nk,)), ok2.reshape(ok2.shape[:4] + (nk,))).reshape(shp)
    o_slc = jnp.einsum('bqghkj,bqgkjd->bqghd', p2.astype(vsel.dtype), vsel)
    return o_cmp, o_slc


def _window_attn(q, qpos, k, v, kpos, slopes):
    sl = slopes.reshape(NSA_KV_HEADS, NSA_HPG)
    s = jnp.einsum('bqghd,bkgd->bqghk', q, k).astype(jnp.float32) * HEAD_DIM ** -0.5
    dist = (qpos[:, None] - kpos[None, :]).astype(jnp.float32)
    s = s - sl[:, :, None] * dist[:, None, None, :]
    ok = ((dist >= 0) & (dist < WINDOW) & (kpos[None, :] >= 0))[:, None, None, :]
    p = _masked_softmax(s, ok)
    return jnp.einsum('bqghk,bkgd->bqghd', p.astype(v.dtype), v)


def _gla_chunked(q, k, v, log_a, s0):
    f32 = jnp.float32
    b, l = q.shape[:2]
    c = min(GLA_CHUNK, l)
    pad = (-l) % c
    n = (l + pad) // c

    def prep(t):
        t = jnp.pad(t.astype(f32), ((0, 0), (0, pad), (0, 0), (0, 0)))
        return t.reshape(b, n, c, GLA_HEADS, t.shape[-1])
    q = prep(q) * GLA_DK ** -0.5
    k, v, log_a = prep(k), prep(v), prep(log_a)
    cum = jnp.cumsum(log_a, axis=2)
    causal = jnp.tril(jnp.ones((c, c), dtype=bool))[:, :, None, None]
    diff = cum[:, :, :, None] - cum[:, :, None, :]
    decay = jnp.where(causal, jnp.exp(jnp.where(causal, diff, 0.0)), 0.0)
    att = jnp.einsum('bnthd,bnshd,bntshd->bnhts', q, k, decay)
    o_intra = jnp.einsum('bnhts,bnshe->bnthe', att, v)
    q_dec = q * jnp.exp(cum)
    k_dec = k * jnp.exp(cum[:, :, -1:] - cum)
    a_tot = jnp.exp(cum[:, :, -1])

    def step(state, inp):
        qd, kd, vv, at = inp
        o = jnp.einsum('bchd,bhde->bche', qd, state)
        state = at[..., None] * state + jnp.einsum('bchd,bche->bhde', kd, vv)
        return state, o
    xs = (jnp.moveaxis(q_dec, 1, 0), jnp.moveaxis(k_dec, 1, 0), jnp.moveaxis(v, 1, 0), jnp.moveaxis(a_tot, 1, 0))
    s_fin, o_inter = lax.scan(step, s0.astype(f32), xs)
    o = o_intra + jnp.moveaxis(o_inter, 0, 1)
    return o.reshape(b, n * c, GLA_HEADS, GLA_DV)[:, :l], s_fin


def _s5_combine(e1, e2):
    a1r, a1i, b1r, b1i = e1
    a2r, a2i, b2r, b2i = e2
    return (a2r * a1r - a2i * a1i, a2r * a1i + a2i * a1r,
            a2r * b1r - a2i * b1i + b2r, a2r * b1i + a2i * b1r + b2i)


def _s5_scan(u, h0, a_re, a_im, b_re, b_im, c_re, c_im, d, log_dt):
    f32 = jnp.float32
    b, l, _ = u.shape
    dt = jnp.exp(log_dt.astype(f32))[:, None]
    mag = jnp.exp(a_re * dt)
    abr, abi = mag * jnp.cos(a_im * dt), mag * jnp.sin(a_im * dt)
    den = a_re * a_re + a_im * a_im
    nr = abr - 1.0
    cr = (nr * a_re + abi * a_im) / den
    ci = (abi * a_re - nr * a_im) / den
    bbr = cr[..., None] * b_re - ci[..., None] * b_im
    bbi = cr[..., None] * b_im + ci[..., None] * b_re
    ug = u.reshape(b, l, S5_GROUPS, S5_GROUP_CH)
    xr = jnp.einsum('gpc,blgc->blgp', bbr, ug)
    xi = jnp.einsum('gpc,blgc->blgp', bbi, ug)
    h0r, h0i = h0[..., 0].astype(f32), h0[..., 1].astype(f32)
    xr = xr.at[:, 0].add(abr * h0r - abi * h0i)
    xi = xi.at[:, 0].add(abr * h0i + abi * h0r)
    ar = jnp.broadcast_to(abr, xr.shape)
    ai = jnp.broadcast_to(abi, xr.shape)
    _, _, hr, hi = lax.associative_scan(_s5_combine, (ar, ai, xr, xi), axis=1)
    y = jnp.einsum('gcp,blgp->blgc', c_re, hr) - jnp.einsum('gcp,blgp->blgc', c_im, hi)
    y = y.reshape(b, l, S5_WIDTH) + d.astype(f32) * u
    return y, jnp.stack([hr[:, -1], hi[:, -1]], axis=-1)


GLA_CHUNK_ROWS = 256


def _split_kv(kv, b, l):
    kv5 = kv.reshape(b, l, 3, 2, NSA_KV_HEADS, HEAD_DIM)
    return kv5[:, :, 0], kv5[:, :, 1], kv5[:, :, 2]


def _prompt_layer(h2d, ple2d, lw, b, l, final_norm):
    qx, kv, kb, vb, nz, gl, gt, su, sz = _in_proj_b(h2d, b, l, lw['norm'], lw['w_in_b'])
    kvc = _compress(kv, b, l, lw['cmp_pe'], lw['cmp_w1'], lw['cmp_w2'])
    o_cmp, sel = _cmp_select(qx, kvc, gt, b, l)
    o_slc = _flash(qx, kb, vb, sel, gt, b, l, "slc")
    o_win = _flash(qx, kb, vb, sel, gt, b, l, "win")
    new_cmp, new_slc, new_win = _split_kv(kv, b, l)
    win_state = new_win[:, l - min(WINDOW, l):]

    t = min(GLA_CHUNK_ROWS, l)
    o_gla, gla_state = _gla(gl, gt, lw, jnp.zeros((b, GLA_HEADS, GLA_DK, GLA_DV), F32), b, l // t, t, t)

    o_s5, hfin = _s5(su, sz, lw, jnp.zeros((2, b * S5_CH), F32), b, l)
    s5_state = jnp.transpose(hfin.reshape(2, b, S5_GROUPS, S5_STATE), (1, 2, 3, 0))

    h2d = _merge_call(h2d, o_cmp, o_slc, o_win, nz, o_gla, o_s5, True, ple2d, lw, final_norm, b)
    return h2d, (new_cmp, new_slc, win_state, gla_state, s5_state)


def _sample_layer(h2d, ple2d, lw, past, b, l, final_norm):
    f32 = jnp.float32
    n = b * l
    qx, kv, kb, vb, nz, gl, gt, su, sz = _in_proj_b(h2d, 1, n, lw['norm'], lw['w_in_b'])
    su, sz = [jnp.transpose(a.reshape(b, l, S5_WIDTH), (1, 0, 2)).reshape(l, b * S5_WIDTH) for a in (su, sz)]
    new_cmp, new_slc, new_win = _split_kv(kv, b, l)
    pt = past['page_table']
    pos0 = pt.shape[1] * past['cmp_t'].shape[-1]
    assert (pos0 + l) // CMP_STRIDE == pos0 // CMP_STRIDE and pos0 % CMP_STRIDE == 0
    assert past['win'].shape[1] == WINDOW <= pos0
    kvc = _paged_compress(past['cmp_t'], past['layer'], pt, lw['cmp_pe'], lw['cmp_w1'], lw['cmp_w2'])
    gated = _decode_nsa(past['slc_t'], past['win_t'], past['layer'], pt, kvc, qx.astype(f32), kv, gt, b, l)
    win_state = jnp.concatenate([past['win'], new_win], axis=1)[:, l:]

    seqs = GLA_SAMPLE_ROWS // l
    o_gla, gla_state = _gla(gl, gt, lw, past['gla'], b // seqs, 1, GLA_SAMPLE_ROWS, l)

    h0 = jnp.transpose(past['s5'].astype(f32), (3, 0, 1, 2)).reshape(2, b * S5_CH)
    o_s5, hfin = _s5(su, sz, lw, h0, b, l)
    s5_state = jnp.transpose(hfin.reshape(2, b, S5_GROUPS, S5_STATE), (1, 2, 3, 0))

    o_s5 = jnp.transpose(o_s5.reshape(l, b, S5_WIDTH), (1, 0, 2)).reshape(n, S5_WIDTH)
    h2d = _merge_call(h2d, gated[0], gated[1], gated[2], nz, o_gla, o_s5, False, ple2d, lw, final_norm, 1)
    return h2d, (new_cmp, new_slc, win_state, gla_state, s5_state)


GLA_SAMPLE_ROWS = 128


def kernel(x_prompt, x_sample, cache_cmp, cache_slc, cache_win, state_gla, state_s5, page_table,
           p_prompt, p_sample, norm_mix, w_in, w_out, cmp_pe, cmp_w1, cmp_w2, gla_wg, gla_bg, gla_norm,
           s5_a_re, s5_a_im, s5_b_re, s5_b_im, s5_c_re, s5_c_im, s5_d, s5_log_dt, s5_w_glu,
           ple_norm, ple_w_gate, ple_w_proj, final_norm):
    depth = w_in.shape[0]
    w_in_b = _relayout_w_in_b(w_in)
    cmp_t = jnp.transpose(cache_cmp, (0, 1, 3, 4, 5, 2))
    slc_t = jnp.transpose(cache_slc, (0, 1, 3, 4, 5, 2))
    win_t = jnp.transpose(cache_win, (0, 1, 3, 4, 5, 2))
    bp, lp, d = x_prompt.shape
    bs, ls, _ = x_sample.shape
    hp, hs = x_prompt.reshape(bp * lp, d), x_sample.reshape(bs * ls, d)
    st_p, st_s = [], []
    for i in range(depth):
        fin = final_norm if i == depth - 1 else None
        lw = {'norm': norm_mix[i], 'w_in_b': w_in_b[i], 'w_out': w_out[i],
              'cmp_pe': cmp_pe[i], 'cmp_w1': cmp_w1[i], 'cmp_w2': cmp_w2[i],
              'gla_wg': gla_wg[i], 'gla_bg': gla_bg[i], 'gla_norm': gla_norm[i],
              's5_a_re': s5_a_re[i], 's5_a_im': s5_a_im[i], 's5_b_re': s5_b_re[i], 's5_b_im': s5_b_im[i],
              's5_c_re': s5_c_re[i], 's5_c_im': s5_c_im[i], 's5_d': s5_d[i], 's5_log_dt': s5_log_dt[i],
              's5_w_glu': s5_w_glu[i], 'ple_norm': ple_norm[i], 'ple_w_gate': ple_w_gate[i],
              'ple_w_proj': ple_w_proj[i]}
        hp, sp = _prompt_layer(hp, p_prompt[i].reshape(bp * lp, PLE_DIM), lw, bp, lp, fin)
        past = {'page_table': page_table, 'win': cache_win[i], 'gla': state_gla[i], 's5': state_s5[i],
                'cmp_t': cmp_t, 'slc_t': slc_t, 'win_t': win_t, 'layer': i}
        hs, ss = _sample_layer(hs, p_sample[i].reshape(bs * ls, PLE_DIM), lw, past, bs, ls, fin)
        st_p.append(sp)
        st_s.append(ss)
    outs = [hp.reshape(bp, lp, d), hs.reshape(bs, ls, d)]
    for j in range(5):
        outs.append(jnp.stack([s[j] for s in st_p]))
        outs.append(jnp.stack([s[j] for s in st_s]))
    return tuple(outs)
```
